```python
import jax, jax.numpy as jnp
from jax import lax
import numpy as np

D_MODEL = 1024
BATCH = 2
SEQ = 16384
DEPTH = 2

GRID_W = 64
CTX_LEN = 256
POOL_WINDOWS = (2, 4, 8, 16)
N_POOL_GROUPS = 4
POOL_GROUP_W = D_MODEL // 8
POOL_W = N_POOL_GROUPS * POOL_GROUP_W
HEAD_DIM = 64
N_HEADS = D_MODEL // (2 * HEAD_DIM)
N_KV_HEADS = N_HEADS // 4
Q_REP = N_HEADS // N_KV_HEADS
ATTN_W = N_HEADS * HEAD_DIM
KV_W = N_KV_HEADS * HEAD_DIM
WINDOW = 128
BLOCK = 128
ROPE_BASE = 10000.0
N_BRANCHES = 2
COL_POOL = 0
COL_Q = COL_POOL + POOL_W
COL_K = COL_Q + ATTN_W
COL_V = COL_K + KV_W
COL_GATE = COL_V + KV_W
IN_COLS = COL_GATE + N_BRANCHES * D_MODEL
N_EXPERTS = 32
N_EXPERT_GROUPS = 8
EXPERTS_PER_GROUP = N_EXPERTS // N_EXPERT_GROUPS
TOP_K = 2
D_EXPERT = D_MODEL
EXPERT_BLOCK = 256
DEEPNORM_ALPHA = (2 * DEPTH) ** 0.25
DEEPNORM_BETA = (8 * DEPTH) ** -0.25
LN_EPS = 1e-6

kernel_name = 'hybrid_pool_swa_moe_dit_block'


def layer_norm(x, gain=None, bias=None):
    xf = x.astype(jnp.float32)
    xc = xf - xf.mean(-1, keepdims=True)
    y = xc * lax.rsqrt((xc * xc).mean(-1, keepdims=True) + LN_EPS)
    if gain is not None:
        y = y * gain.astype(jnp.float32) + bias.astype(jnp.float32)
    return y.astype(x.dtype)


def modulate(x, shift, scale):
    return layer_norm(x) * (1 + scale) + shift


def ada_terms(cond, w_ada, b_ada):
    return jnp.split(jax.nn.silu(cond) @ w_ada + b_ada, 6, axis=-1)


def axial_rope_tables(rows):
    row = jnp.repeat(jnp.arange(rows), GRID_W).astype(jnp.float32)
    col = jnp.tile(jnp.arange(GRID_W), rows).astype(jnp.float32)
    half = HEAD_DIM // 2
    inv_freq = ROPE_BASE ** (-jnp.arange(0, half, 2, dtype=jnp.float32) / half)
    ang_r = row[:, None] * inv_freq
    ang_c = col[:, None] * inv_freq
    return (jnp.cos(ang_r), jnp.sin(ang_r), jnp.cos(ang_c), jnp.sin(ang_c))


def rotate(x, cos, sin):
    x1, x2 = jnp.split(x, 2, axis=-1)
    c = cos[:, None, :].astype(x.dtype)
    s = sin[:, None, :].astype(x.dtype)
    return jnp.concatenate([x1 * c - x2 * s, x2 * c + x1 * s], axis=-1)


def apply_axial_rope(x, tables):
    cr, sr, cc, sc = tables
    x_row, x_col = jnp.split(x, 2, axis=-1)
    return jnp.concatenate([rotate(x_row, cr, sr), rotate(x_col, cc, sc)], axis=-1)


def sink_softmax(s, sink):
    m = jnp.maximum(jnp.max(s, axis=-1, keepdims=True), sink)
    e = jnp.exp(s - m)
    return e / (jnp.sum(e, axis=-1, keepdims=True) + jnp.exp(sink - m))


def multiscale_pool(u, w_grp, scale):
    B, L, _ = u.shape
    cs = jnp.pad(jnp.cumsum(u.astype(jnp.float32), axis=1), ((0, 0), (1, 0), (0, 0)))
    t = jnp.arange(L)
    means = []
    for g, w in enumerate(POOL_WINDOWS):
        lo = jnp.clip(t - w // 2, 0, L - 1)
        hi = jnp.clip(t - w // 2 + w - 1, 0, L - 1)
        csg = cs[:, :, g * POOL_GROUP_W:(g + 1) * POOL_GROUP_W]
        cnt = (hi - lo + 1).astype(jnp.float32)[None, :, None]
        means.append((jnp.take(csg, hi + 1, axis=1) - jnp.take(csg, lo, axis=1)) / cnt)
    pooled = jnp.concatenate(means, axis=-1).astype(u.dtype) - u
    y = jnp.einsum('blgc,gcd->blgd', pooled.reshape(B, L, N_POOL_GROUPS, POOL_GROUP_W), w_grp)
    return y.reshape(B, L, POOL_W) * scale


def windowed_gqa(q, k, v, kc, vc, sink):
    B, L = q.shape[:2]
    nb = L // BLOCK
    qb = q.reshape(B, nb, BLOCK, N_KV_HEADS, Q_REP, HEAD_DIM)

    def band(t):
        tp = jnp.pad(t, ((0, 0), (BLOCK, BLOCK), (0, 0), (0, 0))).reshape(B, nb + 2, BLOCK, N_KV_HEADS, HEAD_DIM)
        return jnp.concatenate([tp[:, :-2], tp[:, 1:-1], tp[:, 2:]], axis=2)

    kb, vb = band(k), band(v)
    scale = HEAD_DIM ** -0.5
    s_loc = jnp.einsum('bnqgrd,bnkgd->bngrqk', qb, kb).astype(jnp.float32) * scale
    s_ctx = jnp.einsum('bnqgrd,bcgd->bngrqc', qb, kc).astype(jnp.float32) * scale
    blk = jnp.arange(nb)[:, None, None] * BLOCK
    qpos = blk + jnp.arange(BLOCK)[None, :, None]
    kpos = blk - BLOCK + jnp.arange(3 * BLOCK)[None, None, :]
    valid = (jnp.abs(qpos - kpos) <= WINDOW) & (kpos >= 0) & (kpos < L)
    s_loc = jnp.where(valid[None, :, None, None], s_loc, -jnp.inf)
    sk = sink.astype(jnp.float32).reshape(N_KV_HEADS, Q_REP, 1, 1)
    p = sink_softmax(jnp.concatenate([s_loc, s_ctx], axis=-1), sk).astype(v.dtype)
    nk = 3 * BLOCK
    o = (jnp.einsum('bngrqk,bnkgd->bnqgrd', p[..., :nk], vb)
         + jnp.einsum('bngrqc,bcgd->bnqgrd', p[..., nk:], vc))
    return o.reshape(B, L, ATTN_W)


def context_gqa(qc, kc, vc, sink):
    B, C = qc.shape[:2]
    qg = qc.reshape(B, C, N_KV_HEADS, Q_REP, HEAD_DIM)
    s = jnp.einsum('bqgrd,bkgd->bgrqk', qg, kc).astype(jnp.float32) * HEAD_DIM ** -0.5
    sk = sink.astype(jnp.float32).reshape(N_KV_HEADS, Q_REP, 1, 1)
    p = sink_softmax(s, sk).astype(vc.dtype)
    return jnp.einsum('bgrqk,bkgd->bqgrd', p, vc).reshape(B, C, ATTN_W)


def merge_branches(gate_cols, pool_out, attn_out, w_pool_br, w_attn_br, w_o):
    g_pool, g_attn = jnp.split(jax.nn.sigmoid(gate_cols), N_BRANCHES, axis=-1)
    return (g_pool * (pool_out @ w_pool_br) + g_attn * (attn_out @ w_attn_br)) @ w_o


def token_mixer(h, hc, rope, w_in, w_pool_grp, pool_scale, w_pool_br, w_attn_br, sink, w_o, ctx_out):
    B, L, _ = h.shape
    C = hc.shape[1]
    p = h @ w_in
    q = apply_axial_rope(p[..., COL_Q:COL_K].reshape(B, L, N_HEADS, HEAD_DIM), rope)
    k = apply_axial_rope(p[..., COL_K:COL_V].reshape(B, L, N_KV_HEADS, HEAD_DIM), rope)
    v = p[..., COL_V:COL_GATE].reshape(B, L, N_KV_HEADS, HEAD_DIM)
    if ctx_out:
        pc = hc @ w_in
        kvc = pc[..., COL_K:COL_GATE]
    else:
        kvc = hc @ w_in[:, COL_K:COL_GATE]
    kc = kvc[..., :KV_W].reshape(B, C, N_KV_HEADS, HEAD_DIM)
    vc = kvc[..., KV_W:].reshape(B, C, N_KV_HEADS, HEAD_DIM)
    pool_lat = multiscale_pool(p[..., COL_POOL:COL_Q], w_pool_grp, pool_scale)
    attn_lat = windowed_gqa(q, k, v, kc, vc, sink)
    y = merge_branches(p[..., COL_GATE:], pool_lat, attn_lat, w_pool_br, w_attn_br, w_o)
    if not ctx_out:
        return y, None
    pool_c = multiscale_pool(pc[..., COL_POOL:COL_Q], w_pool_grp, pool_scale)
    attn_c = context_gqa(pc[..., COL_Q:COL_K].reshape(B, C, N_HEADS, HEAD_DIM), kc, vc, sink)
    yc = merge_branches(pc[..., COL_GATE:], pool_c, attn_c, w_pool_br, w_attn_br, w_o)
    return y, yc


def route(h, w_router, router_bias):
    N = h.shape[0]
    scores = jax.nn.sigmoid((h @ w_router).astype(jnp.float32))
    biased = (scores + router_bias.astype(jnp.float32)).reshape(N, N_EXPERT_GROUPS, EXPERTS_PER_GROUP)
    group_score = lax.top_k(biased, 2)[0].sum(-1)
    g_sel = jnp.argmax(group_score, axis=-1).astype(jnp.int32)
    in_group = jnp.take_along_axis(biased, g_sel[:, None, None], axis=1)[:, 0]
    _, local = lax.top_k(in_group, TOP_K)
    experts = (g_sel[:, None] * EXPERTS_PER_GROUP + local).astype(jnp.int32)
    w = jnp.take_along_axis(scores, experts, axis=1)
    return experts, (w / w.sum(-1, keepdims=True)).astype(h.dtype)


def expert_ffn(h, experts, weights, w_gate, w_up, w_down):
    N, D = h.shape
    M = N * TOP_K
    flat_e = experts.reshape(M)
    flat_tok = jnp.repeat(jnp.arange(N, dtype=jnp.int32), TOP_K)
    order = jnp.argsort(flat_e)
    e_sorted = flat_e[order]
    counts = jnp.bincount(flat_e, length=N_EXPERTS)
    padded = (counts + EXPERT_BLOCK - 1) // EXPERT_BLOCK * EXPERT_BLOCK
    starts = jnp.cumsum(counts) - counts
    pends = jnp.cumsum(padded)
    pstarts = pends - padded
    dest = (pstarts[e_sorted] + jnp.arange(M) - starts[e_sorted]).astype(jnp.int32)
    n_rows = -(-M // EXPERT_BLOCK) * EXPERT_BLOCK + N_EXPERTS * EXPERT_BLOCK
    n_blocks = n_rows // EXPERT_BLOCK
    row_tok = jnp.full((n_rows,), N, jnp.int32).at[dest].set(flat_tok[order])
    block_expert = jnp.minimum(
        jnp.searchsorted(pends, jnp.arange(n_blocks) * EXPERT_BLOCK, side='right'), N_EXPERTS - 1).astype(jnp.int32)
    h_pad = jnp.concatenate([h, jnp.zeros((1, D), h.dtype)], axis=0)
    xb = h_pad[row_tok].reshape(n_blocks, EXPERT_BLOCK, D)

    def run_block(args):
        xblk, e = args
        return (jax.nn.silu(xblk @ w_gate[e]) * (xblk @ w_up[e])) @ w_down[e]

    yb = lax.map(run_block, (xb, block_expert)).reshape(n_rows, D)
    slot = jnp.zeros((M,), jnp.int32).at[order].set(dest)
    return jnp.einsum('nkd,nk->nd', yb[slot].reshape(N, TOP_K, D), weights)


def setup_inputs(seed: int = 0) -> dict:
    key = jax.random.key(seed)
    ks = jax.random.split(key, 24)

    def nrm(k, shape, s):
        return jax.random.normal(k, shape, jnp.float32) * s

    D = D_MODEL
    w_in = nrm(ks[4], (DEPTH, D, IN_COLS), D ** -0.5)
    w_in = w_in.at[:, :, COL_V:COL_GATE].multiply(DEEPNORM_BETA)
    return {
        'x': nrm(ks[0], (BATCH, SEQ, D), 1.0),
        'c': nrm(ks[1], (BATCH, D), 1.0),
        'ctx': nrm(ks[2], (BATCH, CTX_LEN, D), 1.0),
        'c_ctx': nrm(ks[3], (D,), 1.0),
        'w_ada': nrm(ks[5], (DEPTH, D, 6 * D), 0.5 * D ** -0.5),
        'b_ada': nrm(ks[6], (DEPTH, 6 * D), 0.02),
        'w_in': w_in,
        'w_pool_grp': nrm(ks[7], (DEPTH, N_POOL_GROUPS, POOL_GROUP_W, POOL_GROUP_W), POOL_GROUP_W ** -0.5),
        'pool_scale': 1.0 + nrm(ks[8], (DEPTH, POOL_W), 0.1),
        'w_pool_br': nrm(ks[9], (DEPTH, POOL_W, D), DEEPNORM_BETA * POOL_W ** -0.5),
        'w_attn_br': nrm(ks[10], (DEPTH, ATTN_W, D), DEEPNORM_BETA * ATTN_W ** -0.5),
        'attn_sink': nrm(ks[11], (DEPTH, N_HEADS), 0.5),
        'w_o': nrm(ks[12], (DEPTH, D, D), DEEPNORM_BETA * D ** -0.5),
        'ln1_g': 1.0 + nrm(ks[13], (DEPTH, D), 0.02),
        'ln1_b': nrm(ks[14], (DEPTH, D), 0.02),
        'w_router': nrm(ks[15], (D, N_EXPERTS), D ** -0.5),
        'router_bias': nrm(ks[16], (N_EXPERTS,), 0.01),
        'w_exp_gate': nrm(ks[17], (DEPTH, N_EXPERTS, D, D_EXPERT), D ** -0.5),
        'w_exp_up': nrm(ks[18], (DEPTH, N_EXPERTS, D, D_EXPERT), D ** -0.5),
        'w_exp_down': nrm(ks[19], (DEPTH, N_EXPERTS, D_EXPERT, D), DEEPNORM_BETA * D_EXPERT ** -0.5),
        'ln2_g': 1.0 + nrm(ks[20], (DEPTH, D), 0.02),
        'ln2_b': nrm(ks[21], (DEPTH, D), 0.02),
    }


def reference(x, c, ctx, c_ctx, w_ada, b_ada, w_in, w_pool_grp, pool_scale, w_pool_br, w_attn_br,
              attn_sink, w_o, ln1_g, ln1_b, w_router, router_bias, w_exp_gate, w_exp_up, w_exp_down,
              ln2_g, ln2_b):
    B, L, D = x.shape
    C = ctx.shape[1]
    ROWS = L // GRID_W
    rope = axial_rope_tables(ROWS)
    xc = ctx
    for l in range(DEPTH):
        ctx_out = l < DEPTH - 1
        sh1, sc1, g1, sh2, sc2, g2 = ada_terms(c[:, None, :], w_ada[l], b_ada[l])
        csh1, csc1, cg1, csh2, csc2, cg2 = ada_terms(c_ctx, w_ada[l], b_ada[l])
        h = modulate(x, sh1, sc1)
        hc = modulate(xc, csh1, csc1)
        y, yc = token_mixer(h, hc, rope, w_in[l], w_pool_grp[l], pool_scale[l], w_pool_br[l],
                            w_attn_br[l], attn_sink[l], w_o[l], ctx_out)
        x = layer_norm(DEEPNORM_ALPHA * x + g1 * y, ln1_g[l], ln1_b[l])
        h = modulate(x, sh2, sc2)
        if ctx_out:
            xc = layer_norm(DEEPNORM_ALPHA * xc + cg1 * yc, ln1_g[l], ln1_b[l])
            hc = modulate(xc, csh2, csc2)
            tokens = jnp.concatenate([h.reshape(B * L, D), hc.reshape(B * C, D)], axis=0)
        else:
            tokens = h.reshape(B * L, D)
        experts, weights = route(tokens, w_router, router_bias)
        f = expert_ffn(tokens, experts, weights, w_exp_gate[l], w_exp_up[l], w_exp_down[l])
        x = layer_norm(DEEPNORM_ALPHA * x + g2 * f[:B * L].reshape(B, L, D), ln2_g[l], ln2_b[l])
        if ctx_out:
            xc = layer_norm(DEEPNORM_ALPHA * xc + cg2 * f[B * L:].reshape(B, C, D), ln2_g[l], ln2_b[l])
    return x
```

```python
import functools

import jax
import jax.numpy as jnp
from jax import lax
from jax.experimental import pallas as pl
from jax.experimental.pallas import tpu as pltpu

F32 = jnp.float32
BF16 = jnp.bfloat16

GRID_W = 64
POOL_WINDOWS = (2, 4, 8, 16)
POOL_GROUP_W = 128
POOL_W = 512
HEAD_DIM = 64
N_HEADS = 8
N_KV_HEADS = 2
Q_REP = N_HEADS // N_KV_HEADS
ATTN_W = N_HEADS * HEAD_DIM
KV_W = N_KV_HEADS * HEAD_DIM
BLOCK = 128
ROPE_BASE = 10000.0
COL_POOL = 0
COL_Q = COL_POOL + POOL_W
COL_K = COL_Q + ATTN_W
COL_V = COL_K + KV_W
COL_GATE = COL_V + KV_W
N_EXPERTS = 32
N_EXPERT_GROUPS = 8
EXPERTS_PER_GROUP = N_EXPERTS // N_EXPERT_GROUPS
TOP_K = 2
LN_EPS = 1e-6

LANES = 128
POOL_HALO = 8
ROW_CHUNKS = 8
VMEM_LIMIT = 56 * 1024 * 1024

TM_IN = 512
TQ_MIX = 256
TN_ROUTE = 512
TN_DISP = 512
T_FFN = 256
TC_COMB = 256


def _cparams(sem):
    return pltpu.CompilerParams(dimension_semantics=sem, vmem_limit_bytes=VMEM_LIMIT)


def _layer_norm(x):
    mu = jnp.mean(x, axis=-1, keepdims=True)
    xc = x - mu
    var = jnp.mean(xc * xc, axis=-1, keepdims=True)
    return xc * lax.rsqrt(var + LN_EPS)


def _dot(a, b):
    return jnp.dot(a, b, preferred_element_type=F32)


def _store_chunked(ref, val):
    t = val.shape[0]
    for s in range(ROW_CHUNKS):
        ref[pl.ds(s, t, stride=ROW_CHUNKS), :] = val[:, s * LANES:(s + 1) * LANES]


def _load_chunked(ref, t):
    return jnp.concatenate([ref[pl.ds(s, t, stride=ROW_CHUNKS), :] for s in range(ROW_CHUNKS)], axis=1)


def _ada_kernel(cond_ref, w_ref, b_ref, o_ref):
    s = cond_ref[...]
    s = s * jax.nn.sigmoid(s)
    o_ref[0] = _dot(s.astype(BF16), w_ref[0].astype(BF16)) + b_ref[0]


def _ada_terms(cond, w_ada, b_ada):
    depth, d, n6 = w_ada.shape
    tn = n6 // 4
    return pl.pallas_call(
        _ada_kernel,
        grid=(depth, n6 // tn),
        in_specs=[
            pl.BlockSpec((8, d), lambda l, j: (0, 0)),
            pl.BlockSpec((1, d, tn), lambda l, j: (l, 0, j)),
            pl.BlockSpec((1, 1, tn), lambda l, j: (l, 0, j)),
        ],
        out_specs=pl.BlockSpec((1, 8, tn), lambda l, j: (l, 0, j)),
        out_shape=jax.ShapeDtypeStruct((depth, 8, n6), F32),
        compiler_params=_cparams(("arbitrary", "arbitrary")),
        name="ada_terms",
    )(cond, w_ada, b_ada.reshape(depth, 1, n6))


def _rope(t, cos, sin):
    lane = lax.broadcasted_iota(jnp.int32, (1, LANES), 1)
    first = (lane % 32) < 16
    outs = []
    for j in range(t.shape[1] // LANES):
        tj = t[:, j * LANES:(j + 1) * LANES]
        partner = jnp.where(first, pltpu.roll(tj, LANES - 16, 1), pltpu.roll(tj, 16, 1))
        outs.append(tj * cos + partner * sin)
    return outs[0] if len(outs) == 1 else jnp.concatenate(outs, axis=1)


def _inproj_kernel(x_ref, mod_ref, w_ref, cos_ref, sin_ref, *out_refs, rope, kv_only):
    d = x_ref.shape[1]
    mod = mod_ref[0]
    shift, scale = mod[:, 0:d], mod[:, d:2 * d]
    h = (_layer_norm(x_ref[...]) * (1.0 + scale) + shift).astype(BF16)

    def proj(lo, hi):
        return _dot(h, w_ref[:, lo:hi])

    if kv_only:
        k_ref, v_ref = out_refs
    else:
        u_ref, q_ref, k_ref, v_ref, g_ref = out_refs
        u_ref[...] = proj(COL_POOL, COL_Q)
        q = proj(COL_Q, COL_K)
        if rope:
            q = _rope(q, cos_ref[...], sin_ref[...])
        q_ref[...] = (q * (HEAD_DIM ** -0.5)).astype(BF16)
        g_ref[...] = jax.nn.sigmoid(proj(COL_GATE, w_ref.shape[1]))
    k = proj(COL_K, COL_V)
    if rope:
        k = _rope(k, cos_ref[...], sin_ref[...])
    k_ref[...] = k.astype(BF16)
    v_ref[...] = proj(COL_V, COL_GATE).astype(BF16)


def _inproj(x2d, mods, w_in, cos, sin, *, seq_len, mod_row, rope, kv_only=False):
    n, d = x2d.shape
    tm = min(TM_IN, seq_len)
    tps = seq_len // tm
    n_cols = w_in.shape[1]
    mod_map = (lambda i: (i // tps, 0, 0)) if mod_row is None else (lambda i: (mod_row, 0, 0))
    tab_map = (lambda i: (i % tps, 0)) if rope else (lambda i: (0, 0))
    row = lambda i: (i, 0)
    kv_shapes = [jax.ShapeDtypeStruct((n, KV_W), BF16)] * 2
    kv_specs = [pl.BlockSpec((tm, KV_W), row)] * 2
    if kv_only:
        out_shape, out_specs = kv_shapes, kv_specs
    else:
        out_shape = [jax.ShapeDtypeStruct((n, POOL_W), F32), jax.ShapeDtypeStruct((n, ATTN_W), BF16),
                     *kv_shapes, jax.ShapeDtypeStruct((n, n_cols - COL_GATE), F32)]
        out_specs = [pl.BlockSpec((tm, POOL_W), row), pl.BlockSpec((tm, ATTN_W), row),
                     *kv_specs, pl.BlockSpec((tm, n_cols - COL_GATE), row)]
    return pl.pallas_call(
        functools.partial(_inproj_kernel, rope=rope, kv_only=kv_only),
        grid=(n // tm,),
        in_specs=[
            pl.BlockSpec((tm, d), row),
            pl.BlockSpec((1, 1, mods.shape[2]), mod_map),
            pl.BlockSpec((d, n_cols), lambda i: (0, 0)),
            pl.BlockSpec((tm, LANES), tab_map),
            pl.BlockSpec((tm, LANES), tab_map),
        ],
        out_specs=out_specs,
        out_shape=out_shape,
        compiler_params=_cparams(("parallel",)),
        name="inproj_kv" if kv_only else "inproj",
    )(x2d, mods, w_in, cos, sin)


def _mixer_kernel(u_prev_ref, u_ref, u_next_ref, q_ref, k_prev_ref, k_ref, k_next_ref,
                  v_prev_ref, v_ref, v_next_ref, kc_ref, vc_ref, g_ref, x_ref, mod_ref,
                  wgrp_ref, pscale_ref, wpool_ref, wattn_ref, wo_ref, sink_ref, lng_ref, lnb_ref,
                  x1_ref, h2_ref, uext_ref, attn_ref, *, seq_len, local, alpha):
    tq, d = x_ref.shape
    nb = tq // BLOCK
    tps = seq_len // tq
    t_in_seq = pl.program_id(0) % tps
    is_first = t_in_seq == 0
    is_last = t_in_seq == tps - 1

    h8 = POOL_HALO
    uext_ref[0:h8, :] = jnp.where(is_first, 0.0, u_prev_ref[...])
    uext_ref[h8:h8 + tq, :] = u_ref[...]
    uext_ref[h8 + tq:, :] = jnp.where(is_last, 0.0, u_next_ref[...])
    pos = t_in_seq * tq + lax.broadcasted_iota(jnp.int32, (tq, 1), 0)
    pooled = []
    for gi, w in enumerate(POOL_WINDOWS):
        cols = slice(gi * POOL_GROUP_W, (gi + 1) * POOL_GROUP_W)
        acc = uext_ref[h8 - w // 2:h8 - w // 2 + tq, cols]
        for off in range(-w // 2 + 1, w // 2):
            acc = acc + uext_ref[h8 + off:h8 + off + tq, cols]
        lo = jnp.maximum(pos - w // 2, 0)
        hi = jnp.minimum(pos - w // 2 + w - 1, seq_len - 1)
        mean = acc / (hi - lo + 1).astype(F32)
        pg = (mean - u_ref[:, cols]).astype(BF16)
        pooled.append(_dot(pg, wgrp_ref[gi]))
    pool_lat = jnp.concatenate(pooled, axis=1) * pscale_ref[...]
    pool_proj = _dot(pool_lat.astype(BF16), wpool_ref[...])

    lane = lax.broadcasted_iota(jnp.int32, (1, LANES), 1)
    lo_half = lane < HEAD_DIM
    kc = kc_ref[0]
    vc = vc_ref[0]
    kc_g = [jnp.where(lo_half, kc, 0), jnp.where(lo_half, 0, kc)]
    if local:
        k_ext = jnp.concatenate([k_prev_ref[...], k_ref[...], k_next_ref[...]], axis=0)
        v_ext = jnp.concatenate([v_prev_ref[...], v_ref[...], v_next_ref[...]], axis=0)
        k_g = [jnp.where(lo_half, k_ext, 0), jnp.where(lo_half, 0, k_ext)]
        qq = lax.broadcasted_iota(jnp.int32, (Q_REP * BLOCK, BLOCK), 0) % BLOCK
        kk = lax.broadcasted_iota(jnp.int32, (Q_REP * BLOCK, BLOCK), 1)
        neg = jnp.float32(-jnp.inf)
        mask_prev = jnp.where(kk >= qq, 0.0, neg)
        mask_next = jnp.where(kk <= qq, 0.0, neg)
    for b in range(nb):
        rows = slice(b * BLOCK, (b + 1) * BLOCK)
        q_st = jnp.concatenate([q_ref[rows, c * LANES:(c + 1) * LANES] for c in range(Q_REP)], axis=0)
        if local:
            keys = slice(b * BLOCK, (b + 3) * BLOCK)
            v_all = jnp.concatenate([v_ext[keys], vc], axis=0)
            m_prev = jnp.where(is_first, neg, mask_prev) if b == 0 else mask_prev
            m_next = jnp.where(is_last, neg, mask_next) if b == nb - 1 else mask_next
        else:
            v_all = vc
        outs = []
        for g in range(N_KV_HEADS):
            k_all = jnp.concatenate([k_g[g][keys], kc_g[g]], axis=0) if local else kc_g[g]
            s = lax.dot_general(q_st, k_all, (((1,), (1,)), ((), ())), preferred_element_type=F32)
            if local:
                s = jnp.concatenate([s[:, 0:BLOCK] + m_prev, s[:, BLOCK:2 * BLOCK],
                                     s[:, 2 * BLOCK:3 * BLOCK] + m_next, s[:, 3 * BLOCK:]], axis=1)
            sk = sink_ref[g]
            m = jnp.maximum(jnp.max(s, axis=-1, keepdims=True), sk)
            e = jnp.exp(s - m)
            den = jnp.sum(e, axis=-1, keepdims=True) + jnp.exp(sk - m)
            outs.append(_dot(e.astype(BF16), v_all) / den)
        o = jnp.where(lo_half, outs[0], outs[1])
        for c in range(Q_REP):
            attn_ref[rows, c * LANES:(c + 1) * LANES] = o[c * BLOCK:(c + 1) * BLOCK].astype(BF16)
    attn_proj = _dot(attn_ref[...], wattn_ref[...])

    gates = g_ref[...]
    merged = gates[:, 0:d] * pool_proj + gates[:, d:2 * d] * attn_proj
    y = _dot(merged.astype(BF16), wo_ref[...])
    mod = mod_ref[0]
    g1 = mod[:, 2 * d:3 * d]
    sh2, sc2 = mod[:, 3 * d:4 * d], mod[:, 4 * d:5 * d]
    x1 = _layer_norm(alpha * x_ref[...] + g1 * y) * lng_ref[...] + lnb_ref[...]
    x1_ref[...] = x1
    _store_chunked(h2_ref, _layer_norm(x1) * (1.0 + sc2) + sh2)


def _mixer(u, q, k, v, kc, vc, gates, x2d, mods, wts, *, seq_len, mod_row, local, alpha):
    n, d = x2d.shape
    tq = min(TQ_MIX, seq_len)
    tps = seq_len // tq
    hb = tq // POOL_HALO
    kb = tq // BLOCK
    n_hb, n_kb = n // POOL_HALO, n // BLOCK
    c_len = kc.shape[1]
    row = lambda i: (i, 0)
    const2 = lambda i: (0, 0)
    const3 = lambda i: (0, 0, 0)
    mod_map = (lambda i: (i // tps, 0, 0)) if mod_row is None else (lambda i: (mod_row, 0, 0))
    ctx_map = lambda i: (i // tps, 0, 0)
    u_prev = pl.BlockSpec((POOL_HALO, POOL_W), lambda i: (jnp.maximum(i * hb - 1, 0), 0))
    u_next = pl.BlockSpec((POOL_HALO, POOL_W), lambda i: (jnp.minimum((i + 1) * hb, n_hb - 1), 0))
    kv_prev = pl.BlockSpec((BLOCK, KV_W), lambda i: (jnp.maximum(i * kb - 1, 0), 0))
    kv_cur = pl.BlockSpec((tq, KV_W), row)
    kv_next = pl.BlockSpec((BLOCK, KV_W), lambda i: (jnp.minimum((i + 1) * kb, n_kb - 1), 0))
    wgrp, pscale, wpool, wattn, wo, sink_col, lng, lnb = wts
    return pl.pallas_call(
        functools.partial(_mixer_kernel, seq_len=seq_len, local=local, alpha=alpha),
        grid=(n // tq,),
        in_specs=[
            u_prev, pl.BlockSpec((tq, POOL_W), row), u_next,
            pl.BlockSpec((tq, ATTN_W), row),
            kv_prev, kv_cur, kv_next, kv_prev, kv_cur, kv_next,
            pl.BlockSpec((1, c_len, KV_W), ctx_map), pl.BlockSpec((1, c_len, KV_W), ctx_map),
            pl.BlockSpec((tq, 2 * d), row),
            pl.BlockSpec((tq, d), row),
            pl.BlockSpec((1, 1, mods.shape[2]), mod_map),
            pl.BlockSpec(wgrp.shape, const3), pl.BlockSpec(pscale.shape, const2),
            pl.BlockSpec(wpool.shape, const2), pl.BlockSpec(wattn.shape, const2),
            pl.BlockSpec(wo.shape, const2), pl.BlockSpec(sink_col.shape, const3),
            pl.BlockSpec(lng.shape, const2), pl.BlockSpec(lnb.shape, const2),
        ],
        out_specs=[pl.BlockSpec((tq, d), row), pl.BlockSpec((tq * ROW_CHUNKS, LANES), row)],
        out_shape=[jax.ShapeDtypeStruct((n, d), F32), jax.ShapeDtypeStruct((n * ROW_CHUNKS, LANES), F32)],
        scratch_shapes=[pltpu.VMEM((tq + 2 * POOL_HALO, POOL_W), F32), pltpu.VMEM((tq, ATTN_W), BF16)],
        compiler_params=_cparams(("parallel",)),
        name="mixer" if local else "mixer_ctx",
    )(u, u, u, q, k, k, k, v, v, v, kc, vc, gates, x2d, mods,
      wgrp, pscale, wpool, wattn, wo, sink_col, lng, lnb)


def _route_kernel(h_ref, wr_ref, bias_ref, eid_ref, wts_ref, rank_ref, cnt_ref, base_ref):
    tn = h_ref.shape[0] // ROW_CHUNKS
    ng, epg = N_EXPERT_GROUPS, EXPERTS_PER_GROUP

    @pl.when(pl.program_id(0) == 0)
    def _():
        base_ref[...] = jnp.zeros_like(base_ref)

    logits = lax.dot_general(wr_ref[...], _load_chunked(h_ref, tn).astype(BF16), (((1,), (1,)), ((), ())),
                             preferred_element_type=F32)
    scores = jax.nn.sigmoid(logits)
    biased = scores + bias_ref[...]
    bj = [biased[j * ng:(j + 1) * ng] for j in range(epg)]
    sj = [scores[j * ng:(j + 1) * ng] for j in range(epg)]
    hi01, lo01 = jnp.maximum(bj[0], bj[1]), jnp.minimum(bj[0], bj[1])
    hi23, lo23 = jnp.maximum(bj[2], bj[3]), jnp.minimum(bj[2], bj[3])
    gscore = jnp.maximum(hi01, hi23) + jnp.maximum(jnp.minimum(hi01, hi23), jnp.maximum(lo01, lo23))
    giota = lax.broadcasted_iota(jnp.int32, (ng, tn), 0)
    gmax = jnp.max(gscore, axis=0, keepdims=True)
    g_first = jnp.min(jnp.where(gscore == gmax, giota.astype(F32), float(ng)), axis=0, keepdims=True)
    g_sel = g_first.astype(jnp.int32)
    in_g = giota == g_sel
    vb = [jnp.sum(jnp.where(in_g, b, 0.0), axis=0, keepdims=True) for b in bj]
    vs = [jnp.sum(jnp.where(in_g, s, 0.0), axis=0, keepdims=True) for s in sj]

    def first_best(vals):
        best = functools.reduce(jnp.maximum, vals)
        idx = jnp.full(best.shape, epg - 1, jnp.int32)
        for j in range(epg - 2, -1, -1):
            idx = jnp.where(vals[j] == best, j, idx)
        return idx

    def pick(vals, idx):
        out = vals[epg - 1]
        for j in range(epg - 2, -1, -1):
            out = jnp.where(idx == j, vals[j], out)
        return out

    l1 = first_best(vb)
    l2 = first_best([jnp.where(l1 == j, -jnp.inf, vb[j]) for j in range(epg)])
    w1, w2 = pick(vs, l1), pick(vs, l2)
    wsum = w1 + w2
    eid_ref[0:1, :] = g_sel * epg + l1
    eid_ref[1:2, :] = g_sel * epg + l2
    wts_ref[0:1, :] = w1 / wsum
    wts_ref[1:2, :] = w2 / wsum

    r1, r2 = l1 * ng + g_sel, l2 * ng + g_sel
    riota = lax.broadcasted_iota(jnp.int32, (N_EXPERTS, tn), 0)
    hit1, hit2 = riota == r1, riota == r2
    onehot = jnp.where(hit1 | hit2, 1.0, 0.0)
    before = lax.broadcasted_iota(jnp.int32, (tn, tn), 0) < lax.broadcasted_iota(jnp.int32, (tn, tn), 1)
    prefix = _dot(onehot.astype(BF16), jnp.where(before, 1.0, 0.0).astype(BF16)) + base_ref[:, 0:1]
    rank_ref[0:1, :] = jnp.sum(jnp.where(hit1, prefix, 0.0), axis=0, keepdims=True).astype(jnp.int32)
    rank_ref[1:2, :] = jnp.sum(jnp.where(hit2, prefix, 0.0), axis=0, keepdims=True).astype(jnp.int32)
    base_ref[...] = base_ref[...] + jnp.sum(onehot, axis=1, keepdims=True)
    cnt_ref[...] = base_ref[...]


def _route(h2c, wr_t, bias_col):
    n = h2c.shape[0] // ROW_CHUNKS
    d = wr_t.shape[1]
    tn = TN_ROUTE
    col = lambda i: (0, i)
    return pl.pallas_call(
        _route_kernel,
        grid=(n // tn,),
        in_specs=[
            pl.BlockSpec((tn * ROW_CHUNKS, LANES), lambda i: (i, 0)),
            pl.BlockSpec((N_EXPERTS, d), lambda i: (0, 0)),
            pl.BlockSpec((N_EXPERTS, 1), lambda i: (0, 0)),
        ],
        out_specs=[pl.BlockSpec((TOP_K, tn), col), pl.BlockSpec((TOP_K, tn), col),
                   pl.BlockSpec((TOP_K, tn), col), pl.BlockSpec((N_EXPERTS, LANES), lambda i: (0, 0))],
        out_shape=[jax.ShapeDtypeStruct((TOP_K, n), jnp.int32), jax.ShapeDtypeStruct((TOP_K, n), F32),
                   jax.ShapeDtypeStruct((TOP_K, n), jnp.int32), jax.ShapeDtypeStruct((N_EXPERTS, LANES), F32)],
        scratch_shapes=[pltpu.VMEM((N_EXPERTS, LANES), F32)],
        compiler_params=_cparams(("arbitrary",)),
        name="route",
    )(h2c, wr_t, bias_col)


def _dispatch_kernel(dest_ref, zlo_ref, h_hbm, xs_hbm, zero_ref, zsem, sem, *, t_ffn):
    i = pl.program_id(0)
    tn = dest_ref.shape[2]

    @pl.when(i == 0)
    def _():
        zero_ref[...] = jnp.zeros_like(zero_ref)

        def zcopy(e):
            return pltpu.make_async_copy(zero_ref, xs_hbm.at[pl.ds(jnp.maximum(zlo_ref[e], 0), t_ffn)], zsem)

        def start(e, carry):
            @pl.when(zlo_ref[e] >= 0)
            def _():
                zcopy(e).start()
            return carry

        def wait(e, carry):
            @pl.when(zlo_ref[e] >= 0)
            def _():
                zcopy(e).wait()
            return carry

        lax.fori_loop(0, N_EXPERTS, start, 0)
        lax.fori_loop(0, N_EXPERTS, wait, 0)

    def rcopy(n, slot):
        return pltpu.make_async_copy(h_hbm.at[i * tn + n], xs_hbm.at[dest_ref[0, slot, n]], sem)

    def start(n, carry):
        for slot in range(TOP_K):
            rcopy(n, slot).start()
        return carry

    def wait(n, carry):
        for slot in range(TOP_K):
            rcopy(n, slot).wait()
        return carry

    lax.fori_loop(0, tn, start, 0)
    lax.fori_loop(0, tn, wait, 0)


def _dispatch(h3, dest, zlo, n_rows, t_ffn):
    n = h3.shape[0]
    tn = TN_DISP
    dest3 = dest.reshape(TOP_K, n // tn, tn).transpose(1, 0, 2)
    return pl.pallas_call(
        functools.partial(_dispatch_kernel, t_ffn=t_ffn),
        grid=(n // tn,),
        in_specs=[
            pl.BlockSpec((1, TOP_K, tn), lambda i: (i, 0, 0), memory_space=pltpu.SMEM),
            pl.BlockSpec(memory_space=pltpu.SMEM),
            pl.BlockSpec(memory_space=pl.ANY),
        ],
        out_specs=pl.BlockSpec(memory_space=pl.ANY),
        out_shape=jax.ShapeDtypeStruct((n_rows,) + h3.shape[1:], h3.dtype),
        scratch_shapes=[pltpu.VMEM((t_ffn,) + h3.shape[1:], h3.dtype), pltpu.SemaphoreType.DMA,
                        pltpu.SemaphoreType.DMA],
        compiler_params=_cparams(("arbitrary",)),
        name="dispatch",
    )(dest3, zlo, h3)


def _ffn_kernel(texp_ref, nused_ref, x_ref, wg_ref, wu_ref, wd_ref, y_ref, wgb_ref, wub_ref, wdb_ref):
    j = pl.program_id(0)
    active = j < nused_ref[0]

    @pl.when(active & ((j == 0) | (texp_ref[j] != texp_ref[jnp.maximum(j - 1, 0)])))
    def _():
        wgb_ref[...] = wg_ref[0].astype(BF16)
        wub_ref[...] = wu_ref[0].astype(BF16)
        wdb_ref[...] = wd_ref[0].astype(BF16)

    @pl.when(active)
    def _():
        x = _load_chunked(x_ref, x_ref.shape[0] // ROW_CHUNKS).astype(BF16)
        gate = _dot(x, wgb_ref[...])
        up = _dot(x, wub_ref[...])
        act = (gate * jax.nn.sigmoid(gate) * up).astype(BF16)
        _store_chunked(y_ref, _dot(act, wdb_ref[...]))


def _expert_ffn(xs, tile_expert, n_used, w_gate, w_up, w_down, t_ffn):
    _, d, de = w_gate.shape
    n_rows = xs.shape[0] // ROW_CHUNKS
    n_tiles = n_rows // t_ffn
    rowmap = lambda j, te, nu: (jnp.minimum(j, nu[0] - 1), 0)
    wmap = lambda j, te, nu: (te[j], 0, 0)
    return pl.pallas_call(
        _ffn_kernel,
        grid_spec=pltpu.PrefetchScalarGridSpec(
            num_scalar_prefetch=2,
            grid=(n_tiles,),
            in_specs=[
                pl.BlockSpec((t_ffn * ROW_CHUNKS, LANES), rowmap),
                pl.BlockSpec((1, d, de), wmap),
                pl.BlockSpec((1, d, de), wmap),
                pl.BlockSpec((1, de, d), wmap),
            ],
            out_specs=pl.BlockSpec((t_ffn * ROW_CHUNKS, LANES), rowmap),
            scratch_shapes=[pltpu.VMEM((d, de), BF16), pltpu.VMEM((d, de), BF16), pltpu.VMEM((de, d), BF16)],
        ),
        out_shape=jax.ShapeDtypeStruct((n_rows * ROW_CHUNKS, LANES), F32),
        compiler_params=_cparams(("arbitrary",)),
        name="expert_ffn",
    )(tile_expert, n_used, xs, w_gate, w_up, w_down)


def _combine_kernel(dest_ref, dest_next_ref, y_hbm, wts_ref, x1_ref, mod_ref, lng_ref, lnb_ref, x2_ref,
                    buf_ref, sem, *, alpha):
    i = pl.program_id(0)
    n_steps = pl.num_programs(0)
    tc, d = x1_ref.shape

    def rcopy(idx_ref, step, n, slot):
        rows = pl.ds(pl.multiple_of(n * ROW_CHUNKS, ROW_CHUNKS), ROW_CHUNKS)
        return pltpu.make_async_copy(y_hbm.at[idx_ref[0, slot, n]],
                                     buf_ref.at[(step % 2) * TOP_K + slot, rows], sem.at[step % 2])

    def issue(idx_ref, step):
        def body(n, carry):
            for slot in range(TOP_K):
                rcopy(idx_ref, step, n, slot).start()
            return carry
        lax.fori_loop(0, tc, body, 0)

    @pl.when(i == 0)
    def _():
        issue(dest_ref, i)

    @pl.when(i + 1 < n_steps)
    def _():
        issue(dest_next_ref, i + 1)

    def wait(n, carry):
        for slot in range(TOP_K):
            rcopy(dest_ref, i, n, slot).wait()
        return carry

    lax.fori_loop(0, tc, wait, 0)
    w = wts_ref[...]
    cur = (i % 2) * TOP_K
    f = w[:, 0:1] * _load_chunked(buf_ref.at[cur], tc) + w[:, 1:2] * _load_chunked(buf_ref.at[cur + 1], tc)
    g2 = mod_ref[0][:, 5 * d:6 * d]
    x2_ref[...] = _layer_norm(alpha * x1_ref[...] + g2 * f) * lng_ref[...] + lnb_ref[...]


def _combine(y, dest, wts_t, x1, mods, lng, lnb, *, tok0, seq_len, mod_row, alpha):
    n, d = x1.shape
    tc = min(TC_COMB, seq_len)
    tps = seq_len // tc
    steps = n // tc
    dest3 = dest[:, tok0:tok0 + n].reshape(TOP_K, steps, tc).transpose(1, 0, 2)
    blk0 = tok0 // tc
    mod_map = (lambda i: (i // tps, 0, 0)) if mod_row is None else (lambda i: (mod_row, 0, 0))
    return pl.pallas_call(
        functools.partial(_combine_kernel, alpha=alpha),
        grid=(steps,),
        in_specs=[
            pl.BlockSpec((1, TOP_K, tc), lambda i: (i, 0, 0), memory_space=pltpu.SMEM),
            pl.BlockSpec((1, TOP_K, tc), lambda i: (jnp.minimum(i + 1, steps - 1), 0, 0),
                         memory_space=pltpu.SMEM),
            pl.BlockSpec(memory_space=pl.ANY),
            pl.BlockSpec((tc, TOP_K), lambda i: (blk0 + i, 0)),
            pl.BlockSpec((tc, d), lambda i: (i, 0)),
            pl.BlockSpec((1, 1, mods.shape[2]), mod_map),
            pl.BlockSpec(lng.shape, lambda i: (0, 0)),
            pl.BlockSpec(lnb.shape, lambda i: (0, 0)),
        ],
        out_specs=pl.BlockSpec((tc, d), lambda i: (i, 0)),
        out_shape=jax.ShapeDtypeStruct((n, d), F32),
        scratch_shapes=[pltpu.VMEM((2 * TOP_K, tc * ROW_CHUNKS, LANES), F32), pltpu.SemaphoreType.DMA((2,))],
        compiler_params=_cparams(("arbitrary",)),
        name="combine",
    )(dest3, dest3, y, wts_t, x1, mods, lng, lnb)


def _rope_tables(seq_len):
    t = jnp.arange(seq_len)
    row = (t // GRID_W).astype(F32)
    col = (t % GRID_W).astype(F32)
    half = HEAD_DIM // 2
    inv_freq = ROPE_BASE ** (-jnp.arange(0, half, 2, dtype=F32) / half)
    ang_r, ang_c = row[:, None] * inv_freq, col[:, None] * inv_freq
    cr, sr, cc, sc = jnp.cos(ang_r), jnp.sin(ang_r), jnp.cos(ang_c), jnp.sin(ang_c)
    cos = jnp.concatenate([cr, cr, cc, cc], axis=1)
    sin = jnp.concatenate([-sr, sr, -sc, sc], axis=1)
    return jnp.tile(cos, (1, LANES // HEAD_DIM)), jnp.tile(sin, (1, LANES // HEAD_DIM))


def _pair_heads(a, axis):
    shp = a.shape
    a = a.reshape(*shp[:axis], N_KV_HEADS, Q_REP, HEAD_DIM, *shp[axis + 1:])
    a = jnp.swapaxes(a, axis, axis + 1)
    return a.reshape(shp)


def _moe_plan(eid, rank, counts, n_tok, t_ffn):
    counts = counts.reshape(EXPERTS_PER_GROUP, N_EXPERT_GROUPS).T.reshape(N_EXPERTS).astype(jnp.int32)
    padded = (counts + t_ffn - 1) // t_ffn * t_ffn
    pends = jnp.cumsum(padded)
    pstarts = pends - padded
    dest = pstarts[eid] + rank
    n_rows = -(-(n_tok * TOP_K) // t_ffn) * t_ffn + N_EXPERTS * t_ffn
    n_tiles = n_rows // t_ffn
    tile_expert = jnp.minimum(jnp.searchsorted(pends, jnp.arange(n_tiles) * t_ffn, side='right'),
                              N_EXPERTS - 1).astype(jnp.int32)
    n_used = (pends[-1] // t_ffn).astype(jnp.int32).reshape(1)
    zlo = jnp.where(counts > 0, pends - t_ffn, -1).astype(jnp.int32)
    return dest.astype(jnp.int32), tile_expert, n_used, zlo, n_rows


def kernel(x, c, ctx, c_ctx, w_ada, b_ada, w_in, w_pool_grp, pool_scale, w_pool_br, w_attn_br, attn_sink,
           w_o, ln1_g, ln1_b, w_router, router_bias, w_exp_gate, w_exp_up, w_exp_down, ln2_g, ln2_b):
    bsz, seq, d = x.shape
    assert d == ROW_CHUNKS * LANES
    c_len = ctx.shape[1]
    depth = w_in.shape[0]
    n_lat, n_ctx = bsz * seq, bsz * c_len
    alpha = (2 * depth) ** 0.25

    cond = jnp.zeros((8, d), F32).at[:bsz].set(c).at[bsz].set(c_ctx)
    ada = _ada_terms(cond, w_ada, b_ada)
    cos, sin = _rope_tables(seq)
    wr_t = w_router.reshape(d, N_EXPERT_GROUPS, EXPERTS_PER_GROUP).transpose(2, 1, 0).reshape(N_EXPERTS, d)
    wr_t = wr_t.astype(BF16)
    bias_col = router_bias.reshape(N_EXPERT_GROUPS, EXPERTS_PER_GROUP).T.reshape(N_EXPERTS, 1).astype(F32)

    xl = x.reshape(n_lat, d)
    xc = ctx.reshape(n_ctx, d)
    for l in range(depth):
        ctx_out = l < depth - 1
        mods = ada[l].reshape(8, 1, 6 * d)
        w_l = w_in[l]
        w_inp = jnp.concatenate([w_l[:, :COL_Q], _pair_heads(w_l[:, COL_Q:COL_K], 1), w_l[:, COL_K:]],
                                axis=1).astype(BF16)
        sink_col = jnp.broadcast_to(attn_sink[l].reshape(N_KV_HEADS, Q_REP, 1, 1),
                                    (N_KV_HEADS, Q_REP, BLOCK, 1)).reshape(N_KV_HEADS, Q_REP * BLOCK, 1)
        mix_w = (w_pool_grp[l].astype(BF16), pool_scale[l].reshape(1, POOL_W), w_pool_br[l].astype(BF16),
                 _pair_heads(w_attn_br[l], 0).astype(BF16), w_o[l].astype(BF16), sink_col.astype(F32),
                 ln1_g[l].reshape(1, d), ln1_b[l].reshape(1, d))
        lng2, lnb2 = ln2_g[l].reshape(1, d), ln2_b[l].reshape(1, d)

        if ctx_out:
            uc, qc, kc, vc, gc = _inproj(xc, mods, w_inp, cos, sin, seq_len=c_len, mod_row=bsz, rope=False)
        else:
            kc, vc = _inproj(xc, mods, w_inp, cos, sin, seq_len=c_len, mod_row=bsz, rope=False, kv_only=True)
        kc3, vc3 = kc.reshape(bsz, c_len, KV_W), vc.reshape(bsz, c_len, KV_W)
        u, q, k, v, g = _inproj(xl, mods, w_inp, cos, sin, seq_len=seq, mod_row=None, rope=True)
        x1, h2 = _mixer(u, q, k, v, kc3, vc3, g, xl, mods, mix_w, seq_len=seq, mod_row=None, local=True,
                        alpha=alpha)
        if ctx_out:
            xc1, hc2 = _mixer(uc, qc, kc, vc, kc3, vc3, gc, xc, mods, mix_w, seq_len=c_len, mod_row=bsz,
                              local=False, alpha=alpha)
            tokens = jnp.concatenate([h2, hc2], axis=0)
        else:
            tokens = h2

        n_tok = tokens.shape[0] // ROW_CHUNKS
        eid, wts, rank, counts = _route(tokens, wr_t, bias_col)
        dest, tile_expert, n_used, zlo, n_rows = _moe_plan(eid, rank, counts[:, 0], n_tok, T_FFN)
        xs = _dispatch(tokens.reshape(n_tok, ROW_CHUNKS, LANES), dest, zlo, n_rows, T_FFN)
        y = _expert_ffn(xs.reshape(n_rows * ROW_CHUNKS, LANES), tile_expert, n_used,
                        w_exp_gate[l], w_exp_up[l], w_exp_down[l], T_FFN)
        y3 = y.reshape(n_rows, ROW_CHUNKS, LANES)
        wts_t = wts.T
        xl = _combine(y3, dest, wts_t, x1, mods, lng2, lnb2, tok0=0, seq_len=seq, mod_row=None, alpha=alpha)
        if ctx_out:
            xc = _combine(y3, dest, wts_t, xc1, mods, lng2, lnb2, tok0=n_lat, seq_len=c_len, mod_row=bsz,
                          alpha=alpha)
    return xl.reshape(bsz, seq, d)
```

```python
import functools

import jax
import jax.numpy as jnp
from jax import lax
from jax.experimental import pallas as pl
from jax.experimental.pallas import tpu as pltpu

F32 = jnp.float32
BF16 = jnp.bfloat16

GRID_W = 64
POOL_WINDOWS = (2, 4, 8, 16)
POOL_GROUP_W = 128
POOL_W = 512
HEAD_DIM = 64
N_HEADS = 8
N_KV_HEADS = 2
Q_REP = N_HEADS // N_KV_HEADS
ATTN_W = N_HEADS * HEAD_DIM
KV_W = N_KV_HEADS * HEAD_DIM
BLOCK = 128
ROPE_BASE = 10000.0
COL_POOL = 0
COL_Q = COL_POOL + POOL_W
COL_K = COL_Q + ATTN_W
COL_V = COL_K + KV_W
COL_GATE = COL_V + KV_W
N_EXPERTS = 32
N_EXPERT_GROUPS = 8
EXPERTS_PER_GROUP = N_EXPERTS // N_EXPERT_GROUPS
TOP_K = 2
LN_EPS = 1e-6

LANES = 128
POOL_HALO = 8
ROW_CHUNKS = 8
VMEM_LIMIT = 56 * 1024 * 1024

TM_IN = 512
TQ_MIX = 256
TN_TOK = 512
T_FFN = 256
DMA_UNROLL = 8


def _cparams(sem):
    return pltpu.CompilerParams(dimension_semantics=sem, vmem_limit_bytes=VMEM_LIMIT)


def _layer_norm(x):
    mu = jnp.mean(x, axis=-1, keepdims=True)
    xc = x - mu
    var = jnp.mean(xc * xc, axis=-1, keepdims=True)
    return xc * lax.rsqrt(var + LN_EPS)


def _dot(a, b):
    return jnp.dot(a, b, preferred_element_type=F32)


def _store_chunked(ref, val):
    t = val.shape[0]
    for s in range(ROW_CHUNKS):
        ref[pl.ds(s, t, stride=ROW_CHUNKS), :] = val[:, s * LANES:(s + 1) * LANES]


def _load_chunked(ref, t):
    return jnp.concatenate([ref[pl.ds(s, t, stride=ROW_CHUNKS), :] for s in range(ROW_CHUNKS)], axis=1)


def _ada_kernel(cond_ref, w_ref, b_ref, o_ref):
    s = cond_ref[...]
    s = s * jax.nn.sigmoid(s)
    o_ref[0] = _dot(s.astype(BF16), w_ref[0].astype(BF16)) + b_ref[0]


def _ada_terms(cond, w_ada, b_ada):
    depth, d, n6 = w_ada.shape
    tn = n6 // 4
    return pl.pallas_call(
        _ada_kernel,
        grid=(depth, n6 // tn),
        in_specs=[
            pl.BlockSpec((8, d), lambda l, j: (0, 0)),
            pl.BlockSpec((1, d, tn), lambda l, j: (l, 0, j)),
            pl.BlockSpec((1, 1, tn), lambda l, j: (l, 0, j)),
        ],
        out_specs=pl.BlockSpec((1, 8, tn), lambda l, j: (l, 0, j)),
        out_shape=jax.ShapeDtypeStruct((depth, 8, n6), F32),
        compiler_params=_cparams(("arbitrary", "arbitrary")),
        name="ada_terms",
    )(cond, w_ada, b_ada.reshape(depth, 1, n6))


def _rope(t, cos, sin):
    lane = lax.broadcasted_iota(jnp.int32, (1, LANES), 1)
    first = (lane % 32) < 16
    outs = []
    for j in range(t.shape[1] // LANES):
        tj = t[:, j * LANES:(j + 1) * LANES]
        partner = jnp.where(first, pltpu.roll(tj, LANES - 16, 1), pltpu.roll(tj, 16, 1))
        outs.append(tj * cos + partner * sin)
    return outs[0] if len(outs) == 1 else jnp.concatenate(outs, axis=1)


def _inproj_kernel(x_ref, mod_ref, w_ref, cos_ref, sin_ref, *out_refs, rope, kv_only):
    d = x_ref.shape[1]
    mod = mod_ref[0]
    shift, scale = mod[:, 0:d], mod[:, d:2 * d]
    h = (_layer_norm(x_ref[...]) * (1.0 + scale) + shift).astype(BF16)

    def proj(lo, hi):
        return _dot(h, w_ref[:, lo:hi])

    if kv_only:
        k_ref, v_ref = out_refs
    else:
        u_ref, q_ref, k_ref, v_ref, g_ref = out_refs
        u_ref[...] = proj(COL_POOL, COL_Q)
        q = proj(COL_Q, COL_K)
        if rope:
            q = _rope(q, cos_ref[...], sin_ref[...])
        q_ref[...] = (q * (HEAD_DIM ** -0.5)).astype(BF16)
        g_ref[...] = jax.nn.sigmoid(proj(COL_GATE, w_ref.shape[1]))
    k = proj(COL_K, COL_V)
    if rope:
        k = _rope(k, cos_ref[...], sin_ref[...])
    k_ref[...] = k.astype(BF16)
    v_ref[...] = proj(COL_V, COL_GATE).astype(BF16)


def _inproj(x2d, mods, w_in, cos, sin, *, seq_len, mod_row, rope, kv_only=False):
    n, d = x2d.shape
    tm = min(TM_IN, seq_len)
    tps = seq_len // tm
    n_cols = w_in.shape[1]
    mod_map = (lambda i: (i // tps, 0, 0)) if mod_row is None else (lambda i: (mod_row, 0, 0))
    tab_map = (lambda i: (i % tps, 0)) if rope else (lambda i: (0, 0))
    row = lambda i: (i, 0)
    kv_shapes = [jax.ShapeDtypeStruct((n, KV_W), BF16)] * 2
    kv_specs = [pl.BlockSpec((tm, KV_W), row)] * 2
    if kv_only:
        out_shape, out_specs = kv_shapes, kv_specs
    else:
        out_shape = [jax.ShapeDtypeStruct((n, POOL_W), F32), jax.ShapeDtypeStruct((n, ATTN_W), BF16),
                     *kv_shapes, jax.ShapeDtypeStruct((n, n_cols - COL_GATE), F32)]
        out_specs = [pl.BlockSpec((tm, POOL_W), row), pl.BlockSpec((tm, ATTN_W), row),
                     *kv_specs, pl.BlockSpec((tm, n_cols - COL_GATE), row)]
    return pl.pallas_call(
        functools.partial(_inproj_kernel, rope=rope, kv_only=kv_only),
        grid=(n // tm,),
        in_specs=[
            pl.BlockSpec((tm, d), row),
            pl.BlockSpec((1, 1, mods.shape[2]), mod_map),
            pl.BlockSpec((d, n_cols), lambda i: (0, 0)),
            pl.BlockSpec((tm, LANES), tab_map),
            pl.BlockSpec((tm, LANES), tab_map),
        ],
        out_specs=out_specs,
        out_shape=out_shape,
        compiler_params=_cparams(("parallel",)),
        name="inproj_kv" if kv_only else "inproj",
    )(x2d, mods, w_in, cos, sin)


def _mixer_kernel(u_prev_ref, u_ref, u_next_ref, q_ref, k_prev_ref, k_ref, k_next_ref,
                  v_prev_ref, v_ref, v_next_ref, kc_ref, vc_ref, g_ref, x_ref, mod_ref,
                  wgrp_ref, pscale_ref, wpool_ref, wattn_ref, wo_ref, sink_ref, lng_ref, lnb_ref,
                  *rest, seq_len, local, alpha):
    x1_ref, h2_ref, uext_ref, attn_ref = rest[-4:]
    tq, d = x_ref.shape
    nb = tq // BLOCK
    tps = seq_len // tq
    t_in_seq = pl.program_id(0) % tps
    is_first = t_in_seq == 0
    is_last = t_in_seq == tps - 1

    h8 = POOL_HALO
    uext_ref[0:h8, :] = jnp.where(is_first, 0.0, u_prev_ref[...])
    uext_ref[h8:h8 + tq, :] = u_ref[...]
    uext_ref[h8 + tq:, :] = jnp.where(is_last, 0.0, u_next_ref[...])
    pos = t_in_seq * tq + lax.broadcasted_iota(jnp.int32, (tq, 1), 0)
    pooled = []
    for gi, w in enumerate(POOL_WINDOWS):
        cols = slice(gi * POOL_GROUP_W, (gi + 1) * POOL_GROUP_W)
        acc = uext_ref[h8 - w // 2:h8 - w // 2 + tq, cols]
        for off in range(-w // 2 + 1, w // 2):
            acc = acc + uext_ref[h8 + off:h8 + off + tq, cols]
        lo = jnp.maximum(pos - w // 2, 0)
        hi = jnp.minimum(pos - w // 2 + w - 1, seq_len - 1)
        mean = acc / (hi - lo + 1).astype(F32)
        pg = (mean - u_ref[:, cols]).astype(BF16)
        pooled.append(_dot(pg, wgrp_ref[gi]))
    pool_lat = jnp.concatenate(pooled, axis=1) * pscale_ref[...]
    pool_proj = _dot(pool_lat.astype(BF16), wpool_ref[...])

    lane = lax.broadcasted_iota(jnp.int32, (1, LANES), 1)
    lo_half = lane < HEAD_DIM
    kc = kc_ref[0]
    vc = vc_ref[0]
    kc_g = [jnp.where(lo_half, kc, 0), jnp.where(lo_half, 0, kc)]
    if local:
        k_ext = jnp.concatenate([k_prev_ref[...], k_ref[...], k_next_ref[...]], axis=0)
        v_ext = jnp.concatenate([v_prev_ref[...], v_ref[...], v_next_ref[...]], axis=0)
        k_g = [jnp.where(lo_half, k_ext, 0), jnp.where(lo_half, 0, k_ext)]
        qq = lax.broadcasted_iota(jnp.int32, (Q_REP * BLOCK, BLOCK), 0) % BLOCK
        kk = lax.broadcasted_iota(jnp.int32, (Q_REP * BLOCK, BLOCK), 1)
        neg = jnp.float32(-jnp.inf)
        mask_prev = jnp.where(kk >= qq, 0.0, neg)
        mask_next = jnp.where(kk <= qq, 0.0, neg)
    for b in range(nb):
        rows = slice(b * BLOCK, (b + 1) * BLOCK)
        q_st = jnp.concatenate([q_ref[rows, c * LANES:(c + 1) * LANES] for c in range(Q_REP)], axis=0)
        if local:
            keys = slice(b * BLOCK, (b + 3) * BLOCK)
            v_all = jnp.concatenate([v_ext[keys], vc], axis=0)
            m_prev = jnp.where(is_first, neg, mask_prev) if b == 0 else mask_prev
            m_next = jnp.where(is_last, neg, mask_next) if b == nb - 1 else mask_next
        else:
            v_all = vc
        outs = []
        for g in range(N_KV_HEADS):
            k_all = jnp.concatenate([k_g[g][keys], kc_g[g]], axis=0) if local else kc_g[g]
            s = lax.dot_general(q_st, k_all, (((1,), (1,)), ((), ())), preferred_element_type=F32)
            if local:
                s = jnp.concatenate([s[:, 0:BLOCK] + m_prev, s[:, BLOCK:2 * BLOCK],
                                     s[:, 2 * BLOCK:3 * BLOCK] + m_next, s[:, 3 * BLOCK:]], axis=1)
            sk = sink_ref[g]
            m = jnp.maximum(jnp.max(s, axis=-1, keepdims=True), sk)
            e = jnp.exp(s - m)
            den = jnp.sum(e, axis=-1, keepdims=True) + jnp.exp(sk - m)
            outs.append(_dot(e.astype(BF16), v_all) / den)
        o = jnp.where(lo_half, outs[0], outs[1])
        for c in range(Q_REP):
            attn_ref[rows, c * LANES:(c + 1) * LANES] = o[c * BLOCK:(c + 1) * BLOCK].astype(BF16)
    attn_proj = _dot(attn_ref[...], wattn_ref[...])

    gates = g_ref[...]
    merged = gates[:, 0:d] * pool_proj + gates[:, d:2 * d] * attn_proj
    y = _dot(merged.astype(BF16), wo_ref[...])
    mod = mod_ref[0]
    g1 = mod[:, 2 * d:3 * d]
    sh2, sc2 = mod[:, 3 * d:4 * d], mod[:, 4 * d:5 * d]
    x1 = _layer_norm(alpha * x_ref[...] + g1 * y) * lng_ref[...] + lnb_ref[...]
    x1_ref[...] = x1
    _store_chunked(h2_ref, _layer_norm(x1) * (1.0 + sc2) + sh2)


def _mixer(u, q, k, v, kc, vc, gates, x2d, mods, wts, *, seq_len, mod_row, local, alpha,
           h2_tokens, h2_tok0, h2_buf=None):
    n, d = x2d.shape
    tq = min(TQ_MIX, seq_len)
    h2_blk0 = h2_tok0 // tq
    tps = seq_len // tq
    hb = tq // POOL_HALO
    kb = tq // BLOCK
    n_hb, n_kb = n // POOL_HALO, n // BLOCK
    c_len = kc.shape[1]
    row = lambda i: (i, 0)
    const2 = lambda i: (0, 0)
    const3 = lambda i: (0, 0, 0)
    mod_map = (lambda i: (i // tps, 0, 0)) if mod_row is None else (lambda i: (mod_row, 0, 0))
    ctx_map = lambda i: (i // tps, 0, 0)
    u_prev = pl.BlockSpec((POOL_HALO, POOL_W), lambda i: (jnp.maximum(i * hb - 1, 0), 0))
    u_next = pl.BlockSpec((POOL_HALO, POOL_W), lambda i: (jnp.minimum((i + 1) * hb, n_hb - 1), 0))
    kv_prev = pl.BlockSpec((BLOCK, KV_W), lambda i: (jnp.maximum(i * kb - 1, 0), 0))
    kv_cur = pl.BlockSpec((tq, KV_W), row)
    kv_next = pl.BlockSpec((BLOCK, KV_W), lambda i: (jnp.minimum((i + 1) * kb, n_kb - 1), 0))
    wgrp, pscale, wpool, wattn, wo, sink_col, lng, lnb = wts
    operands = [u, u, u, q, k, k, k, v, v, v, kc, vc, gates, x2d, mods,
                wgrp, pscale, wpool, wattn, wo, sink_col, lng, lnb]
    alias_specs, aliases = [], {}
    if h2_buf is not None:
        alias_specs = [pl.BlockSpec(memory_space=pl.ANY)]
        aliases = {len(operands): 1}
        operands.append(h2_buf)
    return pl.pallas_call(
        functools.partial(_mixer_kernel, seq_len=seq_len, local=local, alpha=alpha),
        grid=(n // tq,),
        input_output_aliases=aliases,
        in_specs=[
            u_prev, pl.BlockSpec((tq, POOL_W), row), u_next,
            pl.BlockSpec((tq, ATTN_W), row),
            kv_prev, kv_cur, kv_next, kv_prev, kv_cur, kv_next,
            pl.BlockSpec((1, c_len, KV_W), ctx_map), pl.BlockSpec((1, c_len, KV_W), ctx_map),
            pl.BlockSpec((tq, 2 * d), row),
            pl.BlockSpec((tq, d), row),
            pl.BlockSpec((1, 1, mods.shape[2]), mod_map),
            pl.BlockSpec(wgrp.shape, const3), pl.BlockSpec(pscale.shape, const2),
            pl.BlockSpec(wpool.shape, const2), pl.BlockSpec(wattn.shape, const2),
            pl.BlockSpec(wo.shape, const2), pl.BlockSpec(sink_col.shape, const3),
            pl.BlockSpec(lng.shape, const2), pl.BlockSpec(lnb.shape, const2),
            *alias_specs,
        ],
        out_specs=[pl.BlockSpec((tq, d), row),
                   pl.BlockSpec((tq * ROW_CHUNKS, LANES), lambda i: (h2_blk0 + i, 0))],
        out_shape=[jax.ShapeDtypeStruct((n, d), F32),
                   jax.ShapeDtypeStruct((h2_tokens * ROW_CHUNKS, LANES), F32)],
        scratch_shapes=[pltpu.VMEM((tq + 2 * POOL_HALO, POOL_W), F32), pltpu.VMEM((tq, ATTN_W), BF16)],
        compiler_params=_cparams(("parallel",)),
        name="mixer" if local else "mixer_ctx",
    )(*operands)


def _route_kernel(h_ref, wr_ref, bias_ref, eid_ref, wts_ref, rank_ref, cnt_ref, base_ref):
    tn = h_ref.shape[0] // ROW_CHUNKS
    ng, epg = N_EXPERT_GROUPS, EXPERTS_PER_GROUP

    @pl.when(pl.program_id(0) == 0)
    def _():
        base_ref[...] = jnp.zeros_like(base_ref)

    logits = lax.dot_general(wr_ref[...], _load_chunked(h_ref, tn).astype(BF16), (((1,), (1,)), ((), ())),
                             preferred_element_type=F32)
    scores = jax.nn.sigmoid(logits)
    biased = scores + bias_ref[...]
    bj = [biased[j * ng:(j + 1) * ng] for j in range(epg)]
    sj = [scores[j * ng:(j + 1) * ng] for j in range(epg)]
    hi01, lo01 = jnp.maximum(bj[0], bj[1]), jnp.minimum(bj[0], bj[1])
    hi23, lo23 = jnp.maximum(bj[2], bj[3]), jnp.minimum(bj[2], bj[3])
    gscore = jnp.maximum(hi01, hi23) + jnp.maximum(jnp.minimum(hi01, hi23), jnp.maximum(lo01, lo23))
    giota = lax.broadcasted_iota(jnp.int32, (ng, tn), 0)
    gmax = jnp.max(gscore, axis=0, keepdims=True)
    g_first = jnp.min(jnp.where(gscore == gmax, giota.astype(F32), float(ng)), axis=0, keepdims=True)
    g_sel = g_first.astype(jnp.int32)
    in_g = giota == g_sel
    vb = [jnp.sum(jnp.where(in_g, b, 0.0), axis=0, keepdims=True) for b in bj]
    vs = [jnp.sum(jnp.where(in_g, s, 0.0), axis=0, keepdims=True) for s in sj]

    def first_best(vals):
        best = functools.reduce(jnp.maximum, vals)
        idx = jnp.full(best.shape, epg - 1, jnp.int32)
        for j in range(epg - 2, -1, -1):
            idx = jnp.where(vals[j] == best, j, idx)
        return idx

    def pick(vals, idx):
        out = vals[epg - 1]
        for j in range(epg - 2, -1, -1):
            out = jnp.where(idx == j, vals[j], out)
        return out

    l1 = first_best(vb)
    l2 = first_best([jnp.where(l1 == j, -jnp.inf, vb[j]) for j in range(epg)])
    w1, w2 = pick(vs, l1), pick(vs, l2)
    wsum = w1 + w2
    eid_ref[0, 0:1, :] = g_sel * epg + l1
    eid_ref[0, 1:2, :] = g_sel * epg + l2
    wts_ref[0:1, :] = w1 / wsum
    wts_ref[1:2, :] = w2 / wsum

    r1, r2 = l1 * ng + g_sel, l2 * ng + g_sel
    riota = lax.broadcasted_iota(jnp.int32, (N_EXPERTS, tn), 0)
    hit1, hit2 = riota == r1, riota == r2
    onehot = jnp.where(hit1 | hit2, 1.0, 0.0)
    before = lax.broadcasted_iota(jnp.int32, (tn, tn), 0) < lax.broadcasted_iota(jnp.int32, (tn, tn), 1)
    prefix = _dot(onehot.astype(BF16), jnp.where(before, 1.0, 0.0).astype(BF16)) + base_ref[:, 0:1]
    rank_ref[0, 0:1, :] = jnp.sum(jnp.where(hit1, prefix, 0.0), axis=0, keepdims=True).astype(jnp.int32)
    rank_ref[0, 1:2, :] = jnp.sum(jnp.where(hit2, prefix, 0.0), axis=0, keepdims=True).astype(jnp.int32)
    base_ref[...] = base_ref[...] + jnp.sum(onehot, axis=1, keepdims=True)
    cnt_ref[...] = base_ref[...]


def _route(h2c, wr_t, bias_col):
    n = h2c.shape[0] // ROW_CHUNKS
    d = wr_t.shape[1]
    tn = TN_TOK
    steps = n // tn
    col = lambda i: (0, i)
    blk = lambda i: (i, 0, 0)
    return pl.pallas_call(
        _route_kernel,
        grid=(steps,),
        in_specs=[
            pl.BlockSpec((tn * ROW_CHUNKS, LANES), lambda i: (i, 0)),
            pl.BlockSpec((N_EXPERTS, d), lambda i: (0, 0)),
            pl.BlockSpec((N_EXPERTS, 1), lambda i: (0, 0)),
        ],
        out_specs=[pl.BlockSpec((1, TOP_K, tn), blk), pl.BlockSpec((TOP_K, tn), col),
                   pl.BlockSpec((1, TOP_K, tn), blk), pl.BlockSpec((N_EXPERTS, LANES), lambda i: (0, 0))],
        out_shape=[jax.ShapeDtypeStruct((steps, TOP_K, tn), jnp.int32), jax.ShapeDtypeStruct((TOP_K, n), F32),
                   jax.ShapeDtypeStruct((steps, TOP_K, tn), jnp.int32),
                   jax.ShapeDtypeStruct((N_EXPERTS, LANES), F32)],
        scratch_shapes=[pltpu.VMEM((N_EXPERTS, LANES), F32)],
        compiler_params=_cparams(("arbitrary",)),
        name="route",
    )(h2c, wr_t, bias_col)


def _dispatch_kernel(eid_ref, rank_ref, pstart_ref, zlo_ref, tok_ref, xs_hbm, zero_ref, zsem, sem, *, t_ffn):
    i = pl.program_id(0)
    tn = eid_ref.shape[2]

    @pl.when(i == 0)
    def _():
        zero_ref[...] = jnp.zeros_like(zero_ref)

        def zcopy(e):
            return pltpu.make_async_copy(zero_ref, xs_hbm.at[pl.ds(jnp.maximum(zlo_ref[e], 0), t_ffn)], zsem)

        def start(e, carry):
            @pl.when(zlo_ref[e] >= 0)
            def _():
                zcopy(e).start()
            return carry

        def wait(e, carry):
            @pl.when(zlo_ref[e] >= 0)
            def _():
                zcopy(e).wait()
            return carry

        lax.fori_loop(0, N_EXPERTS, start, 0)
        lax.fori_loop(0, N_EXPERTS, wait, 0)

    def start(n, carry):
        src = tok_ref.at[pl.ds(pl.multiple_of(n * ROW_CHUNKS, ROW_CHUNKS), ROW_CHUNKS)]
        for slot in range(TOP_K):
            dest = pstart_ref[eid_ref[0, slot, n]] + rank_ref[0, slot, n]
            pltpu.make_async_copy(src, xs_hbm.at[dest], sem).start()
        return carry

    lax.fori_loop(0, tn, start, 0, unroll=DMA_UNROLL)
    for slot in range(TOP_K):
        pltpu.make_async_copy(xs_hbm.at[pl.ds(0, tn)], xs_hbm.at[pl.ds(0, tn)], sem).wait()


def _dispatch(tokens, eid3, rank3, pstart, zlo, n_rows, t_ffn):
    steps, _, tn = eid3.shape
    blk = lambda i: (i, 0, 0)
    smem = pltpu.SMEM
    return pl.pallas_call(
        functools.partial(_dispatch_kernel, t_ffn=t_ffn),
        grid=(steps,),
        in_specs=[
            pl.BlockSpec((1, TOP_K, tn), blk, memory_space=smem),
            pl.BlockSpec((1, TOP_K, tn), blk, memory_space=smem),
            pl.BlockSpec(memory_space=smem),
            pl.BlockSpec(memory_space=smem),
            pl.BlockSpec((tn * ROW_CHUNKS, LANES), lambda i: (i, 0)),
        ],
        out_specs=pl.BlockSpec(memory_space=pl.ANY),
        out_shape=jax.ShapeDtypeStruct((n_rows, ROW_CHUNKS, LANES), tokens.dtype),
        scratch_shapes=[pltpu.VMEM((t_ffn, ROW_CHUNKS, LANES), tokens.dtype), pltpu.SemaphoreType.DMA,
                        pltpu.SemaphoreType.DMA],
        compiler_params=_cparams(("arbitrary",)),
        name="dispatch",
    )(eid3, rank3, pstart, zlo, tokens)


def _ffn_kernel(texp_ref, nused_ref, x_ref, wg_ref, wu_ref, wd_ref, y_ref, wgb_ref, wub_ref, wdb_ref):
    j = pl.program_id(0)
    active = j < nused_ref[0]

    @pl.when(active & ((j == 0) | (texp_ref[j] != texp_ref[jnp.maximum(j - 1, 0)])))
    def _():
        wgb_ref[...] = wg_ref[0].astype(BF16)
        wub_ref[...] = wu_ref[0].astype(BF16)
        wdb_ref[...] = wd_ref[0].astype(BF16)

    @pl.when(active)
    def _():
        x = _load_chunked(x_ref, x_ref.shape[0] // ROW_CHUNKS).astype(BF16)
        gate = _dot(x, wgb_ref[...])
        up = _dot(x, wub_ref[...])
        act = (gate * jax.nn.sigmoid(gate) * up).astype(BF16)
        _store_chunked(y_ref, _dot(act, wdb_ref[...]))


def _expert_ffn(xs, tile_expert, n_used, w_gate, w_up, w_down, t_ffn):
    _, d, de = w_gate.shape
    n_rows = xs.shape[0] // ROW_CHUNKS
    n_tiles = n_rows // t_ffn
    rowmap = lambda j, te, nu: (jnp.minimum(j, nu[0] - 1), 0)
    wmap = lambda j, te, nu: (te[j], 0, 0)
    return pl.pallas_call(
        _ffn_kernel,
        grid_spec=pltpu.PrefetchScalarGridSpec(
            num_scalar_prefetch=2,
            grid=(n_tiles,),
            in_specs=[
                pl.BlockSpec((t_ffn * ROW_CHUNKS, LANES), rowmap),
                pl.BlockSpec((1, d, de), wmap),
                pl.BlockSpec((1, d, de), wmap),
                pl.BlockSpec((1, de, d), wmap),
            ],
            out_specs=pl.BlockSpec((t_ffn * ROW_CHUNKS, LANES), rowmap),
            scratch_shapes=[pltpu.VMEM((d, de), BF16), pltpu.VMEM((d, de), BF16), pltpu.VMEM((de, d), BF16)],
        ),
        out_shape=jax.ShapeDtypeStruct((n_rows * ROW_CHUNKS, LANES), F32),
        compiler_params=_cparams(("arbitrary",)),
        name="expert_ffn",
    )(tile_expert, n_used, xs, w_gate, w_up, w_down)


def _combine_kernel(eid_ref, rank_ref, eid_next_ref, rank_next_ref, pstart_ref, y_hbm, wts_ref, x1_ref,
                    mod_ref, lng_ref, lnb_ref, x2_ref, buf_ref, sem, *, alpha):
    i = pl.program_id(0)
    n_steps = pl.num_programs(0)
    tc, d = x1_ref.shape

    def issue(e_ref, r_ref, step):
        def body(n, carry):
            rows = pl.ds(pl.multiple_of(n * ROW_CHUNKS, ROW_CHUNKS), ROW_CHUNKS)
            for slot in range(TOP_K):
                src = pstart_ref[e_ref[0, slot, n]] + r_ref[0, slot, n]
                pltpu.make_async_copy(y_hbm.at[src], buf_ref.at[(step % 2) * TOP_K + slot, rows],
                                      sem.at[step % 2]).start()
            return carry
        lax.fori_loop(0, tc, body, 0, unroll=DMA_UNROLL)

    @pl.when(i == 0)
    def _():
        issue(eid_ref, rank_ref, i)

    @pl.when(i + 1 < n_steps)
    def _():
        issue(eid_next_ref, rank_next_ref, i + 1)

    for slot in range(TOP_K):
        pltpu.make_async_copy(y_hbm.at[pl.ds(0, tc)], y_hbm.at[pl.ds(0, tc)], sem.at[i % 2]).wait()
    w = wts_ref[...]
    cur = (i % 2) * TOP_K
    f = w[:, 0:1] * _load_chunked(buf_ref.at[cur], tc) + w[:, 1:2] * _load_chunked(buf_ref.at[cur + 1], tc)
    g2 = mod_ref[0][:, 5 * d:6 * d]
    x2_ref[...] = _layer_norm(alpha * x1_ref[...] + g2 * f) * lng_ref[...] + lnb_ref[...]


def _combine(y, eid3, rank3, pstart, wts_t, x1, mods, lng, lnb, *, tok0, seq_len, mod_row, alpha):
    n, d = x1.shape
    tc = eid3.shape[2]
    tps = max(seq_len // tc, 1)
    steps = n // tc
    blk0 = tok0 // tc
    mod_map = (lambda i: (i // tps, 0, 0)) if mod_row is None else (lambda i: (mod_row, 0, 0))
    cur = lambda i: (blk0 + i, 0, 0)
    nxt = lambda i: (blk0 + jnp.minimum(i + 1, steps - 1), 0, 0)
    smem = pltpu.SMEM
    return pl.pallas_call(
        functools.partial(_combine_kernel, alpha=alpha),
        grid=(steps,),
        in_specs=[
            pl.BlockSpec((1, TOP_K, tc), cur, memory_space=smem),
            pl.BlockSpec((1, TOP_K, tc), cur, memory_space=smem),
            pl.BlockSpec((1, TOP_K, tc), nxt, memory_space=smem),
            pl.BlockSpec((1, TOP_K, tc), nxt, memory_space=smem),
            pl.BlockSpec(memory_space=smem),
            pl.BlockSpec(memory_space=pl.ANY),
            pl.BlockSpec((tc, TOP_K), lambda i: (blk0 + i, 0)),
            pl.BlockSpec((tc, d), lambda i: (i, 0)),
            pl.BlockSpec((1, 1, mods.shape[2]), mod_map),
            pl.BlockSpec(lng.shape, lambda i: (0, 0)),
            pl.BlockSpec(lnb.shape, lambda i: (0, 0)),
        ],
        out_specs=pl.BlockSpec((tc, d), lambda i: (i, 0)),
        out_shape=jax.ShapeDtypeStruct((n, d), F32),
        scratch_shapes=[pltpu.VMEM((2 * TOP_K, tc * ROW_CHUNKS, LANES), F32), pltpu.SemaphoreType.DMA((2,))],
        compiler_params=_cparams(("arbitrary",)),
        name="combine",
    )(eid3, rank3, eid3, rank3, pstart, y, wts_t, x1, mods, lng, lnb)


def _rope_tables(seq_len):
    t = jnp.arange(seq_len)
    row = (t // GRID_W).astype(F32)
    col = (t % GRID_W).astype(F32)
    half = HEAD_DIM // 2
    inv_freq = ROPE_BASE ** (-jnp.arange(0, half, 2, dtype=F32) / half)
    ang_r, ang_c = row[:, None] * inv_freq, col[:, None] * inv_freq
    cr, sr, cc, sc = jnp.cos(ang_r), jnp.sin(ang_r), jnp.cos(ang_c), jnp.sin(ang_c)
    cos = jnp.concatenate([cr, cr, cc, cc], axis=1)
    sin = jnp.concatenate([-sr, sr, -sc, sc], axis=1)
    return jnp.tile(cos, (1, LANES // HEAD_DIM)), jnp.tile(sin, (1, LANES // HEAD_DIM))


def _pair_heads(a, axis):
    shp = a.shape
    a = a.reshape(*shp[:axis], N_KV_HEADS, Q_REP, HEAD_DIM, *shp[axis + 1:])
    a = jnp.swapaxes(a, axis, axis + 1)
    return a.reshape(shp)


def _moe_plan(counts, n_tok, t_ffn):
    counts = counts.reshape(EXPERTS_PER_GROUP, N_EXPERT_GROUPS).T.reshape(N_EXPERTS).astype(jnp.int32)
    padded = (counts + t_ffn - 1) // t_ffn * t_ffn
    pends = jnp.cumsum(padded)
    pstarts = (pends - padded).astype(jnp.int32)
    n_rows = -(-(n_tok * TOP_K) // t_ffn) * t_ffn + N_EXPERTS * t_ffn
    n_tiles = n_rows // t_ffn
    tile_row0 = jnp.arange(n_tiles, dtype=jnp.int32) * t_ffn
    tile_expert = jnp.minimum(jnp.sum(pends[None, :] <= tile_row0[:, None], axis=1), N_EXPERTS - 1)
    n_used = (pends[-1] // t_ffn).astype(jnp.int32).reshape(1)
    zlo = jnp.where(counts > 0, pends - t_ffn, -1).astype(jnp.int32)
    return pstarts, tile_expert.astype(jnp.int32), n_used, zlo, n_rows


def kernel(x, c, ctx, c_ctx, w_ada, b_ada, w_in, w_pool_grp, pool_scale, w_pool_br, w_attn_br, attn_sink,
           w_o, ln1_g, ln1_b, w_router, router_bias, w_exp_gate, w_exp_up, w_exp_down, ln2_g, ln2_b):
    bsz, seq, d = x.shape
    assert d == ROW_CHUNKS * LANES
    c_len = ctx.shape[1]
    depth = w_in.shape[0]
    n_lat, n_ctx = bsz * seq, bsz * c_len
    alpha = (2 * depth) ** 0.25

    cond = jnp.zeros((8, d), F32).at[:bsz].set(c).at[bsz].set(c_ctx)
    ada = _ada_terms(cond, w_ada, b_ada)
    cos, sin = _rope_tables(seq)
    wr_t = w_router.reshape(d, N_EXPERT_GROUPS, EXPERTS_PER_GROUP).transpose(2, 1, 0).reshape(N_EXPERTS, d)
    wr_t = wr_t.astype(BF16)
    bias_col = router_bias.reshape(N_EXPERT_GROUPS, EXPERTS_PER_GROUP).T.reshape(N_EXPERTS, 1).astype(F32)

    xl = x.reshape(n_lat, d)
    xc = ctx.reshape(n_ctx, d)
    for l in range(depth):
        ctx_out = l < depth - 1
        mods = ada[l].reshape(8, 1, 6 * d)
        w_l = w_in[l]
        w_inp = jnp.concatenate([w_l[:, :COL_Q], _pair_heads(w_l[:, COL_Q:COL_K], 1), w_l[:, COL_K:]],
                                axis=1).astype(BF16)
        sink_col = jnp.broadcast_to(attn_sink[l].reshape(N_KV_HEADS, Q_REP, 1, 1),
                                    (N_KV_HEADS, Q_REP, BLOCK, 1)).reshape(N_KV_HEADS, Q_REP * BLOCK, 1)
        mix_w = (w_pool_grp[l].astype(BF16), pool_scale[l].reshape(1, POOL_W), w_pool_br[l].astype(BF16),
                 _pair_heads(w_attn_br[l], 0).astype(BF16), w_o[l].astype(BF16), sink_col.astype(F32),
                 ln1_g[l].reshape(1, d), ln1_b[l].reshape(1, d))
        lng2, lnb2 = ln2_g[l].reshape(1, d), ln2_b[l].reshape(1, d)

        if ctx_out:
            uc, qc, kc, vc, gc = _inproj(xc, mods, w_inp, cos, sin, seq_len=c_len, mod_row=bsz, rope=False)
        else:
            kc, vc = _inproj(xc, mods, w_inp, cos, sin, seq_len=c_len, mod_row=bsz, rope=False, kv_only=True)
        kc3, vc3 = kc.reshape(bsz, c_len, KV_W), vc.reshape(bsz, c_len, KV_W)
        u, q, k, v, g = _inproj(xl, mods, w_inp, cos, sin, seq_len=seq, mod_row=None, rope=True)
        n_tok = n_lat + n_ctx if ctx_out else n_lat
        tokens = None
        if ctx_out:
            xc1, tokens = _mixer(uc, qc, kc, vc, kc3, vc3, gc, xc, mods, mix_w, seq_len=c_len, mod_row=bsz,
                                 local=False, alpha=alpha, h2_tokens=n_tok, h2_tok0=n_lat)
        x1, tokens = _mixer(u, q, k, v, kc3, vc3, g, xl, mods, mix_w, seq_len=seq, mod_row=None, local=True,
                            alpha=alpha, h2_tokens=n_tok, h2_tok0=0, h2_buf=tokens)

        eid3, wts, rank3, counts = _route(tokens, wr_t, bias_col)
        pstart, tile_expert, n_used, zlo, n_rows = _moe_plan(counts[:, 0], n_tok, T_FFN)
        xs = _dispatch(tokens, eid3, rank3, pstart, zlo, n_rows, T_FFN)
        y = _expert_ffn(xs.reshape(n_rows * ROW_CHUNKS, LANES), tile_expert, n_used,
                        w_exp_gate[l], w_exp_up[l], w_exp_down[l], T_FFN)
        y3 = y.reshape(n_rows, ROW_CHUNKS, LANES)
        wts_t = wts.T
        xl = _combine(y3, eid3, rank3, pstart, wts_t, x1, mods, lng2, lnb2, tok0=0, seq_len=seq,
                      mod_row=None, alpha=alpha)
        if ctx_out:
            xc = _combine(y3, eid3, rank3, pstart, wts_t, xc1, mods, lng2, lnb2, tok0=n_lat, seq_len=c_len,
                          mod_row=bsz, alpha=alpha)
    return xl.reshape(bsz, seq, d)
```

```python
import functools

import jax
import jax.numpy as jnp
from jax import lax
from jax.experimental import pallas as pl
from jax.experimental.pallas import tpu as pltpu

F32 = jnp.float32
BF16 = jnp.bfloat16

GRID_W = 64
POOL_WINDOWS = (2, 4, 8, 16)
POOL_GROUP_W = 128
POOL_W = 512
HEAD_DIM = 64
N_HEADS = 8
N_KV_HEADS = 2
Q_REP = N_HEADS // N_KV_HEADS
ATTN_W = N_HEADS * HEAD_DIM
KV_W = N_KV_HEADS * HEAD_DIM
BLOCK = 128
ROPE_BASE = 10000.0
COL_POOL = 0
COL_Q = COL_POOL + POOL_W
COL_K = COL_Q + ATTN_W
COL_V = COL_K + KV_W
COL_GATE = COL_V + KV_W
N_EXPERTS = 32
N_EXPERT_GROUPS = 8
EXPERTS_PER_GROUP = N_EXPERTS // N_EXPERT_GROUPS
TOP_K = 2
LN_EPS = 1e-6

LANES = 128
POOL_HALO = 8
ROW_CHUNKS = 8
VMEM_LIMIT = 56 * 1024 * 1024

TM_IN = 512
TQ_MIX = 256
TN_TOK = 512
T_FFN = 256
DMA_UNROLL = 8


def _cparams(sem, flags=None):
    return pltpu.CompilerParams(dimension_semantics=sem, vmem_limit_bytes=VMEM_LIMIT, flags=flags)


def _layer_norm(x):
    mu = jnp.mean(x, axis=-1, keepdims=True)
    xc = x - mu
    var = jnp.mean(xc * xc, axis=-1, keepdims=True)
    return xc * lax.rsqrt(var + LN_EPS)


def _dot(a, b):
    return jnp.dot(a, b, preferred_element_type=F32)


def _store_chunked(ref, val):
    t = val.shape[0]
    for s in range(ROW_CHUNKS):
        ref[pl.ds(s, t, stride=ROW_CHUNKS), :] = val[:, s * LANES:(s + 1) * LANES]


def _load_chunked(ref, t):
    return jnp.concatenate([ref[pl.ds(s, t, stride=ROW_CHUNKS), :] for s in range(ROW_CHUNKS)], axis=1)


def _ada_kernel(cond_ref, w_ref, b_ref, o_ref):
    s = cond_ref[...]
    s = s * jax.nn.sigmoid(s)
    o_ref[0] = _dot(s.astype(BF16), w_ref[0].astype(BF16)) + b_ref[0]


def _ada_terms(cond, w_ada, b_ada):
    depth, d, n6 = w_ada.shape
    tn = n6 // 4
    return pl.pallas_call(
        _ada_kernel,
        grid=(depth, n6 // tn),
        in_specs=[
            pl.BlockSpec((8, d), lambda l, j: (0, 0)),
            pl.BlockSpec((1, d, tn), lambda l, j: (l, 0, j)),
            pl.BlockSpec((1, 1, tn), lambda l, j: (l, 0, j)),
        ],
        out_specs=pl.BlockSpec((1, 8, tn), lambda l, j: (l, 0, j)),
        out_shape=jax.ShapeDtypeStruct((depth, 8, n6), F32),
        compiler_params=_cparams(("arbitrary", "arbitrary")),
        name="ada_terms",
    )(cond, w_ada, b_ada.reshape(depth, 1, n6))


def _rope(t, cos, sin):
    lane = lax.broadcasted_iota(jnp.int32, (1, LANES), 1)
    first = (lane % 32) < 16
    outs = []
    for j in range(t.shape[1] // LANES):
        tj = t[:, j * LANES:(j + 1) * LANES]
        partner = jnp.where(first, pltpu.roll(tj, LANES - 16, 1), pltpu.roll(tj, 16, 1))
        outs.append(tj * cos + partner * sin)
    return outs[0] if len(outs) == 1 else jnp.concatenate(outs, axis=1)


def _inproj_kernel(x_ref, mod_ref, w_ref, cos_ref, sin_ref, *out_refs, rope, kv_only):
    d = x_ref.shape[1]
    mod = mod_ref[0]
    shift, scale = mod[:, 0:d], mod[:, d:2 * d]
    h = (_layer_norm(x_ref[...]) * (1.0 + scale) + shift).astype(BF16)

    def proj(lo, hi):
        return _dot(h, w_ref[:, lo:hi])

    if kv_only:
        k_ref, v_ref = out_refs
    else:
        u_ref, q_ref, k_ref, v_ref, g_ref = out_refs
        u_ref[...] = proj(COL_POOL, COL_Q)
        q = proj(COL_Q, COL_K)
        if rope:
            q = _rope(q, cos_ref[...], sin_ref[...])
        q_ref[...] = (q * (HEAD_DIM ** -0.5)).astype(BF16)
        g_ref[...] = jax.nn.sigmoid(proj(COL_GATE, w_ref.shape[1]))
    k = proj(COL_K, COL_V)
    if rope:
        k = _rope(k, cos_ref[...], sin_ref[...])
    k_ref[...] = k.astype(BF16)
    v_ref[...] = proj(COL_V, COL_GATE).astype(BF16)


def _inproj(x2d, mods, w_in, cos, sin, *, seq_len, mod_row, rope, kv_only=False):
    n, d = x2d.shape
    tm = min(TM_IN, seq_len)
    tps = seq_len // tm
    n_cols = w_in.shape[1]
    mod_map = (lambda i: (i // tps, 0, 0)) if mod_row is None else (lambda i: (mod_row, 0, 0))
    tab_map = (lambda i: (i % tps, 0)) if rope else (lambda i: (0, 0))
    row = lambda i: (i, 0)
    kv_shapes = [jax.ShapeDtypeStruct((n, KV_W), BF16)] * 2
    kv_specs = [pl.BlockSpec((tm, KV_W), row)] * 2
    if kv_only:
        out_shape, out_specs = kv_shapes, kv_specs
    else:
        out_shape = [jax.ShapeDtypeStruct((n, POOL_W), F32), jax.ShapeDtypeStruct((n, ATTN_W), BF16),
                     *kv_shapes, jax.ShapeDtypeStruct((n, n_cols - COL_GATE), F32)]
        out_specs = [pl.BlockSpec((tm, POOL_W), row), pl.BlockSpec((tm, ATTN_W), row),
                     *kv_specs, pl.BlockSpec((tm, n_cols - COL_GATE), row)]
    return pl.pallas_call(
        functools.partial(_inproj_kernel, rope=rope, kv_only=kv_only),
        grid=(n // tm,),
        in_specs=[
            pl.BlockSpec((tm, d), row),
            pl.BlockSpec((1, 1, mods.shape[2]), mod_map),
            pl.BlockSpec((d, n_cols), lambda i: (0, 0)),
            pl.BlockSpec((tm, LANES), tab_map),
            pl.BlockSpec((tm, LANES), tab_map),
        ],
        out_specs=out_specs,
        out_shape=out_shape,
        compiler_params=_cparams(("parallel",)),
        name="inproj_kv" if kv_only else "inproj",
    )(x2d, mods, w_in, cos, sin)


def _mixer_kernel(u_prev_ref, u_ref, u_next_ref, q_ref, k_prev_ref, k_ref, k_next_ref,
                  v_prev_ref, v_ref, v_next_ref, kc_ref, vc_ref, g_ref, x_ref, mod_ref,
                  wgrp_ref, pscale_ref, wpool_ref, wattn_ref, wo_ref, sink_ref, lng_ref, lnb_ref,
                  *rest, seq_len, local, alpha):
    x1_ref, h2_ref, uext_ref, attn_ref = rest[-4:]
    tq, d = x_ref.shape
    nb = tq // BLOCK
    tps = seq_len // tq
    t_in_seq = pl.program_id(0) % tps
    is_first = t_in_seq == 0
    is_last = t_in_seq == tps - 1

    h8 = POOL_HALO
    uext_ref[0:h8, :] = jnp.where(is_first, 0.0, u_prev_ref[...])
    uext_ref[h8:h8 + tq, :] = u_ref[...]
    uext_ref[h8 + tq:, :] = jnp.where(is_last, 0.0, u_next_ref[...])
    pos = t_in_seq * tq + lax.broadcasted_iota(jnp.int32, (tq, 1), 0)
    pooled = []
    for gi, w in enumerate(POOL_WINDOWS):
        cols = slice(gi * POOL_GROUP_W, (gi + 1) * POOL_GROUP_W)
        acc = uext_ref[h8 - w // 2:h8 - w // 2 + tq, cols]
        for off in range(-w // 2 + 1, w // 2):
            acc = acc + uext_ref[h8 + off:h8 + off + tq, cols]
        lo = jnp.maximum(pos - w // 2, 0)
        hi = jnp.minimum(pos - w // 2 + w - 1, seq_len - 1)
        mean = acc / (hi - lo + 1).astype(F32)
        pg = (mean - u_ref[:, cols]).astype(BF16)
        pooled.append(_dot(pg, wgrp_ref[gi]))
    pool_lat = jnp.concatenate(pooled, axis=1) * pscale_ref[...]
    pool_proj = _dot(pool_lat.astype(BF16), wpool_ref[...])

    lane = lax.broadcasted_iota(jnp.int32, (1, LANES), 1)
    lo_half = lane < HEAD_DIM
    kc = kc_ref[0]
    vc = vc_ref[0]
    kc_g = [jnp.where(lo_half, kc, 0), jnp.where(lo_half, 0, kc)]
    if local:
        k_ext = jnp.concatenate([k_prev_ref[...], k_ref[...], k_next_ref[...]], axis=0)
        v_ext = jnp.concatenate([v_prev_ref[...], v_ref[...], v_next_ref[...]], axis=0)
        k_g = [jnp.where(lo_half, k_ext, 0), jnp.where(lo_half, 0, k_ext)]
        qq = lax.broadcasted_iota(jnp.int32, (Q_REP * BLOCK, BLOCK), 0) % BLOCK
        kk = lax.broadcasted_iota(jnp.int32, (Q_REP * BLOCK, BLOCK), 1)
        neg = jnp.float32(-jnp.inf)
        mask_prev = jnp.where(kk >= qq, 0.0, neg)
        mask_next = jnp.where(kk <= qq, 0.0, neg)
    for b in range(nb):
        rows = slice(b * BLOCK, (b + 1) * BLOCK)
        q_st = jnp.concatenate([q_ref[rows, c * LANES:(c + 1) * LANES] for c in range(Q_REP)], axis=0)
        if local:
            keys = slice(b * BLOCK, (b + 3) * BLOCK)
            v_all = jnp.concatenate([v_ext[keys], vc], axis=0)
            m_prev = jnp.where(is_first, neg, mask_prev) if b == 0 else mask_prev
            m_next = jnp.where(is_last, neg, mask_next) if b == nb - 1 else mask_next
        else:
            v_all = vc
        outs = []
        for g in range(N_KV_HEADS):
            k_all = jnp.concatenate([k_g[g][keys], kc_g[g]], axis=0) if local else kc_g[g]
            s = lax.dot_general(q_st, k_all, (((1,), (1,)), ((), ())), preferred_element_type=F32)
            if local:
                s = jnp.concatenate([s[:, 0:BLOCK] + m_prev, s[:, BLOCK:2 * BLOCK],
                                     s[:, 2 * BLOCK:3 * BLOCK] + m_next, s[:, 3 * BLOCK:]], axis=1)
            sk = sink_ref[g]
            m = jnp.maximum(jnp.max(s, axis=-1, keepdims=True), sk)
            e = jnp.exp(s - m)
            den = jnp.sum(e, axis=-1, keepdims=True) + jnp.exp(sk - m)
            outs.append(_dot(e.astype(BF16), v_all) / den)
        o = jnp.where(lo_half, outs[0], outs[1])
        for c in range(Q_REP):
            attn_ref[rows, c * LANES:(c + 1) * LANES] = o[c * BLOCK:(c + 1) * BLOCK].astype(BF16)
    attn_proj = _dot(attn_ref[...], wattn_ref[...])

    gates = g_ref[...]
    merged = gates[:, 0:d] * pool_proj + gates[:, d:2 * d] * attn_proj
    y = _dot(merged.astype(BF16), wo_ref[...])
    mod = mod_ref[0]
    g1 = mod[:, 2 * d:3 * d]
    sh2, sc2 = mod[:, 3 * d:4 * d], mod[:, 4 * d:5 * d]
    x1 = _layer_norm(alpha * x_ref[...] + g1 * y) * lng_ref[...] + lnb_ref[...]
    x1_ref[...] = x1
    _store_chunked(h2_ref, _layer_norm(x1) * (1.0 + sc2) + sh2)


def _mixer(u, q, k, v, kc, vc, gates, x2d, mods, wts, *, seq_len, mod_row, local, alpha,
           h2_tokens, h2_tok0, h2_buf=None):
    n, d = x2d.shape
    tq = min(TQ_MIX, seq_len)
    h2_blk0 = h2_tok0 // tq
    tps = seq_len // tq
    hb = tq // POOL_HALO
    kb = tq // BLOCK
    n_hb, n_kb = n // POOL_HALO, n // BLOCK
    c_len = kc.shape[1]
    row = lambda i: (i, 0)
    const2 = lambda i: (0, 0)
    const3 = lambda i: (0, 0, 0)
    mod_map = (lambda i: (i // tps, 0, 0)) if mod_row is None else (lambda i: (mod_row, 0, 0))
    ctx_map = lambda i: (i // tps, 0, 0)
    u_prev = pl.BlockSpec((POOL_HALO, POOL_W), lambda i: (jnp.maximum(i * hb - 1, 0), 0))
    u_next = pl.BlockSpec((POOL_HALO, POOL_W), lambda i: (jnp.minimum((i + 1) * hb, n_hb - 1), 0))
    kv_prev = pl.BlockSpec((BLOCK, KV_W), lambda i: (jnp.maximum(i * kb - 1, 0), 0))
    kv_cur = pl.BlockSpec((tq, KV_W), row)
    kv_next = pl.BlockSpec((BLOCK, KV_W), lambda i: (jnp.minimum((i + 1) * kb, n_kb - 1), 0))
    wgrp, pscale, wpool, wattn, wo, sink_col, lng, lnb = wts
    operands = [u, u, u, q, k, k, k, v, v, v, kc, vc, gates, x2d, mods,
                wgrp, pscale, wpool, wattn, wo, sink_col, lng, lnb]
    alias_specs, aliases = [], {}
    if h2_buf is not None:
        alias_specs = [pl.BlockSpec(memory_space=pl.ANY)]
        aliases = {len(operands): 1}
        operands.append(h2_buf)
    return pl.pallas_call(
        functools.partial(_mixer_kernel, seq_len=seq_len, local=local, alpha=alpha),
        grid=(n // tq,),
        input_output_aliases=aliases,
        in_specs=[
            u_prev, pl.BlockSpec((tq, POOL_W), row), u_next,
            pl.BlockSpec((tq, ATTN_W), row),
            kv_prev, kv_cur, kv_next, kv_prev, kv_cur, kv_next,
            pl.BlockSpec((1, c_len, KV_W), ctx_map), pl.BlockSpec((1, c_len, KV_W), ctx_map),
            pl.BlockSpec((tq, 2 * d), row),
            pl.BlockSpec((tq, d), row),
            pl.BlockSpec((1, 1, mods.shape[2]), mod_map),
            pl.BlockSpec(wgrp.shape, const3), pl.BlockSpec(pscale.shape, const2),
            pl.BlockSpec(wpool.shape, const2), pl.BlockSpec(wattn.shape, const2),
            pl.BlockSpec(wo.shape, const2), pl.BlockSpec(sink_col.shape, const3),
            pl.BlockSpec(lng.shape, const2), pl.BlockSpec(lnb.shape, const2),
            *alias_specs,
        ],
        out_specs=[pl.BlockSpec((tq, d), row),
                   pl.BlockSpec((tq * ROW_CHUNKS, LANES), lambda i: (h2_blk0 + i, 0))],
        out_shape=[jax.ShapeDtypeStruct((n, d), F32),
                   jax.ShapeDtypeStruct((h2_tokens * ROW_CHUNKS, LANES), F32)],
        scratch_shapes=[pltpu.VMEM((tq + 2 * POOL_HALO, POOL_W), F32), pltpu.VMEM((tq, ATTN_W), BF16)],
        compiler_params=_cparams(("parallel",)),
        name="mixer" if local else "mixer_ctx",
    )(*operands)


def _route_kernel(h_ref, wr_ref, bias_ref, eid_ref, wts_ref, rank_ref, cnt_ref, base_ref):
    tn = h_ref.shape[0] // ROW_CHUNKS
    ng, epg = N_EXPERT_GROUPS, EXPERTS_PER_GROUP

    @pl.when(pl.program_id(0) == 0)
    def _():
        base_ref[...] = jnp.zeros_like(base_ref)

    logits = lax.dot_general(wr_ref[...], _load_chunked(h_ref, tn).astype(BF16), (((1,), (1,)), ((), ())),
                             preferred_element_type=F32)
    scores = jax.nn.sigmoid(logits)
    biased = scores + bias_ref[...]
    bj = [biased[j * ng:(j + 1) * ng] for j in range(epg)]
    sj = [scores[j * ng:(j + 1) * ng] for j in range(epg)]
    hi01, lo01 = jnp.maximum(bj[0], bj[1]), jnp.minimum(bj[0], bj[1])
    hi23, lo23 = jnp.maximum(bj[2], bj[3]), jnp.minimum(bj[2], bj[3])
    gscore = jnp.maximum(hi01, hi23) + jnp.maximum(jnp.minimum(hi01, hi23), jnp.maximum(lo01, lo23))
    giota = lax.broadcasted_iota(jnp.int32, (ng, tn), 0)
    gmax = jnp.max(gscore, axis=0, keepdims=True)
    g_first = jnp.min(jnp.where(gscore == gmax, giota.astype(F32), float(ng)), axis=0, keepdims=True)
    g_sel = g_first.astype(jnp.int32)
    in_g = giota == g_sel
    vb = [jnp.sum(jnp.where(in_g, b, 0.0), axis=0, keepdims=True) for b in bj]
    vs = [jnp.sum(jnp.where(in_g, s, 0.0), axis=0, keepdims=True) for s in sj]

    def first_best(vals):
        best = functools.reduce(jnp.maximum, vals)
        idx = jnp.full(best.shape, epg - 1, jnp.int32)
        for j in range(epg - 2, -1, -1):
            idx = jnp.where(vals[j] == best, j, idx)
        return idx

    def pick(vals, idx):
        out = vals[epg - 1]
        for j in range(epg - 2, -1, -1):
            out = jnp.where(idx == j, vals[j], out)
        return out

    l1 = first_best(vb)
    l2 = first_best([jnp.where(l1 == j, -jnp.inf, vb[j]) for j in range(epg)])
    w1, w2 = pick(vs, l1), pick(vs, l2)
    wsum = w1 + w2
    eid_ref[0, 0:1, :] = g_sel * epg + l1
    eid_ref[0, 1:2, :] = g_sel * epg + l2
    wts_ref[0:1, :] = w1 / wsum
    wts_ref[1:2, :] = w2 / wsum

    r1, r2 = l1 * ng + g_sel, l2 * ng + g_sel
    riota = lax.broadcasted_iota(jnp.int32, (N_EXPERTS, tn), 0)
    hit1, hit2 = riota == r1, riota == r2
    onehot = jnp.where(hit1 | hit2, 1.0, 0.0)
    before = lax.broadcasted_iota(jnp.int32, (tn, tn), 0) < lax.broadcasted_iota(jnp.int32, (tn, tn), 1)
    prefix = _dot(onehot.astype(BF16), jnp.where(before, 1.0, 0.0).astype(BF16)) + base_ref[:, 0:1]
    rank_ref[0, 0:1, :] = jnp.sum(jnp.where(hit1, prefix, 0.0), axis=0, keepdims=True).astype(jnp.int32)
    rank_ref[0, 1:2, :] = jnp.sum(jnp.where(hit2, prefix, 0.0), axis=0, keepdims=True).astype(jnp.int32)
    base_ref[...] = base_ref[...] + jnp.sum(onehot, axis=1, keepdims=True)
    cnt_ref[...] = base_ref[...]


def _route(h2c, wr_t, bias_col):
    n = h2c.shape[0] // ROW_CHUNKS
    d = wr_t.shape[1]
    tn = TN_TOK
    steps = n // tn
    col = lambda i: (0, i)
    blk = lambda i: (i, 0, 0)
    return pl.pallas_call(
        _route_kernel,
        grid=(steps,),
        in_specs=[
            pl.BlockSpec((tn * ROW_CHUNKS, LANES), lambda i: (i, 0)),
            pl.BlockSpec((N_EXPERTS, d), lambda i: (0, 0)),
            pl.BlockSpec((N_EXPERTS, 1), lambda i: (0, 0)),
        ],
        out_specs=[pl.BlockSpec((1, TOP_K, tn), blk), pl.BlockSpec((TOP_K, tn), col),
                   pl.BlockSpec((1, TOP_K, tn), blk), pl.BlockSpec((N_EXPERTS, LANES), lambda i: (0, 0))],
        out_shape=[jax.ShapeDtypeStruct((steps, TOP_K, tn), jnp.int32), jax.ShapeDtypeStruct((TOP_K, n), F32),
                   jax.ShapeDtypeStruct((steps, TOP_K, tn), jnp.int32),
                   jax.ShapeDtypeStruct((N_EXPERTS, LANES), F32)],
        scratch_shapes=[pltpu.VMEM((N_EXPERTS, LANES), F32)],
        compiler_params=_cparams(("arbitrary",)),
        name="route",
    )(h2c, wr_t, bias_col)


def _dispatch_kernel(dest_ref, zlo_ref, tok_ref, xs_hbm, zero_ref, zsem, sem, *, t_ffn):
    i = pl.program_id(0)
    tn = dest_ref.shape[2]

    @pl.when(i == 0)
    def _():
        zero_ref[...] = jnp.zeros_like(zero_ref)

        def zcopy(e):
            return pltpu.make_async_copy(zero_ref, xs_hbm.at[pl.ds(jnp.maximum(zlo_ref[e], 0), t_ffn)], zsem)

        def start(e, carry):
            @pl.when(zlo_ref[e] >= 0)
            def _():
                zcopy(e).start()
            return carry

        def wait(e, carry):
            @pl.when(zlo_ref[e] >= 0)
            def _():
                zcopy(e).wait()
            return carry

        lax.fori_loop(0, N_EXPERTS, start, 0)
        lax.fori_loop(0, N_EXPERTS, wait, 0)

    def start(n, carry):
        src = tok_ref.at[pl.ds(pl.multiple_of(n * ROW_CHUNKS, ROW_CHUNKS), ROW_CHUNKS)]
        for slot in range(TOP_K):
            pltpu.make_async_copy(src, xs_hbm.at[dest_ref[0, slot, n]], sem).start()
        return carry

    lax.fori_loop(0, tn, start, 0, unroll=DMA_UNROLL)
    for slot in range(TOP_K):
        pltpu.make_async_copy(xs_hbm.at[pl.ds(0, tn)], xs_hbm.at[pl.ds(0, tn)], sem).wait()


def _dispatch(tokens, dest3, zlo, n_rows, t_ffn):
    steps, _, tn = dest3.shape
    smem = pltpu.SMEM
    return pl.pallas_call(
        functools.partial(_dispatch_kernel, t_ffn=t_ffn),
        grid=(steps,),
        in_specs=[
            pl.BlockSpec((1, TOP_K, tn), lambda i: (i, 0, 0), memory_space=smem),
            pl.BlockSpec(memory_space=smem),
            pl.BlockSpec((tn * ROW_CHUNKS, LANES), lambda i: (i, 0)),
        ],
        out_specs=pl.BlockSpec(memory_space=pl.ANY),
        out_shape=jax.ShapeDtypeStruct((n_rows, ROW_CHUNKS, LANES), tokens.dtype),
        scratch_shapes=[pltpu.VMEM((t_ffn, ROW_CHUNKS, LANES), tokens.dtype), pltpu.SemaphoreType.DMA,
                        pltpu.SemaphoreType.DMA],
        compiler_params=_cparams(("arbitrary",)),
        name="dispatch",
    )(dest3, zlo, tokens)


def _ffn_kernel(texp_ref, nused_ref, enext_ref, eord_ref, x_ref, wg_hbm, wu_hbm, wd_hbm, y_ref,
                wgf_ref, wuf_ref, wdf_ref, wgb_ref, wub_ref, wdb_ref, wsem, *, layer):
    j = pl.program_id(0)
    e = texp_ref[j]
    active = j < nused_ref[0]
    first = (j == 0) | (e != texp_ref[jnp.maximum(j - 1, 0)])
    slot = eord_ref[e] % 2
    pairs = ((wg_hbm, wgf_ref), (wu_hbm, wuf_ref), (wd_hbm, wdf_ref))

    def wcopies(expert, s):
        return [pltpu.make_async_copy(hbm.at[layer, expert], buf.at[s], wsem.at[s, i])
                for i, (hbm, buf) in enumerate(pairs)]

    @pl.when(active & (j == 0))
    def _():
        for cp in wcopies(e, slot):
            cp.start()

    @pl.when(active & first)
    def _():
        nxt = enext_ref[e]

        @pl.when(nxt >= 0)
        def _():
            for cp in wcopies(nxt, 1 - slot):
                cp.start()

        for cp in wcopies(e, slot):
            cp.wait()
        wgb_ref[...] = wgf_ref[slot].astype(BF16)
        wub_ref[...] = wuf_ref[slot].astype(BF16)
        wdb_ref[...] = wdf_ref[slot].astype(BF16)

    @pl.when(active)
    def _():
        x = _load_chunked(x_ref, x_ref.shape[0] // ROW_CHUNKS).astype(BF16)
        gate = _dot(x, wgb_ref[...])
        up = _dot(x, wub_ref[...])
        act = (gate * jax.nn.sigmoid(gate) * up).astype(BF16)
        _store_chunked(y_ref, _dot(act, wdb_ref[...]))


def _expert_ffn(xs, plan, w_gate, w_up, w_down, layer, t_ffn):
    tile_expert, n_used, e_next, e_ord = plan
    _, _, d, de = w_gate.shape
    n_rows = xs.shape[0] // ROW_CHUNKS
    n_tiles = n_rows // t_ffn
    rowmap = lambda j, te, nu, en, eo: (jnp.minimum(j, nu[0] - 1), 0)
    hbm = pl.BlockSpec(memory_space=pl.ANY)
    return pl.pallas_call(
        functools.partial(_ffn_kernel, layer=layer),
        grid_spec=pltpu.PrefetchScalarGridSpec(
            num_scalar_prefetch=4,
            grid=(n_tiles,),
            in_specs=[pl.BlockSpec((t_ffn * ROW_CHUNKS, LANES), rowmap), hbm, hbm, hbm],
            out_specs=pl.BlockSpec((t_ffn * ROW_CHUNKS, LANES), rowmap),
            scratch_shapes=[pltpu.VMEM((2, d, de), F32), pltpu.VMEM((2, d, de), F32), pltpu.VMEM((2, de, d), F32),
                            pltpu.VMEM((d, de), BF16), pltpu.VMEM((d, de), BF16), pltpu.VMEM((de, d), BF16),
                            pltpu.SemaphoreType.DMA((2, 3))],
        ),
        out_shape=jax.ShapeDtypeStruct((n_rows * ROW_CHUNKS, LANES), F32),
        compiler_params=_cparams(("arbitrary",)),
        name="expert_ffn",
    )(tile_expert, n_used, e_next, e_ord, xs, w_gate, w_up, w_down)


def _combine_kernel(dest_ref, dest_next_ref, y_hbm, wts_ref, x1_ref, mod_ref, lng_ref, lnb_ref, x2_ref,
                    buf_ref, sem, *, alpha):
    i = pl.program_id(0)
    n_steps = pl.num_programs(0)
    tc, d = x1_ref.shape

    def issue(idx_ref, step):
        def body(n, carry):
            rows = pl.ds(pl.multiple_of(n * ROW_CHUNKS, ROW_CHUNKS), ROW_CHUNKS)
            for slot in range(TOP_K):
                pltpu.make_async_copy(y_hbm.at[idx_ref[0, slot, n]],
                                      buf_ref.at[(step % 2) * TOP_K + slot, rows], sem.at[step % 2]).start()
            return carry
        lax.fori_loop(0, tc, body, 0, unroll=DMA_UNROLL)

    @pl.when(i == 0)
    def _():
        issue(dest_ref, i)

    @pl.when(i + 1 < n_steps)
    def _():
        issue(dest_next_ref, i + 1)

    for slot in range(TOP_K):
        pltpu.make_async_copy(y_hbm.at[pl.ds(0, tc)], y_hbm.at[pl.ds(0, tc)], sem.at[i % 2]).wait()
    w = wts_ref[...]
    cur = (i % 2) * TOP_K
    f = w[:, 0:1] * _load_chunked(buf_ref.at[cur], tc) + w[:, 1:2] * _load_chunked(buf_ref.at[cur + 1], tc)
    g2 = mod_ref[0][:, 5 * d:6 * d]
    x2_ref[...] = _layer_norm(alpha * x1_ref[...] + g2 * f) * lng_ref[...] + lnb_ref[...]


def _combine(y, dest3, wts_t, x1, mods, lng, lnb, *, tok0, seq_len, mod_row, alpha):
    n, d = x1.shape
    tc = dest3.shape[2]
    tps = max(seq_len // tc, 1)
    steps = n // tc
    blk0 = tok0 // tc
    mod_map = (lambda i: (i // tps, 0, 0)) if mod_row is None else (lambda i: (mod_row, 0, 0))
    cur = lambda i: (blk0 + i, 0, 0)
    nxt = lambda i: (blk0 + jnp.minimum(i + 1, steps - 1), 0, 0)
    smem = pltpu.SMEM
    return pl.pallas_call(
        functools.partial(_combine_kernel, alpha=alpha),
        grid=(steps,),
        in_specs=[
            pl.BlockSpec((1, TOP_K, tc), cur, memory_space=smem),
            pl.BlockSpec((1, TOP_K, tc), nxt, memory_space=smem),
            pl.BlockSpec(memory_space=pl.ANY),
            pl.BlockSpec((tc, TOP_K), lambda i: (blk0 + i, 0)),
            pl.BlockSpec((tc, d), lambda i: (i, 0)),
            pl.BlockSpec((1, 1, mods.shape[2]), mod_map),
            pl.BlockSpec(lng.shape, lambda i: (0, 0)),
            pl.BlockSpec(lnb.shape, lambda i: (0, 0)),
        ],
        out_specs=pl.BlockSpec((tc, d), lambda i: (i, 0)),
        out_shape=jax.ShapeDtypeStruct((n, d), F32),
        scratch_shapes=[pltpu.VMEM((2 * TOP_K, tc * ROW_CHUNKS, LANES), F32), pltpu.SemaphoreType.DMA((2,))],
        compiler_params=_cparams(("arbitrary",)),
        name="combine",
    )(dest3, dest3, y, wts_t, x1, mods, lng, lnb)


def _rope_tables(seq_len):
    t = jnp.arange(seq_len)
    row = (t // GRID_W).astype(F32)
    col = (t % GRID_W).astype(F32)
    half = HEAD_DIM // 2
    inv_freq = ROPE_BASE ** (-jnp.arange(0, half, 2, dtype=F32) / half)
    ang_r, ang_c = row[:, None] * inv_freq, col[:, None] * inv_freq
    cr, sr, cc, sc = jnp.cos(ang_r), jnp.sin(ang_r), jnp.cos(ang_c), jnp.sin(ang_c)
    cos = jnp.concatenate([cr, cr, cc, cc], axis=1)
    sin = jnp.concatenate([-sr, sr, -sc, sc], axis=1)
    return jnp.tile(cos, (1, LANES // HEAD_DIM)), jnp.tile(sin, (1, LANES // HEAD_DIM))


def _pair_heads(a, axis):
    shp = a.shape
    a = a.reshape(*shp[:axis], N_KV_HEADS, Q_REP, HEAD_DIM, *shp[axis + 1:])
    a = jnp.swapaxes(a, axis, axis + 1)
    return a.reshape(shp)


def _moe_plan(eid3, rank3, counts, n_tok, t_ffn):
    counts = counts.reshape(EXPERTS_PER_GROUP, N_EXPERT_GROUPS).T.reshape(N_EXPERTS).astype(jnp.int32)
    padded = (counts + t_ffn - 1) // t_ffn * t_ffn
    pends = jnp.cumsum(padded)
    pstarts = (pends - padded).astype(jnp.int32)
    experts = jnp.arange(N_EXPERTS, dtype=jnp.int32)
    dest3 = rank3 + jnp.sum(jnp.where(eid3[..., None] == experts, pstarts, 0), axis=-1)
    n_rows = -(-(n_tok * TOP_K) // t_ffn) * t_ffn + N_EXPERTS * t_ffn
    n_tiles = n_rows // t_ffn
    tile_row0 = jnp.arange(n_tiles, dtype=jnp.int32) * t_ffn
    tile_expert = jnp.minimum(jnp.sum(pends[None, :] <= tile_row0[:, None], axis=1), N_EXPERTS - 1)
    n_used = (pends[-1] // t_ffn).astype(jnp.int32).reshape(1)
    nonempty = counts > 0
    later = (experts[None, :] > experts[:, None]) & nonempty[None, :]
    e_next = jnp.min(jnp.where(later, experts[None, :], N_EXPERTS), axis=1)
    e_next = jnp.where(e_next < N_EXPERTS, e_next, -1).astype(jnp.int32)
    e_ord = (jnp.cumsum(nonempty.astype(jnp.int32)) - 1).astype(jnp.int32)
    zlo = jnp.where(nonempty, pends - t_ffn, -1).astype(jnp.int32)
    plan = (tile_expert.astype(jnp.int32), n_used, e_next, e_ord)
    return dest3.astype(jnp.int32), plan, zlo, n_rows


def kernel(x, c, ctx, c_ctx, w_ada, b_ada, w_in, w_pool_grp, pool_scale, w_pool_br, w_attn_br, attn_sink,
           w_o, ln1_g, ln1_b, w_router, router_bias, w_exp_gate, w_exp_up, w_exp_down, ln2_g, ln2_b):
    bsz, seq, d = x.shape
    assert d == ROW_CHUNKS * LANES
    c_len = ctx.shape[1]
    depth = w_in.shape[0]
    n_lat, n_ctx = bsz * seq, bsz * c_len
    alpha = (2 * depth) ** 0.25

    cond = jnp.zeros((8, d), F32).at[:bsz].set(c).at[bsz].set(c_ctx)
    ada = _ada_terms(cond, w_ada, b_ada)
    cos, sin = _rope_tables(seq)
    wr_t = w_router.reshape(d, N_EXPERT_GROUPS, EXPERTS_PER_GROUP).transpose(2, 1, 0).reshape(N_EXPERTS, d)
    wr_t = wr_t.astype(BF16)
    bias_col = router_bias.reshape(N_EXPERT_GROUPS, EXPERTS_PER_GROUP).T.reshape(N_EXPERTS, 1).astype(F32)

    xl = x.reshape(n_lat, d)
    xc = ctx.reshape(n_ctx, d)
    for l in range(depth):
        ctx_out = l < depth - 1
        mods = ada[l].reshape(8, 1, 6 * d)
        w_l = w_in[l]
        w_inp = jnp.concatenate([w_l[:, :COL_Q], _pair_heads(w_l[:, COL_Q:COL_K], 1), w_l[:, COL_K:]],
                                axis=1).astype(BF16)
        sink_col = jnp.broadcast_to(attn_sink[l].reshape(N_KV_HEADS, Q_REP, 1, 1),
                                    (N_KV_HEADS, Q_REP, BLOCK, 1)).reshape(N_KV_HEADS, Q_REP * BLOCK, 1)
        mix_w = (w_pool_grp[l].astype(BF16), pool_scale[l].reshape(1, POOL_W), w_pool_br[l].astype(BF16),
                 _pair_heads(w_attn_br[l], 0).astype(BF16), w_o[l].astype(BF16), sink_col.astype(F32),
                 ln1_g[l].reshape(1, d), ln1_b[l].reshape(1, d))
        lng2, lnb2 = ln2_g[l].reshape(1, d), ln2_b[l].reshape(1, d)

        if ctx_out:
            uc, qc, kc, vc, gc = _inproj(xc, mods, w_inp, cos, sin, seq_len=c_len, mod_row=bsz, rope=False)
        else:
            kc, vc = _inproj(xc, mods, w_inp, cos, sin, seq_len=c_len, mod_row=bsz, rope=False, kv_only=True)
        kc3, vc3 = kc.reshape(bsz, c_len, KV_W), vc.reshape(bsz, c_len, KV_W)
        u, q, k, v, g = _inproj(xl, mods, w_inp, cos, sin, seq_len=seq, mod_row=None, rope=True)
        n_tok = n_lat + n_ctx if ctx_out else n_lat
        tokens = None
        if ctx_out:
            xc1, tokens = _mixer(uc, qc, kc, vc, kc3, vc3, gc, xc, mods, mix_w, seq_len=c_len, mod_row=bsz,
                                 local=False, alpha=alpha, h2_tokens=n_tok, h2_tok0=n_lat)
        x1, tokens = _mixer(u, q, k, v, kc3, vc3, g, xl, mods, mix_w, seq_len=seq, mod_row=None, local=True,
                            alpha=alpha, h2_tokens=n_tok, h2_tok0=0, h2_buf=tokens)

        eid3, wts, rank3, counts = _route(tokens, wr_t, bias_col)
        dest3, plan, zlo, n_rows = _moe_plan(eid3, rank3, counts[:, 0], n_tok, T_FFN)
        xs = _dispatch(tokens, dest3, zlo, n_rows, T_FFN)
        y = _expert_ffn(xs.reshape(n_rows * ROW_CHUNKS, LANES), plan, w_exp_gate, w_exp_up, w_exp_down, l, T_FFN)
        y3 = y.reshape(n_rows, ROW_CHUNKS, LANES)
        wts_t = wts.T
        xl = _combine(y3, dest3, wts_t, x1, mods, lng2, lnb2, tok0=0, seq_len=seq, mod_row=None, alpha=alpha)
        if ctx_out:
            xc = _combine(y3, dest3, wts_t, xc1, mods, lng2, lnb2, tok0=n_lat, seq_len=c_len, mod_row=bsz,
                          alpha=alpha)
    return xl.reshape(bsz, seq, d)
```

```python
import functools

import jax
import jax.numpy as jnp
from jax import lax
from jax.experimental import pallas as pl
from jax.experimental.pallas import tpu as pltpu

F32 = jnp.float32
BF16 = jnp.bfloat16

GRID_W = 64
POOL_WINDOWS = (2, 4, 8, 16)
POOL_GROUP_W = 128
POOL_W = 512
HEAD_DIM = 64
N_HEADS = 8
N_KV_HEADS = 2
Q_REP = N_HEADS // N_KV_HEADS
ATTN_W = N_HEADS * HEAD_DIM
KV_W = N_KV_HEADS * HEAD_DIM
BLOCK = 128
ROPE_BASE = 10000.0
COL_POOL = 0
COL_Q = COL_POOL + POOL_W
COL_K = COL_Q + ATTN_W
COL_V = COL_K + KV_W
COL_GATE = COL_V + KV_W
N_EXPERTS = 32
N_EXPERT_GROUPS = 8
EXPERTS_PER_GROUP = N_EXPERTS // N_EXPERT_GROUPS
TOP_K = 2
LN_EPS = 1e-6
LOG2E = 1.4426950408889634

LANES = 128
POOL_HALO = 8
ROW_CHUNKS = 8
VMEM_LIMIT = 56 * 1024 * 1024

TM_IN = 512
TQ_ATT = 512
TQ_MIX = 512
TN_TOK = 512
T_FFN = 256
DMA_UNROLL = 8


def _cparams(sem, flags=None):
    return pltpu.CompilerParams(dimension_semantics=sem, vmem_limit_bytes=VMEM_LIMIT, flags=flags)


def _layer_norm(x):
    mu = jnp.mean(x, axis=-1, keepdims=True)
    xc = x - mu
    var = jnp.mean(xc * xc, axis=-1, keepdims=True)
    return xc * lax.rsqrt(var + LN_EPS)


def _dot(a, b):
    return jnp.dot(a, b, preferred_element_type=F32)


def _store_chunked(ref, val):
    t = val.shape[0]
    for s in range(ROW_CHUNKS):
        ref[pl.ds(s, t, stride=ROW_CHUNKS), :] = val[:, s * LANES:(s + 1) * LANES]


def _load_chunked(ref, t):
    return jnp.concatenate([ref[pl.ds(s, t, stride=ROW_CHUNKS), :] for s in range(ROW_CHUNKS)], axis=1)


def _ada_kernel(cond_ref, w_ref, b_ref, o_ref):
    s = cond_ref[...]
    s = s * jax.nn.sigmoid(s)
    o_ref[0] = _dot(s.astype(BF16), w_ref[0].astype(BF16)) + b_ref[0]


def _ada_terms(cond, w_ada, b_ada):
    depth, d, n6 = w_ada.shape
    tn = n6 // 4
    return pl.pallas_call(
        _ada_kernel,
        grid=(depth, n6 // tn),
        in_specs=[
            pl.BlockSpec((8, d), lambda l, j: (0, 0)),
            pl.BlockSpec((1, d, tn), lambda l, j: (l, 0, j)),
            pl.BlockSpec((1, 1, tn), lambda l, j: (l, 0, j)),
        ],
        out_specs=pl.BlockSpec((1, 8, tn), lambda l, j: (l, 0, j)),
        out_shape=jax.ShapeDtypeStruct((depth, 8, n6), F32),
        compiler_params=_cparams(("arbitrary", "arbitrary")),
        name="ada_terms",
    )(cond, w_ada, b_ada.reshape(depth, 1, n6))


def _rope(t, cos, sin):
    lane = lax.broadcasted_iota(jnp.int32, (1, LANES), 1)
    first = (lane % 32) < 16
    outs = []
    for j in range(t.shape[1] // LANES):
        tj = t[:, j * LANES:(j + 1) * LANES]
        partner = jnp.where(first, pltpu.roll(tj, LANES - 16, 1), pltpu.roll(tj, 16, 1))
        outs.append(tj * cos + partner * sin)
    return outs[0] if len(outs) == 1 else jnp.concatenate(outs, axis=1)


def _inproj_kernel(x_ref, mod_ref, w_ref, cos_ref, sin_ref, *out_refs, rope, kv_only):
    d = x_ref.shape[1]
    mod = mod_ref[0]
    shift, scale = mod[:, 0:d], mod[:, d:2 * d]
    h = (_layer_norm(x_ref[...]) * (1.0 + scale) + shift).astype(BF16)

    def proj(lo, hi):
        return _dot(h, w_ref[:, lo:hi])

    if kv_only:
        k_ref, v_ref = out_refs
    else:
        u_ref, q_ref, k_ref, v_ref, g_ref = out_refs
        u_ref[...] = proj(COL_POOL, COL_Q)
        q = proj(COL_Q, COL_K)
        if rope:
            q = _rope(q, cos_ref[...], sin_ref[...])
        q_ref[...] = (q * (LOG2E * HEAD_DIM ** -0.5)).astype(BF16)
        g_ref[...] = jax.nn.sigmoid(proj(COL_GATE, w_ref.shape[1]))
    kv = proj(COL_K, COL_GATE)
    k = kv[:, 0:KV_W]
    if rope:
        k = _rope(k, cos_ref[...], sin_ref[...])
    k_ref[...] = k.astype(BF16)
    v_ref[...] = kv[:, KV_W:2 * KV_W].astype(BF16)


def _inproj(x2d, mods, w_in, cos, sin, *, seq_len, mod_row, rope, kv_only=False):
    n, d = x2d.shape
    tm = min(TM_IN, seq_len)
    tps = seq_len // tm
    n_cols = w_in.shape[1]
    mod_map = (lambda i: (i // tps, 0, 0)) if mod_row is None else (lambda i: (mod_row, 0, 0))
    tab_map = (lambda i: (i % tps, 0)) if rope else (lambda i: (0, 0))
    row = lambda i: (i, 0)
    kv_shapes = [jax.ShapeDtypeStruct((n, KV_W), BF16)] * 2
    kv_specs = [pl.BlockSpec((tm, KV_W), row)] * 2
    if kv_only:
        out_shape, out_specs = kv_shapes, kv_specs
    else:
        out_shape = [jax.ShapeDtypeStruct((n, POOL_W), F32), jax.ShapeDtypeStruct((n, ATTN_W), BF16),
                     *kv_shapes, jax.ShapeDtypeStruct((n, n_cols - COL_GATE), F32)]
        out_specs = [pl.BlockSpec((tm, POOL_W), row), pl.BlockSpec((tm, ATTN_W), row),
                     *kv_specs, pl.BlockSpec((tm, n_cols - COL_GATE), row)]
    return pl.pallas_call(
        functools.partial(_inproj_kernel, rope=rope, kv_only=kv_only),
        grid=(n // tm,),
        in_specs=[
            pl.BlockSpec((tm, d), row),
            pl.BlockSpec((1, 1, mods.shape[2]), mod_map),
            pl.BlockSpec((d, n_cols), lambda i: (0, 0)),
            pl.BlockSpec((tm, LANES), tab_map),
            pl.BlockSpec((tm, LANES), tab_map),
        ],
        out_specs=out_specs,
        out_shape=out_shape,
        compiler_params=_cparams(("parallel",)),
        name="inproj_kv" if kv_only else "inproj",
    )(x2d, mods, w_in, cos, sin)


def _merge_kernel(u_prev_ref, u_ref, u_next_ref, attn_ref, g_ref, x_ref, mod_ref,
                  wgrp_ref, pscale_ref, wpool_ref, wattn_ref, wo_ref, lng_ref, lnb_ref,
                  *rest, seq_len, alpha):
    x1_ref, h2_ref, uext_ref = rest[-3:]
    tq, d = x_ref.shape
    tps = seq_len // tq
    t_in_seq = pl.program_id(0) % tps
    is_first = t_in_seq == 0
    is_last = t_in_seq == tps - 1

    h8 = POOL_HALO
    uext_ref[0:h8, :] = jnp.where(is_first, 0.0, u_prev_ref[...])
    uext_ref[h8:h8 + tq, :] = u_ref[...]
    uext_ref[h8 + tq:, :] = jnp.where(is_last, 0.0, u_next_ref[...])
    pos = t_in_seq * tq + lax.broadcasted_iota(jnp.int32, (tq, 1), 0)
    pooled = []
    for gi, w in enumerate(POOL_WINDOWS):
        cols = slice(gi * POOL_GROUP_W, (gi + 1) * POOL_GROUP_W)
        acc = uext_ref[h8 - w // 2:h8 - w // 2 + tq, cols]
        for off in range(-w // 2 + 1, w // 2):
            acc = acc + uext_ref[h8 + off:h8 + off + tq, cols]
        lo = jnp.maximum(pos - w // 2, 0)
        hi = jnp.minimum(pos - w // 2 + w - 1, seq_len - 1)
        mean = acc / (hi - lo + 1).astype(F32)
        pg = (mean - u_ref[:, cols]).astype(BF16)
        pooled.append(_dot(pg, wgrp_ref[gi]))
    pool_lat = jnp.concatenate(pooled, axis=1) * pscale_ref[...]
    pool_proj = _dot(pool_lat.astype(BF16), wpool_ref[...])
    attn_proj = _dot(attn_ref[...], wattn_ref[...])

    gates = g_ref[...]
    merged = gates[:, 0:d] * pool_proj + gates[:, d:2 * d] * attn_proj
    y = _dot(merged.astype(BF16), wo_ref[...])
    mod = mod_ref[0]
    g1 = mod[:, 2 * d:3 * d]
    sh2, sc2 = mod[:, 3 * d:4 * d], mod[:, 4 * d:5 * d]
    x1 = _layer_norm(alpha * x_ref[...] + g1 * y) * lng_ref[...] + lnb_ref[...]
    x1_ref[...] = x1
    _store_chunked(h2_ref, _layer_norm(x1) * (1.0 + sc2) + sh2)


def _attn_kernel(q_ref, k_prev_ref, k_ref, k_next_ref, v_prev_ref, v_ref, v_next_ref, kc_ref, vc_ref,
                 sink_ref, attn_ref, *, seq_len, local):
    tq = q_ref.shape[0]
    nb = tq // BLOCK
    tps = seq_len // tq
    t_in_seq = pl.program_id(0) % tps
    is_first = t_in_seq == 0
    is_last = t_in_seq == tps - 1
    lane = lax.broadcasted_iota(jnp.int32, (1, LANES), 1)
    lo_half = lane < HEAD_DIM
    one = jnp.ones((), BF16)
    kc = kc_ref[0]
    vc = vc_ref[0]
    kc_g = [jnp.where(lo_half, kc, 0), jnp.where(lo_half, 0, kc)]
    vc_g = [jnp.where(lo_half, vc, one), jnp.where(lo_half, one, vc)]
    if local:
        k_ext = jnp.concatenate([k_prev_ref[...], k_ref[...], k_next_ref[...]], axis=0)
        v_ext = jnp.concatenate([v_prev_ref[...], v_ref[...], v_next_ref[...]], axis=0)
        k_g = [jnp.where(lo_half, k_ext, 0), jnp.where(lo_half, 0, k_ext)]
        v_g = [jnp.where(lo_half, v_ext, one), jnp.where(lo_half, one, v_ext)]
        qq = lax.broadcasted_iota(jnp.int32, (Q_REP * BLOCK, BLOCK), 0) % BLOCK
        kk = lax.broadcasted_iota(jnp.int32, (Q_REP * BLOCK, BLOCK), 1)
        neg = jnp.float32(-jnp.inf)
        mask_prev = jnp.where(kk >= qq, 0.0, neg)
        mask_next = jnp.where(kk <= qq, 0.0, neg)
    for b in range(nb):
        rows = slice(b * BLOCK, (b + 1) * BLOCK)
        q_st = jnp.concatenate([q_ref[rows, c * LANES:(c + 1) * LANES] for c in range(Q_REP)], axis=0)
        if local:
            keys = slice(b * BLOCK, (b + 3) * BLOCK)
            m_prev = jnp.where(is_first, neg, mask_prev) if b == 0 else mask_prev
            m_next = jnp.where(is_last, neg, mask_next) if b == nb - 1 else mask_next
        pv, sink_w = [], []
        for g in range(N_KV_HEADS):
            k_all = jnp.concatenate([k_g[g][keys], kc_g[g]], axis=0) if local else kc_g[g]
            v_all = jnp.concatenate([v_g[g][keys], vc_g[g]], axis=0) if local else vc_g[g]
            s = lax.dot_general(q_st, k_all, (((1,), (1,)), ((), ())), preferred_element_type=F32)
            if local:
                s = jnp.concatenate([s[:, 0:BLOCK] + m_prev, s[:, BLOCK:2 * BLOCK],
                                     s[:, 2 * BLOCK:3 * BLOCK] + m_next, s[:, 3 * BLOCK:]], axis=1)
            sk = sink_ref[g]
            m = jnp.maximum(jnp.max(s, axis=-1, keepdims=True), sk)
            p = jnp.exp2((s - m).astype(BF16))
            pv.append(_dot(p, v_all))
            sink_w.append(jnp.exp2(sk - m))
        num = jnp.where(lo_half, pv[0], pv[1])
        den = pltpu.roll(jnp.where(lo_half, pv[1], pv[0]), HEAD_DIM, 1) + jnp.where(lo_half, sink_w[0], sink_w[1])
        o = num / den
        for c in range(Q_REP):
            attn_ref[rows, c * LANES:(c + 1) * LANES] = o[c * BLOCK:(c + 1) * BLOCK].astype(BF16)


def _attention(q, k, v, kc, vc, sink_col, *, seq_len, local):
    n = q.shape[0]
    tq = min(TQ_ATT, seq_len)
    tps = seq_len // tq
    kb = tq // BLOCK
    n_kb = n // BLOCK
    c_len = kc.shape[1]
    row = lambda i: (i, 0)
    ctx_map = lambda i: (i // tps, 0, 0)
    kv_prev = pl.BlockSpec((BLOCK, KV_W), lambda i: (jnp.maximum(i * kb - 1, 0), 0))
    kv_cur = pl.BlockSpec((tq, KV_W), row)
    kv_next = pl.BlockSpec((BLOCK, KV_W), lambda i: (jnp.minimum((i + 1) * kb, n_kb - 1), 0))
    return pl.pallas_call(
        functools.partial(_attn_kernel, seq_len=seq_len, local=local),
        grid=(n // tq,),
        in_specs=[
            pl.BlockSpec((tq, ATTN_W), row),
            kv_prev, kv_cur, kv_next, kv_prev, kv_cur, kv_next,
            pl.BlockSpec((1, c_len, KV_W), ctx_map), pl.BlockSpec((1, c_len, KV_W), ctx_map),
            pl.BlockSpec(sink_col.shape, lambda i: (0, 0, 0)),
        ],
        out_specs=pl.BlockSpec((tq, ATTN_W), row),
        out_shape=jax.ShapeDtypeStruct((n, ATTN_W), BF16),
        compiler_params=_cparams(("parallel",)),
        name="attention" if local else "attention_ctx",
    )(q, k, k, k, v, v, v, kc, vc, sink_col)


def _merge(u, attn, gates, x2d, mods, wts, *, seq_len, mod_row, alpha, h2_tokens, h2_tok0, h2_buf=None):
    n, d = x2d.shape
    tq = min(TQ_MIX, seq_len)
    h2_blk0 = h2_tok0 // tq
    tps = seq_len // tq
    hb = tq // POOL_HALO
    n_hb = n // POOL_HALO
    row = lambda i: (i, 0)
    const2 = lambda i: (0, 0)
    const3 = lambda i: (0, 0, 0)
    mod_map = (lambda i: (i // tps, 0, 0)) if mod_row is None else (lambda i: (mod_row, 0, 0))
    u_prev = pl.BlockSpec((POOL_HALO, POOL_W), lambda i: (jnp.maximum(i * hb - 1, 0), 0))
    u_next = pl.BlockSpec((POOL_HALO, POOL_W), lambda i: (jnp.minimum((i + 1) * hb, n_hb - 1), 0))
    wgrp, pscale, wpool, wattn, wo, lng, lnb = wts
    operands = [u, u, u, attn, gates, x2d, mods, wgrp, pscale, wpool, wattn, wo, lng, lnb]
    alias_specs, aliases = [], {}
    if h2_buf is not None:
        alias_specs = [pl.BlockSpec(memory_space=pl.ANY)]
        aliases = {len(operands): 1}
        operands.append(h2_buf)
    return pl.pallas_call(
        functools.partial(_merge_kernel, seq_len=seq_len, alpha=alpha),
        grid=(n // tq,),
        input_output_aliases=aliases,
        in_specs=[
            u_prev, pl.BlockSpec((tq, POOL_W), row), u_next,
            pl.BlockSpec((tq, ATTN_W), row),
            pl.BlockSpec((tq, 2 * d), row),
            pl.BlockSpec((tq, d), row),
            pl.BlockSpec((1, 1, mods.shape[2]), mod_map),
            pl.BlockSpec(wgrp.shape, const3), pl.BlockSpec(pscale.shape, const2),
            pl.BlockSpec(wpool.shape, const2), pl.BlockSpec(wattn.shape, const2),
            pl.BlockSpec(wo.shape, const2),
            pl.BlockSpec(lng.shape, const2), pl.BlockSpec(lnb.shape, const2),
            *alias_specs,
        ],
        out_specs=[pl.BlockSpec((tq, d), row),
                   pl.BlockSpec((tq * ROW_CHUNKS, LANES), lambda i: (h2_blk0 + i, 0))],
        out_shape=[jax.ShapeDtypeStruct((n, d), F32),
                   jax.ShapeDtypeStruct((h2_tokens * ROW_CHUNKS, LANES), F32)],
        scratch_shapes=[pltpu.VMEM((tq + 2 * POOL_HALO, POOL_W), F32)],
        compiler_params=_cparams(("parallel",)),
        name="merge",
    )(*operands)


def _route_kernel(h_ref, wr_ref, bias_ref, eid_ref, wts_ref, rank_ref, cnt_ref, base_ref):
    tn = h_ref.shape[0] // ROW_CHUNKS
    ng, epg = N_EXPERT_GROUPS, EXPERTS_PER_GROUP

    @pl.when(pl.program_id(0) == 0)
    def _():
        base_ref[...] = jnp.zeros_like(base_ref)

    logits = lax.dot_general(wr_ref[...], _load_chunked(h_ref, tn).astype(BF16), (((1,), (1,)), ((), ())),
                             preferred_element_type=F32)
    scores = jax.nn.sigmoid(logits)
    biased = scores + bias_ref[...]
    bj = [biased[j * ng:(j + 1) * ng] for j in range(epg)]
    sj = [scores[j * ng:(j + 1) * ng] for j in range(epg)]
    hi01, lo01 = jnp.maximum(bj[0], bj[1]), jnp.minimum(bj[0], bj[1])
    hi23, lo23 = jnp.maximum(bj[2], bj[3]), jnp.minimum(bj[2], bj[3])
    gscore = jnp.maximum(hi01, hi23) + jnp.maximum(jnp.minimum(hi01, hi23), jnp.maximum(lo01, lo23))
    giota = lax.broadcasted_iota(jnp.int32, (ng, tn), 0)
    gmax = jnp.max(gscore, axis=0, keepdims=True)
    g_first = jnp.min(jnp.where(gscore == gmax, giota.astype(F32), float(ng)), axis=0, keepdims=True)
    g_sel = g_first.astype(jnp.int32)
    in_g = giota == g_sel
    vb = [jnp.sum(jnp.where(in_g, b, 0.0), axis=0, keepdims=True) for b in bj]
    vs = [jnp.sum(jnp.where(in_g, s, 0.0), axis=0, keepdims=True) for s in sj]

    def first_best(vals):
        best = functools.reduce(jnp.maximum, vals)
        idx = jnp.full(best.shape, epg - 1, jnp.int32)
        for j in range(epg - 2, -1, -1):
            idx = jnp.where(vals[j] == best, j, idx)
        return idx

    def pick(vals, idx):
        out = vals[epg - 1]
        for j in range(epg - 2, -1, -1):
            out = jnp.where(idx == j, vals[j], out)
        return out

    l1 = first_best(vb)
    l2 = first_best([jnp.where(l1 == j, -jnp.inf, vb[j]) for j in range(epg)])
    w1, w2 = pick(vs, l1), pick(vs, l2)
    wsum = w1 + w2
    eid_ref[0, 0:1, :] = g_sel * epg + l1
    eid_ref[0, 1:2, :] = g_sel * epg + l2
    wts_ref[0:1, :] = w1 / wsum
    wts_ref[1:2, :] = w2 / wsum

    r1, r2 = l1 * ng + g_sel, l2 * ng + g_sel
    riota = lax.broadcasted_iota(jnp.int32, (N_EXPERTS, tn), 0)
    hit1, hit2 = riota == r1, riota == r2
    onehot = jnp.where(hit1 | hit2, 1.0, 0.0)
    before = lax.broadcasted_iota(jnp.int32, (tn, tn), 0) < lax.broadcasted_iota(jnp.int32, (tn, tn), 1)
    prefix = _dot(onehot.astype(BF16), jnp.where(before, 1.0, 0.0).astype(BF16)) + base_ref[:, 0:1]
    rank_ref[0, 0:1, :] = jnp.sum(jnp.where(hit1, prefix, 0.0), axis=0, keepdims=True).astype(jnp.int32)
    rank_ref[0, 1:2, :] = jnp.sum(jnp.where(hit2, prefix, 0.0), axis=0, keepdims=True).astype(jnp.int32)
    base_ref[...] = base_ref[...] + jnp.sum(onehot, axis=1, keepdims=True)
    cnt_ref[...] = base_ref[...]


def _route(h2c, wr_t, bias_col):
    n = h2c.shape[0] // ROW_CHUNKS
    d = wr_t.shape[1]
    tn = TN_TOK
    steps = n // tn
    col = lambda i: (0, i)
    blk = lambda i: (i, 0, 0)
    return pl.pallas_call(
        _route_kernel,
        grid=(steps,),
        in_specs=[
            pl.BlockSpec((tn * ROW_CHUNKS, LANES), lambda i: (i, 0)),
            pl.BlockSpec((N_EXPERTS, d), lambda i: (0, 0)),
            pl.BlockSpec((N_EXPERTS, 1), lambda i: (0, 0)),
        ],
        out_specs=[pl.BlockSpec((1, TOP_K, tn), blk), pl.BlockSpec((TOP_K, tn), col),
                   pl.BlockSpec((1, TOP_K, tn), blk), pl.BlockSpec((N_EXPERTS, LANES), lambda i: (0, 0))],
        out_shape=[jax.ShapeDtypeStruct((steps, TOP_K, tn), jnp.int32), jax.ShapeDtypeStruct((TOP_K, n), F32),
                   jax.ShapeDtypeStruct((steps, TOP_K, tn), jnp.int32),
                   jax.ShapeDtypeStruct((N_EXPERTS, LANES), F32)],
        scratch_shapes=[pltpu.VMEM((N_EXPERTS, LANES), F32)],
        compiler_params=_cparams(("arbitrary",)),
        name="route",
    )(h2c, wr_t, bias_col)


def _dispatch_kernel(dest_ref, zlo_ref, tok_ref, xs_hbm, zero_ref, zsem, sem, *, t_ffn):
    i = pl.program_id(0)
    tn = dest_ref.shape[2]

    @pl.when(i == 0)
    def _():
        zero_ref[...] = jnp.zeros_like(zero_ref)

        def zcopy(e):
            return pltpu.make_async_copy(zero_ref, xs_hbm.at[pl.ds(jnp.maximum(zlo_ref[e], 0), t_ffn)], zsem)

        def start(e, carry):
            @pl.when(zlo_ref[e] >= 0)
            def _():
                zcopy(e).start()
            return carry

        def wait(e, carry):
            @pl.when(zlo_ref[e] >= 0)
            def _():
                zcopy(e).wait()
            return carry

        lax.fori_loop(0, N_EXPERTS, start, 0)
        lax.fori_loop(0, N_EXPERTS, wait, 0)

    def start(n, carry):
        src = tok_ref.at[pl.ds(pl.multiple_of(n * ROW_CHUNKS, ROW_CHUNKS), ROW_CHUNKS)]
        for slot in range(TOP_K):
            pltpu.make_async_copy(src, xs_hbm.at[dest_ref[0, slot, n]], sem).start(priority=slot)
        return carry

    lax.fori_loop(0, tn, start, 0, unroll=DMA_UNROLL)
    for slot in range(TOP_K):
        pltpu.make_async_copy(xs_hbm.at[pl.ds(0, tn)], xs_hbm.at[pl.ds(0, tn)], sem).wait()


def _dispatch(tokens, dest3, zlo, n_rows, t_ffn):
    steps, _, tn = dest3.shape
    smem = pltpu.SMEM
    return pl.pallas_call(
        functools.partial(_dispatch_kernel, t_ffn=t_ffn),
        grid=(steps,),
        in_specs=[
            pl.BlockSpec((1, TOP_K, tn), lambda i: (i, 0, 0), memory_space=smem),
            pl.BlockSpec(memory_space=smem),
            pl.BlockSpec((tn * ROW_CHUNKS, LANES), lambda i: (i, 0)),
        ],
        out_specs=pl.BlockSpec(memory_space=pl.ANY),
        out_shape=jax.ShapeDtypeStruct((n_rows, ROW_CHUNKS, LANES), tokens.dtype),
        scratch_shapes=[pltpu.VMEM((t_ffn, ROW_CHUNKS, LANES), tokens.dtype), pltpu.SemaphoreType.DMA,
                        pltpu.SemaphoreType.DMA],
        compiler_params=_cparams(("arbitrary",)),
        name="dispatch",
    )(dest3, zlo, tokens)


def _ffn_kernel(texp_ref, nused_ref, enext_ref, eord_ref, x_ref, wg_hbm, wu_hbm, wd_hbm, y_ref,
                wgf_ref, wuf_ref, wdf_ref, wgb_ref, wub_ref, wdb_ref, wsem, *, layer):
    j = pl.program_id(0)
    e = texp_ref[j]
    active = j < nused_ref[0]
    first = (j == 0) | (e != texp_ref[jnp.maximum(j - 1, 0)])
    slot = eord_ref[e] % 2
    pairs = ((wg_hbm, wgf_ref), (wu_hbm, wuf_ref), (wd_hbm, wdf_ref))

    def wcopies(expert, s):
        return [pltpu.make_async_copy(hbm.at[layer, expert], buf.at[s], wsem.at[s, i])
                for i, (hbm, buf) in enumerate(pairs)]

    @pl.when(active & (j == 0))
    def _():
        for cp in wcopies(e, slot):
            cp.start()

    @pl.when(active & first)
    def _():
        nxt = enext_ref[e]

        @pl.when(nxt >= 0)
        def _():
            for cp in wcopies(nxt, 1 - slot):
                cp.start()

        for cp in wcopies(e, slot):
            cp.wait()
        wgb_ref[...] = wgf_ref[slot].astype(BF16)
        wub_ref[...] = wuf_ref[slot].astype(BF16)
        wdb_ref[...] = wdf_ref[slot].astype(BF16)

    @pl.when(active)
    def _():
        x = _load_chunked(x_ref, x_ref.shape[0] // ROW_CHUNKS).astype(BF16)
        gate = _dot(x, wgb_ref[...])
        up = _dot(x, wub_ref[...])
        act = (gate * jax.nn.sigmoid(gate) * up).astype(BF16)
        _store_chunked(y_ref, _dot(act, wdb_ref[...]))


def _expert_ffn(xs, plan, w_gate, w_up, w_down, layer, t_ffn):
    tile_expert, n_used, e_next, e_ord = plan
    _, _, d, de = w_gate.shape
    n_rows = xs.shape[0] // ROW_CHUNKS
    n_tiles = n_rows // t_ffn
    rowmap = lambda j, te, nu, en, eo: (jnp.minimum(j, nu[0] - 1), 0)
    hbm = pl.BlockSpec(memory_space=pl.ANY)
    return pl.pallas_call(
        functools.partial(_ffn_kernel, layer=layer),
        grid_spec=pltpu.PrefetchScalarGridSpec(
            num_scalar_prefetch=4,
            grid=(n_tiles,),
            in_specs=[pl.BlockSpec((t_ffn * ROW_CHUNKS, LANES), rowmap), hbm, hbm, hbm],
            out_specs=pl.BlockSpec((t_ffn * ROW_CHUNKS, LANES), rowmap),
            scratch_shapes=[pltpu.VMEM((2, d, de), F32), pltpu.VMEM((2, d, de), F32), pltpu.VMEM((2, de, d), F32),
                            pltpu.VMEM((d, de), BF16), pltpu.VMEM((d, de), BF16), pltpu.VMEM((de, d), BF16),
                            pltpu.SemaphoreType.DMA((2, 3))],
        ),
        out_shape=jax.ShapeDtypeStruct((n_rows * ROW_CHUNKS, LANES), F32),
        compiler_params=_cparams(("arbitrary",)),
        name="expert_ffn",
    )(tile_expert, n_used, e_next, e_ord, xs, w_gate, w_up, w_down)


def _combine_kernel(dest_ref, dest_next_ref, y_hbm, wts_ref, x1_ref, mod_ref, lng_ref, lnb_ref, x2_ref,
                    buf_ref, sem, *, alpha):
    i = pl.program_id(0)
    n_steps = pl.num_programs(0)
    tc, d = x1_ref.shape

    def issue(idx_ref, step):
        def body(n, carry):
            rows = pl.ds(pl.multiple_of(n * ROW_CHUNKS, ROW_CHUNKS), ROW_CHUNKS)
            for slot in range(TOP_K):
                pltpu.make_async_copy(y_hbm.at[idx_ref[0, slot, n]], buf_ref.at[(step % 2) * TOP_K + slot, rows],
                                      sem.at[step % 2]).start(priority=slot)
            return carry
        lax.fori_loop(0, tc, body, 0, unroll=DMA_UNROLL)

    @pl.when(i == 0)
    def _():
        issue(dest_ref, i)

    @pl.when(i + 1 < n_steps)
    def _():
        issue(dest_next_ref, i + 1)

    for slot in range(TOP_K):
        pltpu.make_async_copy(y_hbm.at[pl.ds(0, tc)], y_hbm.at[pl.ds(0, tc)], sem.at[i % 2]).wait()
    w = wts_ref[...]
    cur = (i % 2) * TOP_K
    f = w[:, 0:1] * _load_chunked(buf_ref.at[cur], tc) + w[:, 1:2] * _load_chunked(buf_ref.at[cur + 1], tc)
    g2 = mod_ref[0][:, 5 * d:6 * d]
    x2_ref[...] = _layer_norm(alpha * x1_ref[...] + g2 * f) * lng_ref[...] + lnb_ref[...]


def _combine(y, dest3, wts_t, x1, mods, lng, lnb, *, tok0, seq_len, mod_row, alpha):
    n, d = x1.shape
    tc = dest3.shape[2]
    tps = max(seq_len // tc, 1)
    steps = n // tc
    blk0 = tok0 // tc
    mod_map = (lambda i: (i // tps, 0, 0)) if mod_row is None else (lambda i: (mod_row, 0, 0))
    cur = lambda i: (blk0 + i, 0, 0)
    nxt = lambda i: (blk0 + jnp.minimum(i + 1, steps - 1), 0, 0)
    smem = pltpu.SMEM
    return pl.pallas_call(
        functools.partial(_combine_kernel, alpha=alpha),
        grid=(steps,),
        in_specs=[
            pl.BlockSpec((1, TOP_K, tc), cur, memory_space=smem),
            pl.BlockSpec((1, TOP_K, tc), nxt, memory_space=smem),
            pl.BlockSpec(memory_space=pl.ANY),
            pl.BlockSpec((tc, TOP_K), lambda i: (blk0 + i, 0)),
            pl.BlockSpec((tc, d), lambda i: (i, 0)),
            pl.BlockSpec((1, 1, mods.shape[2]), mod_map),
            pl.BlockSpec(lng.shape, lambda i: (0, 0)),
            pl.BlockSpec(lnb.shape, lambda i: (0, 0)),
        ],
        out_specs=pl.BlockSpec((tc, d), lambda i: (i, 0)),
        out_shape=jax.ShapeDtypeStruct((n, d), F32),
        scratch_shapes=[pltpu.VMEM((2 * TOP_K, tc * ROW_CHUNKS, LANES), F32), pltpu.SemaphoreType.DMA((2,))],
        compiler_params=_cparams(("arbitrary",)),
        name="combine",
    )(dest3, dest3, y, wts_t, x1, mods, lng, lnb)


def _rope_tables(seq_len):
    t = jnp.arange(seq_len)
    row = (t // GRID_W).astype(F32)
    col = (t % GRID_W).astype(F32)
    half = HEAD_DIM // 2
    inv_freq = ROPE_BASE ** (-jnp.arange(0, half, 2, dtype=F32) / half)
    ang_r, ang_c = row[:, None] * inv_freq, col[:, None] * inv_freq
    cr, sr, cc, sc = jnp.cos(ang_r), jnp.sin(ang_r), jnp.cos(ang_c), jnp.sin(ang_c)
    cos = jnp.concatenate([cr, cr, cc, cc], axis=1)
    sin = jnp.concatenate([-sr, sr, -sc, sc], axis=1)
    return jnp.tile(cos, (1, LANES // HEAD_DIM)), jnp.tile(sin, (1, LANES // HEAD_DIM))


def _pair_heads(a, axis):
    shp = a.shape
    a = a.reshape(*shp[:axis], N_KV_HEADS, Q_REP, HEAD_DIM, *shp[axis + 1:])
    a = jnp.swapaxes(a, axis, axis + 1)
    return a.reshape(shp)


def _moe_plan(eid3, rank3, counts, n_tok, t_ffn):
    counts = counts.reshape(EXPERTS_PER_GROUP, N_EXPERT_GROUPS).T.reshape(N_EXPERTS).astype(jnp.int32)
    padded = (counts + t_ffn - 1) // t_ffn * t_ffn
    pends = jnp.cumsum(padded)
    pstarts = (pends - padded).astype(jnp.int32)
    experts = jnp.arange(N_EXPERTS, dtype=jnp.int32)
    dest3 = rank3 + jnp.sum(jnp.where(eid3[..., None] == experts, pstarts, 0), axis=-1)
    n_rows = -(-(n_tok * TOP_K) // t_ffn) * t_ffn + N_EXPERTS * t_ffn
    n_tiles = n_rows // t_ffn
    tile_row0 = jnp.arange(n_tiles, dtype=jnp.int32) * t_ffn
    tile_expert = jnp.minimum(jnp.sum(pends[None, :] <= tile_row0[:, None], axis=1), N_EXPERTS - 1)
    n_used = (pends[-1] // t_ffn).astype(jnp.int32).reshape(1)
    nonempty = counts > 0
    later = (experts[None, :] > experts[:, None]) & nonempty[None, :]
    e_next = jnp.min(jnp.where(later, experts[None, :], N_EXPERTS), axis=1)
    e_next = jnp.where(e_next < N_EXPERTS, e_next, -1).astype(jnp.int32)
    e_ord = (jnp.cumsum(nonempty.astype(jnp.int32)) - 1).astype(jnp.int32)
    zlo = jnp.where(nonempty, pends - t_ffn, -1).astype(jnp.int32)
    plan = (tile_expert.astype(jnp.int32), n_used, e_next, e_ord)
    return dest3.astype(jnp.int32), plan, zlo, n_rows


def kernel(x, c, ctx, c_ctx, w_ada, b_ada, w_in, w_pool_grp, pool_scale, w_pool_br, w_attn_br, attn_sink,
           w_o, ln1_g, ln1_b, w_router, router_bias, w_exp_gate, w_exp_up, w_exp_down, ln2_g, ln2_b):
    bsz, seq, d = x.shape
    assert d == ROW_CHUNKS * LANES
    c_len = ctx.shape[1]
    depth = w_in.shape[0]
    n_lat, n_ctx = bsz * seq, bsz * c_len
    alpha = (2 * depth) ** 0.25

    cond = jnp.zeros((8, d), F32).at[:bsz].set(c).at[bsz].set(c_ctx)
    ada = _ada_terms(cond, w_ada, b_ada)
    cos, sin = _rope_tables(seq)
    wr_t = w_router.reshape(d, N_EXPERT_GROUPS, EXPERTS_PER_GROUP).transpose(2, 1, 0).reshape(N_EXPERTS, d)
    wr_t = wr_t.astype(BF16)
    bias_col = router_bias.reshape(N_EXPERT_GROUPS, EXPERTS_PER_GROUP).T.reshape(N_EXPERTS, 1).astype(F32)

    xl = x.reshape(n_lat, d)
    xc = ctx.reshape(n_ctx, d)
    for l in range(depth):
        ctx_out = l < depth - 1
        mods = ada[l].reshape(8, 1, 6 * d)
        w_l = w_in[l]
        w_inp = jnp.concatenate([w_l[:, :COL_Q], _pair_heads(w_l[:, COL_Q:COL_K], 1), w_l[:, COL_K:]],
                                axis=1).astype(BF16)
        sink_col = jnp.broadcast_to(attn_sink[l].reshape(N_KV_HEADS, Q_REP, 1, 1),
                                    (N_KV_HEADS, Q_REP, BLOCK, 1)).reshape(N_KV_HEADS, Q_REP * BLOCK, 1)
        sink_col = (sink_col * LOG2E).astype(F32)
        mix_w = (w_pool_grp[l].astype(BF16), pool_scale[l].reshape(1, POOL_W), w_pool_br[l].astype(BF16),
                 _pair_heads(w_attn_br[l], 0).astype(BF16), w_o[l].astype(BF16),
                 ln1_g[l].reshape(1, d), ln1_b[l].reshape(1, d))
        lng2, lnb2 = ln2_g[l].reshape(1, d), ln2_b[l].reshape(1, d)

        if ctx_out:
            uc, qc, kc, vc, gc = _inproj(xc, mods, w_inp, cos, sin, seq_len=c_len, mod_row=bsz, rope=False)
        else:
            kc, vc = _inproj(xc, mods, w_inp, cos, sin, seq_len=c_len, mod_row=bsz, rope=False, kv_only=True)
        kc3, vc3 = kc.reshape(bsz, c_len, KV_W), vc.reshape(bsz, c_len, KV_W)
        u, q, k, v, g = _inproj(xl, mods, w_inp, cos, sin, seq_len=seq, mod_row=None, rope=True)
        n_tok = n_lat + n_ctx if ctx_out else n_lat
        tokens = None
        if ctx_out:
            attn_c = _attention(qc, kc, vc, kc3, vc3, sink_col, seq_len=c_len, local=False)
            xc1, tokens = _merge(uc, attn_c, gc, xc, mods, mix_w, seq_len=c_len, mod_row=bsz, alpha=alpha,
                                 h2_tokens=n_tok, h2_tok0=n_lat)
        attn = _attention(q, k, v, kc3, vc3, sink_col, seq_len=seq, local=True)
        x1, tokens = _merge(u, attn, g, xl, mods, mix_w, seq_len=seq, mod_row=None, alpha=alpha,
                            h2_tokens=n_tok, h2_tok0=0, h2_buf=tokens)

        eid3, wts, rank3, counts = _route(tokens, wr_t, bias_col)
        dest3, plan, zlo, n_rows = _moe_plan(eid3, rank3, counts[:, 0], n_tok, T_FFN)
        xs = _dispatch(tokens, dest3, zlo, n_rows, T_FFN)
        y = _expert_ffn(xs.reshape(n_rows * ROW_CHUNKS, LANES), plan, w_exp_gate, w_exp_up, w_exp_down, l, T_FFN)
        y3 = y.reshape(n_rows, ROW_CHUNKS, LANES)
        wts_t = wts.T
        xl = _combine(y3, dest3, wts_t, x1, mods, lng2, lnb2, tok0=0, seq_len=seq, mod_row=None, alpha=alpha)
        if ctx_out:
            xc = _combine(y3, dest3, wts_t, xc1, mods, lng2, lnb2, tok0=n_lat, seq_len=c_len, mod_row=bsz,
                          alpha=alpha)
    return xl.reshape(bsz, seq, d)
```

```python
import functools

import jax
import jax.numpy as jnp
from jax import lax
from jax.experimental import pallas as pl
from jax.experimental.pallas import tpu as pltpu

F32 = jnp.float32
BF16 = jnp.bfloat16

GRID_W = 64
POOL_WINDOWS = (2, 4, 8, 16)
POOL_GROUP_W = 128
POOL_W = 512
HEAD_DIM = 64
N_HEADS = 8
N_KV_HEADS = 2
Q_REP = N_HEADS // N_KV_HEADS
ATTN_W = N_HEADS * HEAD_DIM
KV_W = N_KV_HEADS * HEAD_DIM
BLOCK = 128
ROPE_BASE = 10000.0
COL_POOL = 0
COL_Q = COL_POOL + POOL_W
COL_K = COL_Q + ATTN_W
COL_V = COL_K + KV_W
COL_GATE = COL_V + KV_W
N_EXPERTS = 32
N_EXPERT_GROUPS = 8
EXPERTS_PER_GROUP = N_EXPERTS // N_EXPERT_GROUPS
TOP_K = 2
LN_EPS = 1e-6
LOG2E = 1.4426950408889634

LANES = 128
POOL_HALO = 8
ROW_CHUNKS = 8
VMEM_LIMIT = 56 * 1024 * 1024

TM_IN = 512
TQ_ATT = 2048
TQ_MIX = 512
TN_TOK = 512
T_FFN = 256
DMA_UNROLL = 8


def _cparams(sem, flags=None):
    return pltpu.CompilerParams(dimension_semantics=sem, vmem_limit_bytes=VMEM_LIMIT, flags=flags)


def _layer_norm(x):
    mu = jnp.mean(x, axis=-1, keepdims=True)
    xc = x - mu
    var = jnp.mean(xc * xc, axis=-1, keepdims=True)
    return xc * lax.rsqrt(var + LN_EPS)


def _dot(a, b):
    return jnp.dot(a, b, preferred_element_type=F32)


def _store_chunked(ref, val):
    t = val.shape[0]
    for s in range(ROW_CHUNKS):
        ref[pl.ds(s, t, stride=ROW_CHUNKS), :] = val[:, s * LANES:(s + 1) * LANES]


def _load_chunked(ref, t):
    return jnp.concatenate([ref[pl.ds(s, t, stride=ROW_CHUNKS), :] for s in range(ROW_CHUNKS)], axis=1)


def _ada_kernel(cond_ref, w_ref, b_ref, o_ref):
    s = cond_ref[...]
    s = s * jax.nn.sigmoid(s)
    o_ref[0] = _dot(s.astype(BF16), w_ref[0].astype(BF16)) + b_ref[0]


def _ada_terms(cond, w_ada, b_ada):
    depth, d, n6 = w_ada.shape
    tn = n6 // 4
    return pl.pallas_call(
        _ada_kernel,
        grid=(depth, n6 // tn),
        in_specs=[
            pl.BlockSpec((8, d), lambda l, j: (0, 0)),
            pl.BlockSpec((1, d, tn), lambda l, j: (l, 0, j)),
            pl.BlockSpec((1, 1, tn), lambda l, j: (l, 0, j)),
        ],
        out_specs=pl.BlockSpec((1, 8, tn), lambda l, j: (l, 0, j)),
        out_shape=jax.ShapeDtypeStruct((depth, 8, n6), F32),
        compiler_params=_cparams(("arbitrary", "arbitrary")),
        name="ada_terms",
    )(cond, w_ada, b_ada.reshape(depth, 1, n6))


def _rope(t, cos, sin):
    lane = lax.broadcasted_iota(jnp.int32, (1, LANES), 1)
    first = (lane % 32) < 16
    outs = []
    for j in range(t.shape[1] // LANES):
        tj = t[:, j * LANES:(j + 1) * LANES]
        partner = jnp.where(first, pltpu.roll(tj, LANES - 16, 1), pltpu.roll(tj, 16, 1))
        outs.append(tj * cos + partner * sin)
    return outs[0] if len(outs) == 1 else jnp.concatenate(outs, axis=1)


def _inproj_kernel(x_ref, mod_ref, w_ref, cos_ref, sin_ref, *out_refs, rope, kv_only):
    d = x_ref.shape[1]
    mod = mod_ref[0]
    shift, scale = mod[:, 0:d], mod[:, d:2 * d]
    h = (_layer_norm(x_ref[...]) * (1.0 + scale) + shift).astype(BF16)

    def proj(lo, hi):
        return _dot(h, w_ref[:, lo:hi])

    if kv_only:
        k_ref, v_ref = out_refs
    else:
        u_ref, q_ref, k_ref, v_ref, g_ref = out_refs
        u_ref[...] = proj(COL_POOL, COL_Q)
        q = proj(COL_Q, COL_K)
        if rope:
            q = _rope(q, cos_ref[...], sin_ref[...])
        q_ref[...] = (q * (LOG2E * HEAD_DIM ** -0.5)).astype(BF16)
        g_ref[...] = jax.nn.sigmoid(proj(COL_GATE, w_ref.shape[1]))
    kv = proj(COL_K, COL_GATE)
    k = kv[:, 0:KV_W]
    if rope:
        k = _rope(k, cos_ref[...], sin_ref[...])
    k_ref[...] = k.astype(BF16)
    v_ref[...] = kv[:, KV_W:2 * KV_W].astype(BF16)


def _inproj(x2d, mods, w_in, cos, sin, *, seq_len, mod_row, rope, kv_only=False):
    n, d = x2d.shape
    tm = min(TM_IN, seq_len)
    tps = seq_len // tm
    n_cols = w_in.shape[1]
    mod_map = (lambda i: (i // tps, 0, 0)) if mod_row is None else (lambda i: (mod_row, 0, 0))
    tab_map = (lambda i: (i % tps, 0)) if rope else (lambda i: (0, 0))
    row = lambda i: (i, 0)
    kv_shapes = [jax.ShapeDtypeStruct((n, KV_W), BF16)] * 2
    kv_specs = [pl.BlockSpec((tm, KV_W), row)] * 2
    if kv_only:
        out_shape, out_specs = kv_shapes, kv_specs
    else:
        out_shape = [jax.ShapeDtypeStruct((n, POOL_W), F32), jax.ShapeDtypeStruct((n, ATTN_W), BF16),
                     *kv_shapes, jax.ShapeDtypeStruct((n, n_cols - COL_GATE), F32)]
        out_specs = [pl.BlockSpec((tm, POOL_W), row), pl.BlockSpec((tm, ATTN_W), row),
                     *kv_specs, pl.BlockSpec((tm, n_cols - COL_GATE), row)]
    return pl.pallas_call(
        functools.partial(_inproj_kernel, rope=rope, kv_only=kv_only),
        grid=(n // tm,),
        in_specs=[
            pl.BlockSpec((tm, d), row),
            pl.BlockSpec((1, 1, mods.shape[2]), mod_map),
            pl.BlockSpec((d, n_cols), lambda i: (0, 0)),
            pl.BlockSpec((tm, LANES), tab_map),
            pl.BlockSpec((tm, LANES), tab_map),
        ],
        out_specs=out_specs,
        out_shape=out_shape,
        compiler_params=_cparams(("parallel",)),
        name="inproj_kv" if kv_only else "inproj",
    )(x2d, mods, w_in, cos, sin)


def _merge_kernel(u_prev_ref, u_ref, u_next_ref, attn_ref, g_ref, x_ref, mod_ref,
                  wgrp_ref, pscale_ref, wpool_ref, wattn_ref, wo_ref, lng_ref, lnb_ref,
                  *rest, seq_len, alpha):
    x1_ref, h2_ref, uext_ref = rest[-3:]
    tq, d = x_ref.shape
    tps = seq_len // tq
    t_in_seq = pl.program_id(0) % tps
    is_first = t_in_seq == 0
    is_last = t_in_seq == tps - 1

    h8 = POOL_HALO
    uext_ref[0:h8, :] = jnp.where(is_first, 0.0, u_prev_ref[...])
    uext_ref[h8:h8 + tq, :] = u_ref[...]
    uext_ref[h8 + tq:, :] = jnp.where(is_last, 0.0, u_next_ref[...])
    pos = t_in_seq * tq + lax.broadcasted_iota(jnp.int32, (tq, 1), 0)
    pooled = []
    for gi, w in enumerate(POOL_WINDOWS):
        cols = slice(gi * POOL_GROUP_W, (gi + 1) * POOL_GROUP_W)
        acc = uext_ref[h8 - w // 2:h8 - w // 2 + tq, cols]
        for off in range(-w // 2 + 1, w // 2):
            acc = acc + uext_ref[h8 + off:h8 + off + tq, cols]
        lo = jnp.maximum(pos - w // 2, 0)
        hi = jnp.minimum(pos - w // 2 + w - 1, seq_len - 1)
        mean = acc / (hi - lo + 1).astype(F32)
        pg = (mean - u_ref[:, cols]).astype(BF16)
        pooled.append(_dot(pg, wgrp_ref[gi]))
    pool_lat = jnp.concatenate(pooled, axis=1) * pscale_ref[...]
    pool_proj = _dot(pool_lat.astype(BF16), wpool_ref[...])
    attn_proj = _dot(attn_ref[...], wattn_ref[...])

    gates = g_ref[...]
    merged = gates[:, 0:d] * pool_proj + gates[:, d:2 * d] * attn_proj
    y = _dot(merged.astype(BF16), wo_ref[...])
    mod = mod_ref[0]
    g1 = mod[:, 2 * d:3 * d]
    sh2, sc2 = mod[:, 3 * d:4 * d], mod[:, 4 * d:5 * d]
    x1 = _layer_norm(alpha * x_ref[...] + g1 * y) * lng_ref[...] + lnb_ref[...]
    x1_ref[...] = x1
    _store_chunked(h2_ref, _layer_norm(x1) * (1.0 + sc2) + sh2)


def _attn_kernel(q_ref, k_prev_ref, k_ref, k_next_ref, v_prev_ref, v_ref, v_next_ref, kc_ref, vc_ref,
                 sink_ref, attn_ref, s0_ref, s1_ref, kg_ref, vg_ref, *, seq_len, local):
    tq = q_ref.shape[0]
    nb = tq // BLOCK
    tps = seq_len // tq
    t_in_seq = pl.program_id(0) % tps
    is_first = t_in_seq == 0
    is_last = t_in_seq == tps - 1
    lane = lax.broadcasted_iota(jnp.int32, (1, LANES), 1)
    lo_half = lane < HEAD_DIM
    one = jnp.ones((), BF16)
    kc = kc_ref[0]
    vc = vc_ref[0]
    kc_g = [jnp.where(lo_half, kc, 0), jnp.where(lo_half, 0, kc)]
    vc_g = [jnp.where(lo_half, vc, one), jnp.where(lo_half, one, vc)]
    neg = jnp.float32(-jnp.inf)
    if local:
        k_ext = jnp.concatenate([k_prev_ref[...], k_ref[...], k_next_ref[...]], axis=0)
        v_ext = jnp.concatenate([v_prev_ref[...], v_ref[...], v_next_ref[...]], axis=0)
        kg_ref[0] = jnp.where(lo_half, k_ext, 0)
        kg_ref[1] = jnp.where(lo_half, 0, k_ext)
        vg_ref[0] = jnp.where(lo_half, v_ext, one)
        vg_ref[1] = jnp.where(lo_half, one, v_ext)
        qq = lax.broadcasted_iota(jnp.int32, (Q_REP * BLOCK, BLOCK), 0) % BLOCK
        kk = lax.broadcasted_iota(jnp.int32, (Q_REP * BLOCK, BLOCK), 1)
        mask_prev = jnp.where(kk >= qq, 0.0, neg)
        mask_next = jnp.where(kk <= qq, 0.0, neg)

    def block_rows(b):
        return pl.ds(b * BLOCK, BLOCK) if isinstance(b, int) else pl.ds(pl.multiple_of(b * BLOCK, BLOCK), BLOCK)

    def band_keys(b):
        start = b * BLOCK if isinstance(b, int) else pl.multiple_of(b * BLOCK, BLOCK)
        return pl.ds(start, 3 * BLOCK)

    def stage_a(b, s_buf, first_block, last_block):
        q_st = jnp.concatenate([q_ref[block_rows(b), c * LANES:(c + 1) * LANES] for c in range(Q_REP)], axis=0)
        for g in range(N_KV_HEADS):
            k_all = jnp.concatenate([kg_ref[g, band_keys(b), :], kc_g[g]], axis=0) if local else kc_g[g]
            s = lax.dot_general(q_st, k_all, (((1,), (1,)), ((), ())), preferred_element_type=F32)
            if local:
                m_prev = jnp.where(first_block, neg, mask_prev)
                m_next = jnp.where(last_block, neg, mask_next)
                s = jnp.concatenate([s[:, 0:BLOCK] + m_prev, s[:, BLOCK:2 * BLOCK],
                                     s[:, 2 * BLOCK:3 * BLOCK] + m_next, s[:, 3 * BLOCK:]], axis=1)
            s_buf[g] = s

    def stage_b(b, s_buf):
        pv, sink_w = [], []
        for g in range(N_KV_HEADS):
            v_all = jnp.concatenate([vg_ref[g, band_keys(b), :], vc_g[g]], axis=0) if local else vc_g[g]
            s = s_buf[g]
            sk = sink_ref[g]
            m = jnp.maximum(jnp.max(s, axis=-1, keepdims=True), sk)
            p = jnp.exp2((s - m).astype(BF16))
            pv.append(_dot(p, v_all))
            sink_w.append(jnp.exp2(sk - m))
        num = jnp.where(lo_half, pv[0], pv[1])
        den = pltpu.roll(jnp.where(lo_half, pv[1], pv[0]), HEAD_DIM, 1) + jnp.where(lo_half, sink_w[0], sink_w[1])
        o = num / den
        for c in range(Q_REP):
            attn_ref[block_rows(b), c * LANES:(c + 1) * LANES] = o[c * BLOCK:(c + 1) * BLOCK].astype(BF16)

    stage_a(0, s0_ref, is_first, False)

    def body(j, carry):
        stage_a(2 * j + 1, s1_ref, False, False)
        stage_b(2 * j, s0_ref)
        stage_a(2 * j + 2, s0_ref, False, False)
        stage_b(2 * j + 1, s1_ref)
        return carry

    lax.fori_loop(0, nb // 2 - 1, body, 0)
    stage_a(nb - 1, s1_ref, False, is_last)
    stage_b(nb - 2, s0_ref)
    stage_b(nb - 1, s1_ref)


def _attention(q, k, v, kc, vc, sink_col, *, seq_len, local):
    n = q.shape[0]
    tq = min(TQ_ATT, seq_len)
    tps = seq_len // tq
    kb = tq // BLOCK
    n_kb = n // BLOCK
    c_len = kc.shape[1]
    n_keys = (3 * BLOCK if local else 0) + c_len
    row = lambda i: (i, 0)
    ctx_map = lambda i: (i // tps, 0, 0)
    kv_prev = pl.BlockSpec((BLOCK, KV_W), lambda i: (jnp.maximum(i * kb - 1, 0), 0))
    kv_cur = pl.BlockSpec((tq, KV_W), row)
    kv_next = pl.BlockSpec((BLOCK, KV_W), lambda i: (jnp.minimum((i + 1) * kb, n_kb - 1), 0))
    return pl.pallas_call(
        functools.partial(_attn_kernel, seq_len=seq_len, local=local),
        grid=(n // tq,),
        in_specs=[
            pl.BlockSpec((tq, ATTN_W), row),
            kv_prev, kv_cur, kv_next, kv_prev, kv_cur, kv_next,
            pl.BlockSpec((1, c_len, KV_W), ctx_map), pl.BlockSpec((1, c_len, KV_W), ctx_map),
            pl.BlockSpec(sink_col.shape, lambda i: (0, 0, 0)),
        ],
        out_specs=pl.BlockSpec((tq, ATTN_W), row),
        out_shape=jax.ShapeDtypeStruct((n, ATTN_W), BF16),
        scratch_shapes=[pltpu.VMEM((N_KV_HEADS, Q_REP * BLOCK, n_keys), F32),
                        pltpu.VMEM((N_KV_HEADS, Q_REP * BLOCK, n_keys), F32),
                        pltpu.VMEM((N_KV_HEADS, tq + 2 * BLOCK, KV_W), BF16),
                        pltpu.VMEM((N_KV_HEADS, tq + 2 * BLOCK, KV_W), BF16)],
        compiler_params=_cparams(("parallel",)),
        name="attention" if local else "attention_ctx",
    )(q, k, k, k, v, v, v, kc, vc, sink_col)


def _merge(u, attn, gates, x2d, mods, wts, *, seq_len, mod_row, alpha, h2_tokens, h2_tok0, h2_buf=None):
    n, d = x2d.shape
    tq = min(TQ_MIX, seq_len)
    h2_blk0 = h2_tok0 // tq
    tps = seq_len // tq
    hb = tq // POOL_HALO
    n_hb = n // POOL_HALO
    row = lambda i: (i, 0)
    const2 = lambda i: (0, 0)
    const3 = lambda i: (0, 0, 0)
    mod_map = (lambda i: (i // tps, 0, 0)) if mod_row is None else (lambda i: (mod_row, 0, 0))
    u_prev = pl.BlockSpec((POOL_HALO, POOL_W), lambda i: (jnp.maximum(i * hb - 1, 0), 0))
    u_next = pl.BlockSpec((POOL_HALO, POOL_W), lambda i: (jnp.minimum((i + 1) * hb, n_hb - 1), 0))
    wgrp, pscale, wpool, wattn, wo, lng, lnb = wts
    operands = [u, u, u, attn, gates, x2d, mods, wgrp, pscale, wpool, wattn, wo, lng, lnb]
    alias_specs, aliases = [], {}
    if h2_buf is not None:
        alias_specs = [pl.BlockSpec(memory_space=pl.ANY)]
        aliases = {len(operands): 1}
        operands.append(h2_buf)
    return pl.pallas_call(
        functools.partial(_merge_kernel, seq_len=seq_len, alpha=alpha),
        grid=(n // tq,),
        input_output_aliases=aliases,
        in_specs=[
            u_prev, pl.BlockSpec((tq, POOL_W), row), u_next,
            pl.BlockSpec((tq, ATTN_W), row),
            pl.BlockSpec((tq, 2 * d), row),
            pl.BlockSpec((tq, d), row),
            pl.BlockSpec((1, 1, mods.shape[2]), mod_map),
            pl.BlockSpec(wgrp.shape, const3), pl.BlockSpec(pscale.shape, const2),
            pl.BlockSpec(wpool.shape, const2), pl.BlockSpec(wattn.shape, const2),
            pl.BlockSpec(wo.shape, const2),
            pl.BlockSpec(lng.shape, const2), pl.BlockSpec(lnb.shape, const2),
            *alias_specs,
        ],
        out_specs=[pl.BlockSpec((tq, d), row),
                   pl.BlockSpec((tq * ROW_CHUNKS, LANES), lambda i: (h2_blk0 + i, 0))],
        out_shape=[jax.ShapeDtypeStruct((n, d), F32),
                   jax.ShapeDtypeStruct((h2_tokens * ROW_CHUNKS, LANES), F32)],
        scratch_shapes=[pltpu.VMEM((tq + 2 * POOL_HALO, POOL_W), F32)],
        compiler_params=_cparams(("parallel",)),
        name="merge",
    )(*operands)


def _route_kernel(h_ref, wr_ref, bias_ref, eid_ref, wts_ref, rank_ref, cnt_ref, base_ref):
    tn = h_ref.shape[0] // ROW_CHUNKS
    ng, epg = N_EXPERT_GROUPS, EXPERTS_PER_GROUP

    @pl.when(pl.program_id(0) == 0)
    def _():
        base_ref[...] = jnp.zeros_like(base_ref)

    logits = lax.dot_general(wr_ref[...], _load_chunked(h_ref, tn).astype(BF16), (((1,), (1,)), ((), ())),
                             preferred_element_type=F32)
    scores = jax.nn.sigmoid(logits)
    biased = scores + bias_ref[...]
    bj = [biased[j * ng:(j + 1) * ng] for j in range(epg)]
    sj = [scores[j * ng:(j + 1) * ng] for j in range(epg)]
    hi01, lo01 = jnp.maximum(bj[0], bj[1]), jnp.minimum(bj[0], bj[1])
    hi23, lo23 = jnp.maximum(bj[2], bj[3]), jnp.minimum(bj[2], bj[3])
    gscore = jnp.maximum(hi01, hi23) + jnp.maximum(jnp.minimum(hi01, hi23), jnp.maximum(lo01, lo23))
    giota = lax.broadcasted_iota(jnp.int32, (ng, tn), 0)
    gmax = jnp.max(gscore, axis=0, keepdims=True)
    g_first = jnp.min(jnp.where(gscore == gmax, giota.astype(F32), float(ng)), axis=0, keepdims=True)
    g_sel = g_first.astype(jnp.int32)
    in_g = giota == g_sel
    vb = [jnp.sum(jnp.where(in_g, b, 0.0), axis=0, keepdims=True) for b in bj]
    vs = [jnp.sum(jnp.where(in_g, s, 0.0), axis=0, keepdims=True) for s in sj]

    def first_best(vals):
        best = functools.reduce(jnp.maximum, vals)
        idx = jnp.full(best.shape, epg - 1, jnp.int32)
        for j in range(epg - 2, -1, -1):
            idx = jnp.where(vals[j] == best, j, idx)
        return idx

    def pick(vals, idx):
        out = vals[epg - 1]
        for j in range(epg - 2, -1, -1):
            out = jnp.where(idx == j, vals[j], out)
        return out

    l1 = first_best(vb)
    l2 = first_best([jnp.where(l1 == j, -jnp.inf, vb[j]) for j in range(epg)])
    w1, w2 = pick(vs, l1), pick(vs, l2)
    wsum = w1 + w2
    eid_ref[0, 0:1, :] = g_sel * epg + l1
    eid_ref[0, 1:2, :] = g_sel * epg + l2
    wts_ref[0:1, :] = w1 / wsum
    wts_ref[1:2, :] = w2 / wsum

    r1, r2 = l1 * ng + g_sel, l2 * ng + g_sel
    riota = lax.broadcasted_iota(jnp.int32, (N_EXPERTS, tn), 0)
    hit1, hit2 = riota == r1, riota == r2
    onehot = jnp.where(hit1 | hit2, 1.0, 0.0)
    before = lax.broadcasted_iota(jnp.int32, (tn, tn), 0) < lax.broadcasted_iota(jnp.int32, (tn, tn), 1)
    prefix = _dot(onehot.astype(BF16), jnp.where(before, 1.0, 0.0).astype(BF16)) + base_ref[:, 0:1]
    rank_ref[0, 0:1, :] = jnp.sum(jnp.where(hit1, prefix, 0.0), axis=0, keepdims=True).astype(jnp.int32)
    rank_ref[0, 1:2, :] = jnp.sum(jnp.where(hit2, prefix, 0.0), axis=0, keepdims=True).astype(jnp.int32)
    base_ref[...] = base_ref[...] + jnp.sum(onehot, axis=1, keepdims=True)
    cnt_ref[...] = base_ref[...]


def _route(h2c, wr_t, bias_col):
    n = h2c.shape[0] // ROW_CHUNKS
    d = wr_t.shape[1]
    tn = TN_TOK
    steps = n // tn
    col = lambda i: (0, i)
    blk = lambda i: (i, 0, 0)
    return pl.pallas_call(
        _route_kernel,
        grid=(steps,),
        in_specs=[
            pl.BlockSpec((tn * ROW_CHUNKS, LANES), lambda i: (i, 0)),
            pl.BlockSpec((N_EXPERTS, d), lambda i: (0, 0)),
            pl.BlockSpec((N_EXPERTS, 1), lambda i: (0, 0)),
        ],
        out_specs=[pl.BlockSpec((1, TOP_K, tn), blk), pl.BlockSpec((TOP_K, tn), col),
                   pl.BlockSpec((1, TOP_K, tn), blk), pl.BlockSpec((N_EXPERTS, LANES), lambda i: (0, 0))],
        out_shape=[jax.ShapeDtypeStruct((steps, TOP_K, tn), jnp.int32), jax.ShapeDtypeStruct((TOP_K, n), F32),
                   jax.ShapeDtypeStruct((steps, TOP_K, tn), jnp.int32),
                   jax.ShapeDtypeStruct((N_EXPERTS, LANES), F32)],
        scratch_shapes=[pltpu.VMEM((N_EXPERTS, LANES), F32)],
        compiler_params=_cparams(("arbitrary",)),
        name="route",
    )(h2c, wr_t, bias_col)


def _dispatch_kernel(dest_ref, zlo_ref, tok_ref, xs_hbm, zero_ref, zsem, sem, *, t_ffn):
    i = pl.program_id(0)
    tn = dest_ref.shape[2]

    @pl.when(i == 0)
    def _():
        zero_ref[...] = jnp.zeros_like(zero_ref)

        def zcopy(e):
            return pltpu.make_async_copy(zero_ref, xs_hbm.at[pl.ds(jnp.maximum(zlo_ref[e], 0), t_ffn)], zsem)

        def start(e, carry):
            @pl.when(zlo_ref[e] >= 0)
            def _():
                zcopy(e).start()
            return carry

        def wait(e, carry):
            @pl.when(zlo_ref[e] >= 0)
            def _():
                zcopy(e).wait()
            return carry

        lax.fori_loop(0, N_EXPERTS, start, 0)
        lax.fori_loop(0, N_EXPERTS, wait, 0)

    for n in range(tn):
        src = tok_ref.at[pl.ds(n * ROW_CHUNKS, ROW_CHUNKS)]
        for slot in range(TOP_K):
            pltpu.make_async_copy(src, xs_hbm.at[dest_ref[0, slot, n]], sem).start(priority=slot)
    for slot in range(TOP_K):
        pltpu.make_async_copy(xs_hbm.at[pl.ds(0, tn)], xs_hbm.at[pl.ds(0, tn)], sem).wait()


def _dispatch(tokens, dest3, zlo, n_rows, t_ffn):
    steps, _, tn = dest3.shape
    smem = pltpu.SMEM
    return pl.pallas_call(
        functools.partial(_dispatch_kernel, t_ffn=t_ffn),
        grid=(steps,),
        in_specs=[
            pl.BlockSpec((1, TOP_K, tn), lambda i: (i, 0, 0), memory_space=smem),
            pl.BlockSpec(memory_space=smem),
            pl.BlockSpec((tn * ROW_CHUNKS, LANES), lambda i: (i, 0)),
        ],
        out_specs=pl.BlockSpec(memory_space=pl.ANY),
        out_shape=jax.ShapeDtypeStruct((n_rows, ROW_CHUNKS, LANES), tokens.dtype),
        scratch_shapes=[pltpu.VMEM((t_ffn, ROW_CHUNKS, LANES), tokens.dtype), pltpu.SemaphoreType.DMA,
                        pltpu.SemaphoreType.DMA],
        compiler_params=_cparams(("arbitrary",)),
        name="dispatch",
    )(dest3, zlo, tokens)


def _ffn_kernel(texp_ref, nused_ref, enext_ref, eord_ref, x_ref, wg_hbm, wu_hbm, wd_hbm, y_ref,
                wgf_ref, wuf_ref, wdf_ref, wgb_ref, wub_ref, wdb_ref, wsem, *, layer):
    j = pl.program_id(0)
    e = texp_ref[j]
    active = j < nused_ref[0]
    first = (j == 0) | (e != texp_ref[jnp.maximum(j - 1, 0)])
    slot = eord_ref[e] % 2
    pairs = ((wg_hbm, wgf_ref), (wu_hbm, wuf_ref), (wd_hbm, wdf_ref))

    def wcopies(expert, s):
        return [pltpu.make_async_copy(hbm.at[layer, expert], buf.at[s], wsem.at[s, i])
                for i, (hbm, buf) in enumerate(pairs)]

    @pl.when(active & (j == 0))
    def _():
        for cp in wcopies(e, slot):
            cp.start()

    @pl.when(active & first)
    def _():
        nxt = enext_ref[e]

        @pl.when(nxt >= 0)
        def _():
            for cp in wcopies(nxt, 1 - slot):
                cp.start()

        for cp in wcopies(e, slot):
            cp.wait()
        wgb_ref[...] = wgf_ref[slot].astype(BF16)
        wub_ref[...] = wuf_ref[slot].astype(BF16)
        wdb_ref[...] = wdf_ref[slot].astype(BF16)

    @pl.when(active)
    def _():
        x = _load_chunked(x_ref, x_ref.shape[0] // ROW_CHUNKS).astype(BF16)
        gate = _dot(x, wgb_ref[...])
        up = _dot(x, wub_ref[...])
        act = (gate * jax.nn.sigmoid(gate) * up).astype(BF16)
        _store_chunked(y_ref, _dot(act, wdb_ref[...]))


def _expert_ffn(xs, plan, w_gate, w_up, w_down, layer, t_ffn):
    tile_expert, n_used, e_next, e_ord = plan
    _, _, d, de = w_gate.shape
    n_rows = xs.shape[0] // ROW_CHUNKS
    n_tiles = n_rows // t_ffn
    rowmap = lambda j, te, nu, en, eo: (jnp.minimum(j, nu[0] - 1), 0)
    hbm = pl.BlockSpec(memory_space=pl.ANY)
    return pl.pallas_call(
        functools.partial(_ffn_kernel, layer=layer),
        grid_spec=pltpu.PrefetchScalarGridSpec(
            num_scalar_prefetch=4,
            grid=(n_tiles,),
            in_specs=[pl.BlockSpec((t_ffn * ROW_CHUNKS, LANES), rowmap), hbm, hbm, hbm],
            out_specs=pl.BlockSpec((t_ffn * ROW_CHUNKS, LANES), rowmap),
            scratch_shapes=[pltpu.VMEM((2, d, de), F32), pltpu.VMEM((2, d, de), F32), pltpu.VMEM((2, de, d), F32),
                            pltpu.VMEM((d, de), BF16), pltpu.VMEM((d, de), BF16), pltpu.VMEM((de, d), BF16),
                            pltpu.SemaphoreType.DMA((2, 3))],
        ),
        out_shape=jax.ShapeDtypeStruct((n_rows * ROW_CHUNKS, LANES), F32),
        compiler_params=_cparams(("arbitrary",)),
        name="expert_ffn",
    )(tile_expert, n_used, e_next, e_ord, xs, w_gate, w_up, w_down)


def _combine_kernel(dest_ref, dest_next_ref, y_hbm, wts_ref, x1_ref, mod_ref, lng_ref, lnb_ref, x2_ref,
                    buf_ref, sem, *, alpha):
    i = pl.program_id(0)
    n_steps = pl.num_programs(0)
    tc, d = x1_ref.shape

    def row_copy(idx_ref, parity, n, slot, rows):
        return pltpu.make_async_copy(y_hbm.at[idx_ref[0, slot, n]], buf_ref.at[parity * TOP_K + slot, rows],
                                     sem.at[parity])

    @pl.when(i == 0)
    def _():
        def body(n, carry):
            rows = pl.ds(pl.multiple_of(n * ROW_CHUNKS, ROW_CHUNKS), ROW_CHUNKS)
            for slot in range(TOP_K):
                row_copy(dest_ref, 0, n, slot, rows).start(priority=slot)
            return carry
        lax.fori_loop(0, tc, body, 0, unroll=DMA_UNROLL)

    for parity in range(2):
        @pl.when((i + 1 < n_steps) & ((i + 1) % 2 == parity))
        def _():
            for n in range(tc):
                for slot in range(TOP_K):
                    row_copy(dest_next_ref, parity, n, slot, pl.ds(n * ROW_CHUNKS, ROW_CHUNKS)).start(priority=slot)

    for slot in range(TOP_K):
        pltpu.make_async_copy(y_hbm.at[pl.ds(0, tc)], y_hbm.at[pl.ds(0, tc)], sem.at[i % 2]).wait()
    w = wts_ref[...]
    cur = (i % 2) * TOP_K
    f = w[:, 0:1] * _load_chunked(buf_ref.at[cur], tc) + w[:, 1:2] * _load_chunked(buf_ref.at[cur + 1], tc)
    g2 = mod_ref[0][:, 5 * d:6 * d]
    x2_ref[...] = _layer_norm(alpha * x1_ref[...] + g2 * f) * lng_ref[...] + lnb_ref[...]


def _combine(y, dest3, wts_t, x1, mods, lng, lnb, *, tok0, seq_len, mod_row, alpha):
    n, d = x1.shape
    tc = dest3.shape[2]
    tps = max(seq_len // tc, 1)
    steps = n // tc
    blk0 = tok0 // tc
    mod_map = (lambda i: (i // tps, 0, 0)) if mod_row is None else (lambda i: (mod_row, 0, 0))
    cur = lambda i: (blk0 + i, 0, 0)
    nxt = lambda i: (blk0 + jnp.minimum(i + 1, steps - 1), 0, 0)
    smem = pltpu.SMEM
    return pl.pallas_call(
        functools.partial(_combine_kernel, alpha=alpha),
        grid=(steps,),
        in_specs=[
            pl.BlockSpec((1, TOP_K, tc), cur, memory_space=smem),
            pl.BlockSpec((1, TOP_K, tc), nxt, memory_space=smem),
            pl.BlockSpec(memory_space=pl.ANY),
            pl.BlockSpec((tc, TOP_K), lambda i: (blk0 + i, 0)),
            pl.BlockSpec((tc, d), lambda i: (i, 0)),
            pl.BlockSpec((1, 1, mods.shape[2]), mod_map),
            pl.BlockSpec(lng.shape, lambda i: (0, 0)),
            pl.BlockSpec(lnb.shape, lambda i: (0, 0)),
        ],
        out_specs=pl.BlockSpec((tc, d), lambda i: (i, 0)),
        out_shape=jax.ShapeDtypeStruct((n, d), F32),
        scratch_shapes=[pltpu.VMEM((2 * TOP_K, tc * ROW_CHUNKS, LANES), F32), pltpu.SemaphoreType.DMA((2,))],
        compiler_params=_cparams(("arbitrary",)),
        name="combine",
    )(dest3, dest3, y, wts_t, x1, mods, lng, lnb)


def _rope_tables(seq_len):
    t = jnp.arange(seq_len)
    row = (t // GRID_W).astype(F32)
    col = (t % GRID_W).astype(F32)
    half = HEAD_DIM // 2
    inv_freq = ROPE_BASE ** (-jnp.arange(0, half, 2, dtype=F32) / half)
    ang_r, ang_c = row[:, None] * inv_freq, col[:, None] * inv_freq
    cr, sr, cc, sc = jnp.cos(ang_r), jnp.sin(ang_r), jnp.cos(ang_c), jnp.sin(ang_c)
    cos = jnp.concatenate([cr, cr, cc, cc], axis=1)
    sin = jnp.concatenate([-sr, sr, -sc, sc], axis=1)
    return jnp.tile(cos, (1, LANES // HEAD_DIM)), jnp.tile(sin, (1, LANES // HEAD_DIM))


def _pair_heads(a, axis):
    shp = a.shape
    a = a.reshape(*shp[:axis], N_KV_HEADS, Q_REP, HEAD_DIM, *shp[axis + 1:])
    a = jnp.swapaxes(a, axis, axis + 1)
    return a.reshape(shp)


def _moe_plan(eid3, rank3, counts, n_tok, t_ffn):
    counts = counts.reshape(EXPERTS_PER_GROUP, N_EXPERT_GROUPS).T.reshape(N_EXPERTS).astype(jnp.int32)
    padded = (counts + t_ffn - 1) // t_ffn * t_ffn
    pends = jnp.cumsum(padded)
    pstarts = (pends - padded).astype(jnp.int32)
    experts = jnp.arange(N_EXPERTS, dtype=jnp.int32)
    dest3 = rank3 + jnp.sum(jnp.where(eid3[..., None] == experts, pstarts, 0), axis=-1)
    n_rows = -(-(n_tok * TOP_K) // t_ffn) * t_ffn + N_EXPERTS * t_ffn
    n_tiles = n_rows // t_ffn
    tile_row0 = jnp.arange(n_tiles, dtype=jnp.int32) * t_ffn
    tile_expert = jnp.minimum(jnp.sum(pends[None, :] <= tile_row0[:, None], axis=1), N_EXPERTS - 1)
    n_used = (pends[-1] // t_ffn).astype(jnp.int32).reshape(1)
    nonempty = counts > 0
    later = (experts[None, :] > experts[:, None]) & nonempty[None, :]
    e_next = jnp.min(jnp.where(later, experts[None, :], N_EXPERTS), axis=1)
    e_next = jnp.where(e_next < N_EXPERTS, e_next, -1).astype(jnp.int32)
    e_ord = (jnp.cumsum(nonempty.astype(jnp.int32)) - 1).astype(jnp.int32)
    zlo = jnp.where(nonempty, pends - t_ffn, -1).astype(jnp.int32)
    plan = (tile_expert.astype(jnp.int32), n_used, e_next, e_ord)
    return dest3.astype(jnp.int32), plan, zlo, n_rows


def kernel(x, c, ctx, c_ctx, w_ada, b_ada, w_in, w_pool_grp, pool_scale, w_pool_br, w_attn_br, attn_sink,
           w_o, ln1_g, ln1_b, w_router, router_bias, w_exp_gate, w_exp_up, w_exp_down, ln2_g, ln2_b):
    bsz, seq, d = x.shape
    assert d == ROW_CHUNKS * LANES
    c_len = ctx.shape[1]
    depth = w_in.shape[0]
    n_lat, n_ctx = bsz * seq, bsz * c_len
    alpha = (2 * depth) ** 0.25

    cond = jnp.zeros((8, d), F32).at[:bsz].set(c).at[bsz].set(c_ctx)
    ada = _ada_terms(cond, w_ada, b_ada)
    cos, sin = _rope_tables(seq)
    wr_t = w_router.reshape(d, N_EXPERT_GROUPS, EXPERTS_PER_GROUP).transpose(2, 1, 0).reshape(N_EXPERTS, d)
    wr_t = wr_t.astype(BF16)
    bias_col = router_bias.reshape(N_EXPERT_GROUPS, EXPERTS_PER_GROUP).T.reshape(N_EXPERTS, 1).astype(F32)

    xl = x.reshape(n_lat, d)
    xc = ctx.reshape(n_ctx, d)
    for l in range(depth):
        ctx_out = l < depth - 1
        mods = ada[l].reshape(8, 1, 6 * d)
        w_l = w_in[l]
        w_inp = jnp.concatenate([w_l[:, :COL_Q], _pair_heads(w_l[:, COL_Q:COL_K], 1), w_l[:, COL_K:]],
                                axis=1).astype(BF16)
        sink_col = jnp.broadcast_to(attn_sink[l].reshape(N_KV_HEADS, Q_REP, 1, 1),
                                    (N_KV_HEADS, Q_REP, BLOCK, 1)).reshape(N_KV_HEADS, Q_REP * BLOCK, 1)
        sink_col = (sink_col * LOG2E).astype(F32)
        mix_w = (w_pool_grp[l].astype(BF16), pool_scale[l].reshape(1, POOL_W), w_pool_br[l].astype(BF16),
                 _pair_heads(w_attn_br[l], 0).astype(BF16), w_o[l].astype(BF16),
                 ln1_g[l].reshape(1, d), ln1_b[l].reshape(1, d))
        lng2, lnb2 = ln2_g[l].reshape(1, d), ln2_b[l].reshape(1, d)

        if ctx_out:
            uc, qc, kc, vc, gc = _inproj(xc, mods, w_inp, cos, sin, seq_len=c_len, mod_row=bsz, rope=False)
        else:
            kc, vc = _inproj(xc, mods, w_inp, cos, sin, seq_len=c_len, mod_row=bsz, rope=False, kv_only=True)
        kc3, vc3 = kc.reshape(bsz, c_len, KV_W), vc.reshape(bsz, c_len, KV_W)
        u, q, k, v, g = _inproj(xl, mods, w_inp, cos, sin, seq_len=seq, mod_row=None, rope=True)
        n_tok = n_lat + n_ctx if ctx_out else n_lat
        tokens = None
        if ctx_out:
            attn_c = _attention(qc, kc, vc, kc3, vc3, sink_col, seq_len=c_len, local=False)
            xc1, tokens = _merge(uc, attn_c, gc, xc, mods, mix_w, seq_len=c_len, mod_row=bsz, alpha=alpha,
                                 h2_tokens=n_tok, h2_tok0=n_lat)
        attn = _attention(q, k, v, kc3, vc3, sink_col, seq_len=seq, local=True)
        x1, tokens = _merge(u, attn, g, xl, mods, mix_w, seq_len=seq, mod_row=None, alpha=alpha,
                            h2_tokens=n_tok, h2_tok0=0, h2_buf=tokens)

        eid3, wts, rank3, counts = _route(tokens, wr_t, bias_col)
        dest3, plan, zlo, n_rows = _moe_plan(eid3, rank3, counts[:, 0], n_tok, T_FFN)
        xs = _dispatch(tokens, dest3, zlo, n_rows, T_FFN)
        y = _expert_ffn(xs.reshape(n_rows * ROW_CHUNKS, LANES), plan, w_exp_gate, w_exp_up, w_exp_down, l, T_FFN)
        y3 = y.reshape(n_rows, ROW_CHUNKS, LANES)
        wts_t = wts.T
        xl = _combine(y3, dest3, wts_t, x1, mods, lng2, lnb2, tok0=0, seq_len=seq, mod_row=None, alpha=alpha)
        if ctx_out:
            xc = _combine(y3, dest3, wts_t, xc1, mods, lng2, lnb2, tok0=n_lat, seq_len=c_len, mod_row=bsz,
                          alpha=alpha)
    return xl.reshape(bsz, seq, d)
```

```python
import functools

import jax
import jax.numpy as jnp
from jax import lax
from jax.experimental import pallas as pl
from jax.experimental.pallas import tpu as pltpu

F32 = jnp.float32
BF16 = jnp.bfloat16

GRID_W = 64
POOL_WINDOWS = (2, 4, 8, 16)
POOL_GROUP_W = 128
POOL_W = 512
HEAD_DIM = 64
N_HEADS = 8
N_KV_HEADS = 2
Q_REP = N_HEADS // N_KV_HEADS
ATTN_W = N_HEADS * HEAD_DIM
KV_W = N_KV_HEADS * HEAD_DIM
BLOCK = 128
ROPE_BASE = 10000.0
COL_POOL = 0
COL_Q = COL_POOL + POOL_W
COL_K = COL_Q + ATTN_W
COL_V = COL_K + KV_W
COL_GATE = COL_V + KV_W
N_EXPERTS = 32
N_EXPERT_GROUPS = 8
EXPERTS_PER_GROUP = N_EXPERTS // N_EXPERT_GROUPS
TOP_K = 2
LN_EPS = 1e-6
LOG2E = 1.4426950408889634

LANES = 128
POOL_HALO = 8
ROW_CHUNKS = 8
VMEM_LIMIT = 56 * 1024 * 1024

TM_IN = 512
TQ_ATT = 2048
TQ_MIX = 512
TN_TOK = 512
T_FFN = 256
DMA_UNROLL = 8


def _cparams(sem, flags=None):
    return pltpu.CompilerParams(dimension_semantics=sem, vmem_limit_bytes=VMEM_LIMIT, flags=flags)


def _layer_norm(x):
    mu = jnp.mean(x, axis=-1, keepdims=True)
    xc = x - mu
    var = jnp.mean(xc * xc, axis=-1, keepdims=True)
    return xc * lax.rsqrt(var + LN_EPS)


def _dot(a, b):
    return jnp.dot(a, b, preferred_element_type=F32)


def _store_chunked(ref, val):
    t = val.shape[0]
    for s in range(ROW_CHUNKS):
        ref[pl.ds(s, t, stride=ROW_CHUNKS), :] = val[:, s * LANES:(s + 1) * LANES]


def _load_chunked(ref, t):
    return jnp.concatenate([ref[pl.ds(s, t, stride=ROW_CHUNKS), :] for s in range(ROW_CHUNKS)], axis=1)


def _ada_kernel(cond_ref, w_ref, b_ref, o_ref):
    s = cond_ref[...]
    s = s * jax.nn.sigmoid(s)
    o_ref[0] = _dot(s.astype(BF16), w_ref[0].astype(BF16)) + b_ref[0]


def _ada_terms(cond, w_ada, b_ada):
    depth, d, n6 = w_ada.shape
    tn = n6 // 4
    return pl.pallas_call(
        _ada_kernel,
        grid=(depth, n6 // tn),
        in_specs=[
            pl.BlockSpec((8, d), lambda l, j: (0, 0)),
            pl.BlockSpec((1, d, tn), lambda l, j: (l, 0, j)),
            pl.BlockSpec((1, 1, tn), lambda l, j: (l, 0, j)),
        ],
        out_specs=pl.BlockSpec((1, 8, tn), lambda l, j: (l, 0, j)),
        out_shape=jax.ShapeDtypeStruct((depth, 8, n6), F32),
        compiler_params=_cparams(("arbitrary", "arbitrary")),
        name="ada_terms",
    )(cond, w_ada, b_ada.reshape(depth, 1, n6))


def _rope(t, cos, sin):
    lane = lax.broadcasted_iota(jnp.int32, (1, LANES), 1)
    first = (lane % 32) < 16
    outs = []
    for j in range(t.shape[1] // LANES):
        tj = t[:, j * LANES:(j + 1) * LANES]
        partner = jnp.where(first, pltpu.roll(tj, LANES - 16, 1), pltpu.roll(tj, 16, 1))
        outs.append(tj * cos + partner * sin)
    return outs[0] if len(outs) == 1 else jnp.concatenate(outs, axis=1)


def _inproj_kernel(x_ref, mod_ref, w_ref, cos_ref, sin_ref, *out_refs, rope, kv_only):
    d = x_ref.shape[1]
    mod = mod_ref[0]
    shift, scale = mod[:, 0:d], mod[:, d:2 * d]
    h = (_layer_norm(x_ref[...]) * (1.0 + scale) + shift).astype(BF16)

    def proj(lo, hi):
        return _dot(h, w_ref[:, lo:hi])

    if kv_only:
        k_ref, v_ref = out_refs
    else:
        u_ref, q_ref, k_ref, v_ref, g_ref = out_refs
        u_ref[...] = proj(COL_POOL, COL_Q)
        q = proj(COL_Q, COL_K)
        if rope:
            q = _rope(q, cos_ref[...], sin_ref[...])
        q_ref[...] = (q * (LOG2E * HEAD_DIM ** -0.5)).astype(BF16)
        g_ref[...] = jax.nn.sigmoid(proj(COL_GATE, w_ref.shape[1]))
    kv = proj(COL_K, COL_GATE)
    k = kv[:, 0:KV_W]
    if rope:
        k = _rope(k, cos_ref[...], sin_ref[...])
    k_ref[...] = k.astype(BF16)
    v_ref[...] = kv[:, KV_W:2 * KV_W].astype(BF16)


def _inproj(x2d, mods, w_in, cos, sin, *, seq_len, mod_row, rope, kv_only=False):
    n, d = x2d.shape
    tm = min(TM_IN, seq_len)
    tps = seq_len // tm
    n_cols = w_in.shape[1]
    mod_map = (lambda i: (i // tps, 0, 0)) if mod_row is None else (lambda i: (mod_row, 0, 0))
    tab_map = (lambda i: (i % tps, 0)) if rope else (lambda i: (0, 0))
    row = lambda i: (i, 0)
    kv_shapes = [jax.ShapeDtypeStruct((n, KV_W), BF16)] * 2
    kv_specs = [pl.BlockSpec((tm, KV_W), row)] * 2
    if kv_only:
        out_shape, out_specs = kv_shapes, kv_specs
    else:
        out_shape = [jax.ShapeDtypeStruct((n, POOL_W), F32), jax.ShapeDtypeStruct((n, ATTN_W), BF16),
                     *kv_shapes, jax.ShapeDtypeStruct((n, n_cols - COL_GATE), F32)]
        out_specs = [pl.BlockSpec((tm, POOL_W), row), pl.BlockSpec((tm, ATTN_W), row),
                     *kv_specs, pl.BlockSpec((tm, n_cols - COL_GATE), row)]
    return pl.pallas_call(
        functools.partial(_inproj_kernel, rope=rope, kv_only=kv_only),
        grid=(n // tm,),
        in_specs=[
            pl.BlockSpec((tm, d), row),
            pl.BlockSpec((1, 1, mods.shape[2]), mod_map),
            pl.BlockSpec((d, n_cols), lambda i: (0, 0)),
            pl.BlockSpec((tm, LANES), tab_map),
            pl.BlockSpec((tm, LANES), tab_map),
        ],
        out_specs=out_specs,
        out_shape=out_shape,
        compiler_params=_cparams(("parallel",)),
        name="inproj_kv" if kv_only else "inproj",
    )(x2d, mods, w_in, cos, sin)


def _merge_kernel(u_prev_ref, u_ref, u_next_ref, attn_ref, g_ref, x_ref, mod_ref,
                  wgrp_ref, pscale_ref, wpool_ref, wattn_ref, wo_ref, lng_ref, lnb_ref, wr_ref,
                  *rest, seq_len, alpha):
    x1_ref, h2_ref, logit_ref, uext_ref = rest[-4:]
    tq, d = x_ref.shape
    tps = seq_len // tq
    t_in_seq = pl.program_id(0) % tps
    is_first = t_in_seq == 0
    is_last = t_in_seq == tps - 1

    h8 = POOL_HALO
    uext_ref[0:h8, :] = jnp.where(is_first, 0.0, u_prev_ref[...])
    uext_ref[h8:h8 + tq, :] = u_ref[...]
    uext_ref[h8 + tq:, :] = jnp.where(is_last, 0.0, u_next_ref[...])
    pos = t_in_seq * tq + lax.broadcasted_iota(jnp.int32, (tq, 1), 0)
    pooled = []
    for gi, w in enumerate(POOL_WINDOWS):
        cols = slice(gi * POOL_GROUP_W, (gi + 1) * POOL_GROUP_W)
        acc = uext_ref[h8 - w // 2:h8 - w // 2 + tq, cols]
        for off in range(-w // 2 + 1, w // 2):
            acc = acc + uext_ref[h8 + off:h8 + off + tq, cols]
        lo = jnp.maximum(pos - w // 2, 0)
        hi = jnp.minimum(pos - w // 2 + w - 1, seq_len - 1)
        mean = acc / (hi - lo + 1).astype(F32)
        pg = (mean - u_ref[:, cols]).astype(BF16)
        pooled.append(_dot(pg, wgrp_ref[gi]))
    pool_lat = jnp.concatenate(pooled, axis=1) * pscale_ref[...]
    pool_proj = _dot(pool_lat.astype(BF16), wpool_ref[...])
    attn_proj = _dot(attn_ref[...], wattn_ref[...])

    gates = g_ref[...]
    merged = gates[:, 0:d] * pool_proj + gates[:, d:2 * d] * attn_proj
    y = _dot(merged.astype(BF16), wo_ref[...])
    mod = mod_ref[0]
    g1 = mod[:, 2 * d:3 * d]
    sh2, sc2 = mod[:, 3 * d:4 * d], mod[:, 4 * d:5 * d]
    x1 = _layer_norm(alpha * x_ref[...] + g1 * y) * lng_ref[...] + lnb_ref[...]
    x1_ref[...] = x1
    h2 = _layer_norm(x1) * (1.0 + sc2) + sh2
    _store_chunked(h2_ref, h2)
    logit_ref[...] = lax.dot_general(wr_ref[...], h2.astype(BF16), (((1,), (1,)), ((), ())),
                                     preferred_element_type=F32)


def _attn_kernel(q_ref, k_prev_ref, k_ref, k_next_ref, v_prev_ref, v_ref, v_next_ref, kc_ref, vc_ref,
                 sink_ref, attn_ref, s0_ref, s1_ref, kg_ref, vg_ref, *, seq_len, local):
    tq = q_ref.shape[0]
    nb = tq // BLOCK
    tps = seq_len // tq
    t_in_seq = pl.program_id(0) % tps
    is_first = t_in_seq == 0
    is_last = t_in_seq == tps - 1
    lane = lax.broadcasted_iota(jnp.int32, (1, LANES), 1)
    lo_half = lane < HEAD_DIM
    one = jnp.ones((), BF16)
    kc = kc_ref[0]
    vc = vc_ref[0]
    kc_g = [jnp.where(lo_half, kc, 0), jnp.where(lo_half, 0, kc)]
    vc_g = [jnp.where(lo_half, vc, one), jnp.where(lo_half, one, vc)]
    neg = jnp.float32(-jnp.inf)
    if local:
        k_ext = jnp.concatenate([k_prev_ref[...], k_ref[...], k_next_ref[...]], axis=0)
        v_ext = jnp.concatenate([v_prev_ref[...], v_ref[...], v_next_ref[...]], axis=0)
        kg_ref[0] = jnp.where(lo_half, k_ext, 0)
        kg_ref[1] = jnp.where(lo_half, 0, k_ext)
        vg_ref[0] = jnp.where(lo_half, v_ext, one)
        vg_ref[1] = jnp.where(lo_half, one, v_ext)
        qq = lax.broadcasted_iota(jnp.int32, (Q_REP * BLOCK, BLOCK), 0) % BLOCK
        kk = lax.broadcasted_iota(jnp.int32, (Q_REP * BLOCK, BLOCK), 1)
        mask_prev = jnp.where(kk >= qq, 0.0, neg)
        mask_next = jnp.where(kk <= qq, 0.0, neg)

    def block_rows(b):
        return pl.ds(b * BLOCK, BLOCK) if isinstance(b, int) else pl.ds(pl.multiple_of(b * BLOCK, BLOCK), BLOCK)

    def band_keys(b):
        start = b * BLOCK if isinstance(b, int) else pl.multiple_of(b * BLOCK, BLOCK)
        return pl.ds(start, 3 * BLOCK)

    def stage_a(b, s_buf, first_block, last_block):
        q_st = jnp.concatenate([q_ref[block_rows(b), c * LANES:(c + 1) * LANES] for c in range(Q_REP)], axis=0)
        for g in range(N_KV_HEADS):
            k_all = jnp.concatenate([kg_ref[g, band_keys(b), :], kc_g[g]], axis=0) if local else kc_g[g]
            s = lax.dot_general(q_st, k_all, (((1,), (1,)), ((), ())), preferred_element_type=F32)
            if local:
                m_prev = jnp.where(first_block, neg, mask_prev)
                m_next = jnp.where(last_block, neg, mask_next)
                s = jnp.concatenate([s[:, 0:BLOCK] + m_prev, s[:, BLOCK:2 * BLOCK],
                                     s[:, 2 * BLOCK:3 * BLOCK] + m_next, s[:, 3 * BLOCK:]], axis=1)
            s_buf[g] = s

    def stage_b(b, s_buf):
        pv, sink_w = [], []
        for g in range(N_KV_HEADS):
            v_all = jnp.concatenate([vg_ref[g, band_keys(b), :], vc_g[g]], axis=0) if local else vc_g[g]
            s = s_buf[g]
            sk = sink_ref[g]
            m = jnp.maximum(jnp.max(s, axis=-1, keepdims=True), sk)
            p = jnp.exp2((s - m).astype(BF16))
            pv.append(_dot(p, v_all))
            sink_w.append(jnp.exp2(sk - m))
        num = jnp.where(lo_half, pv[0], pv[1])
        den = pltpu.roll(jnp.where(lo_half, pv[1], pv[0]), HEAD_DIM, 1) + jnp.where(lo_half, sink_w[0], sink_w[1])
        o = num / den
        for c in range(Q_REP):
            attn_ref[block_rows(b), c * LANES:(c + 1) * LANES] = o[c * BLOCK:(c + 1) * BLOCK].astype(BF16)

    stage_a(0, s0_ref, is_first, False)

    def body(j, carry):
        stage_a(2 * j + 1, s1_ref, False, False)
        stage_b(2 * j, s0_ref)
        stage_a(2 * j + 2, s0_ref, False, False)
        stage_b(2 * j + 1, s1_ref)
        return carry

    lax.fori_loop(0, nb // 2 - 1, body, 0)
    stage_a(nb - 1, s1_ref, False, is_last)
    stage_b(nb - 2, s0_ref)
    stage_b(nb - 1, s1_ref)


def _attention(q, k, v, kc, vc, sink_col, *, seq_len, local):
    n = q.shape[0]
    tq = min(TQ_ATT, seq_len)
    tps = seq_len // tq
    kb = tq // BLOCK
    n_kb = n // BLOCK
    c_len = kc.shape[1]
    n_keys = (3 * BLOCK if local else 0) + c_len
    row = lambda i: (i, 0)
    ctx_map = lambda i: (i // tps, 0, 0)
    kv_prev = pl.BlockSpec((BLOCK, KV_W), lambda i: (jnp.maximum(i * kb - 1, 0), 0))
    kv_cur = pl.BlockSpec((tq, KV_W), row)
    kv_next = pl.BlockSpec((BLOCK, KV_W), lambda i: (jnp.minimum((i + 1) * kb, n_kb - 1), 0))
    return pl.pallas_call(
        functools.partial(_attn_kernel, seq_len=seq_len, local=local),
        grid=(n // tq,),
        in_specs=[
            pl.BlockSpec((tq, ATTN_W), row),
            kv_prev, kv_cur, kv_next, kv_prev, kv_cur, kv_next,
            pl.BlockSpec((1, c_len, KV_W), ctx_map), pl.BlockSpec((1, c_len, KV_W), ctx_map),
            pl.BlockSpec(sink_col.shape, lambda i: (0, 0, 0)),
        ],
        out_specs=pl.BlockSpec((tq, ATTN_W), row),
        out_shape=jax.ShapeDtypeStruct((n, ATTN_W), BF16),
        scratch_shapes=[pltpu.VMEM((N_KV_HEADS, Q_REP * BLOCK, n_keys), F32),
                        pltpu.VMEM((N_KV_HEADS, Q_REP * BLOCK, n_keys), F32),
                        pltpu.VMEM((N_KV_HEADS, tq + 2 * BLOCK, KV_W), BF16),
                        pltpu.VMEM((N_KV_HEADS, tq + 2 * BLOCK, KV_W), BF16)],
        compiler_params=_cparams(("parallel",)),
        name="attention" if local else "attention_ctx",
    )(q, k, k, k, v, v, v, kc, vc, sink_col)


def _merge(u, attn, gates, x2d, mods, wts, *, seq_len, mod_row, alpha, h2_tokens, h2_tok0, h2_buf=None):
    n, d = x2d.shape
    tq = min(TQ_MIX, seq_len)
    h2_blk0 = h2_tok0 // tq
    tps = seq_len // tq
    hb = tq // POOL_HALO
    n_hb = n // POOL_HALO
    row = lambda i: (i, 0)
    const2 = lambda i: (0, 0)
    const3 = lambda i: (0, 0, 0)
    mod_map = (lambda i: (i // tps, 0, 0)) if mod_row is None else (lambda i: (mod_row, 0, 0))
    u_prev = pl.BlockSpec((POOL_HALO, POOL_W), lambda i: (jnp.maximum(i * hb - 1, 0), 0))
    u_next = pl.BlockSpec((POOL_HALO, POOL_W), lambda i: (jnp.minimum((i + 1) * hb, n_hb - 1), 0))
    wgrp, pscale, wpool, wattn, wo, lng, lnb, wr_t = wts
    operands = [u, u, u, attn, gates, x2d, mods, wgrp, pscale, wpool, wattn, wo, lng, lnb, wr_t]
    alias_specs, aliases = [], {}
    if h2_buf is not None:
        alias_specs = [pl.BlockSpec(memory_space=pl.ANY)]
        aliases = {len(operands): 1}
        operands.append(h2_buf)
    return pl.pallas_call(
        functools.partial(_merge_kernel, seq_len=seq_len, alpha=alpha),
        grid=(n // tq,),
        input_output_aliases=aliases,
        in_specs=[
            u_prev, pl.BlockSpec((tq, POOL_W), row), u_next,
            pl.BlockSpec((tq, ATTN_W), row),
            pl.BlockSpec((tq, 2 * d), row),
            pl.BlockSpec((tq, d), row),
            pl.BlockSpec((1, 1, mods.shape[2]), mod_map),
            pl.BlockSpec(wgrp.shape, const3), pl.BlockSpec(pscale.shape, const2),
            pl.BlockSpec(wpool.shape, const2), pl.BlockSpec(wattn.shape, const2),
            pl.BlockSpec(wo.shape, const2),
            pl.BlockSpec(lng.shape, const2), pl.BlockSpec(lnb.shape, const2),
            pl.BlockSpec(wr_t.shape, const2),
            *alias_specs,
        ],
        out_specs=[pl.BlockSpec((tq, d), row),
                   pl.BlockSpec((tq * ROW_CHUNKS, LANES), lambda i: (h2_blk0 + i, 0)),
                   pl.BlockSpec((N_EXPERTS, tq), lambda i: (0, i))],
        out_shape=[jax.ShapeDtypeStruct((n, d), F32),
                   jax.ShapeDtypeStruct((h2_tokens * ROW_CHUNKS, LANES), F32),
                   jax.ShapeDtypeStruct((N_EXPERTS, n), F32)],
        scratch_shapes=[pltpu.VMEM((tq + 2 * POOL_HALO, POOL_W), F32)],
        compiler_params=_cparams(("parallel",)),
        name="merge",
    )(*operands)


def _route_kernel(logit_ref, bias_ref, eid_ref, wts_ref, rank_ref, cnt_ref, base_ref):
    tn = logit_ref.shape[1]
    ng, epg = N_EXPERT_GROUPS, EXPERTS_PER_GROUP

    @pl.when(pl.program_id(0) == 0)
    def _():
        base_ref[...] = jnp.zeros_like(base_ref)

    scores = jax.nn.sigmoid(logit_ref[...])
    biased = scores + bias_ref[...]
    bj = [biased[j * ng:(j + 1) * ng] for j in range(epg)]
    sj = [scores[j * ng:(j + 1) * ng] for j in range(epg)]
    hi01, lo01 = jnp.maximum(bj[0], bj[1]), jnp.minimum(bj[0], bj[1])
    hi23, lo23 = jnp.maximum(bj[2], bj[3]), jnp.minimum(bj[2], bj[3])
    gscore = jnp.maximum(hi01, hi23) + jnp.maximum(jnp.minimum(hi01, hi23), jnp.maximum(lo01, lo23))
    giota = lax.broadcasted_iota(jnp.int32, (ng, tn), 0)
    gmax = jnp.max(gscore, axis=0, keepdims=True)
    g_first = jnp.min(jnp.where(gscore == gmax, giota.astype(F32), float(ng)), axis=0, keepdims=True)
    g_sel = g_first.astype(jnp.int32)
    in_g = giota == g_sel
    vb = [jnp.sum(jnp.where(in_g, b, 0.0), axis=0, keepdims=True) for b in bj]
    vs = [jnp.sum(jnp.where(in_g, s, 0.0), axis=0, keepdims=True) for s in sj]

    def first_best(vals):
        best = functools.reduce(jnp.maximum, vals)
        idx = jnp.full(best.shape, epg - 1, jnp.int32)
        for j in range(epg - 2, -1, -1):
            idx = jnp.where(vals[j] == best, j, idx)
        return idx

    def pick(vals, idx):
        out = vals[epg - 1]
        for j in range(epg - 2, -1, -1):
            out = jnp.where(idx == j, vals[j], out)
        return out

    l1 = first_best(vb)
    l2 = first_best([jnp.where(l1 == j, -jnp.inf, vb[j]) for j in range(epg)])
    w1, w2 = pick(vs, l1), pick(vs, l2)
    wsum = w1 + w2
    eid_ref[0, 0:1, :] = g_sel * epg + l1
    eid_ref[0, 1:2, :] = g_sel * epg + l2
    wts_ref[0:1, :] = w1 / wsum
    wts_ref[1:2, :] = w2 / wsum

    r1, r2 = l1 * ng + g_sel, l2 * ng + g_sel
    riota = lax.broadcasted_iota(jnp.int32, (N_EXPERTS, tn), 0)
    hit1, hit2 = riota == r1, riota == r2
    onehot = jnp.where(hit1 | hit2, 1.0, 0.0)
    before = lax.broadcasted_iota(jnp.int32, (tn, tn), 0) < lax.broadcasted_iota(jnp.int32, (tn, tn), 1)
    prefix = _dot(onehot.astype(BF16), jnp.where(before, 1.0, 0.0).astype(BF16)) + base_ref[:, 0:1]
    rank_ref[0, 0:1, :] = jnp.sum(jnp.where(hit1, prefix, 0.0), axis=0, keepdims=True).astype(jnp.int32)
    rank_ref[0, 1:2, :] = jnp.sum(jnp.where(hit2, prefix, 0.0), axis=0, keepdims=True).astype(jnp.int32)
    base_ref[...] = base_ref[...] + jnp.sum(onehot, axis=1, keepdims=True)
    cnt_ref[...] = base_ref[...]


def _route(logits, bias_col):
    n = logits.shape[1]
    tn = TN_TOK
    steps = n // tn
    col = lambda i: (0, i)
    blk = lambda i: (i, 0, 0)
    return pl.pallas_call(
        _route_kernel,
        grid=(steps,),
        in_specs=[
            pl.BlockSpec((N_EXPERTS, tn), col),
            pl.BlockSpec((N_EXPERTS, 1), lambda i: (0, 0)),
        ],
        out_specs=[pl.BlockSpec((1, TOP_K, tn), blk), pl.BlockSpec((TOP_K, tn), col),
                   pl.BlockSpec((1, TOP_K, tn), blk), pl.BlockSpec((N_EXPERTS, LANES), lambda i: (0, 0))],
        out_shape=[jax.ShapeDtypeStruct((steps, TOP_K, tn), jnp.int32), jax.ShapeDtypeStruct((TOP_K, n), F32),
                   jax.ShapeDtypeStruct((steps, TOP_K, tn), jnp.int32),
                   jax.ShapeDtypeStruct((N_EXPERTS, LANES), F32)],
        scratch_shapes=[pltpu.VMEM((N_EXPERTS, LANES), F32)],
        compiler_params=_cparams(("arbitrary",)),
        name="route",
    )(logits, bias_col)


def _dispatch_kernel(dest_ref, zlo_ref, tok_ref, xs_hbm, zero_ref, zsem, sem, *, t_ffn):
    i = pl.program_id(0)
    tn = dest_ref.shape[2]

    @pl.when(i == 0)
    def _():
        zero_ref[...] = jnp.zeros_like(zero_ref)

        def zcopy(e):
            return pltpu.make_async_copy(zero_ref, xs_hbm.at[pl.ds(jnp.maximum(zlo_ref[e], 0), t_ffn)], zsem)

        def start(e, carry):
            @pl.when(zlo_ref[e] >= 0)
            def _():
                zcopy(e).start()
            return carry

        def wait(e, carry):
            @pl.when(zlo_ref[e] >= 0)
            def _():
                zcopy(e).wait()
            return carry

        lax.fori_loop(0, N_EXPERTS, start, 0)
        lax.fori_loop(0, N_EXPERTS, wait, 0)

    for n in range(tn):
        src = tok_ref.at[pl.ds(n * ROW_CHUNKS, ROW_CHUNKS)]
        for slot in range(TOP_K):
            pltpu.make_async_copy(src, xs_hbm.at[dest_ref[0, slot, n]], sem).start(priority=slot)
    for slot in range(TOP_K):
        pltpu.make_async_copy(xs_hbm.at[pl.ds(0, tn)], xs_hbm.at[pl.ds(0, tn)], sem).wait()


def _dispatch(tokens, dest3, zlo, n_rows, t_ffn):
    steps, _, tn = dest3.shape
    smem = pltpu.SMEM
    return pl.pallas_call(
        functools.partial(_dispatch_kernel, t_ffn=t_ffn),
        grid=(steps,),
        in_specs=[
            pl.BlockSpec((1, TOP_K, tn), lambda i: (i, 0, 0), memory_space=smem),
            pl.BlockSpec(memory_space=smem),
            pl.BlockSpec((tn * ROW_CHUNKS, LANES), lambda i: (i, 0)),
        ],
        out_specs=pl.BlockSpec(memory_space=pl.ANY),
        out_shape=jax.ShapeDtypeStruct((n_rows, ROW_CHUNKS, LANES), tokens.dtype),
        scratch_shapes=[pltpu.VMEM((t_ffn, ROW_CHUNKS, LANES), tokens.dtype), pltpu.SemaphoreType.DMA,
                        pltpu.SemaphoreType.DMA],
        compiler_params=_cparams(("arbitrary",)),
        name="dispatch",
    )(dest3, zlo, tokens)


def _ffn_kernel(texp_ref, nused_ref, enext_ref, eord_ref, x_ref, wg_hbm, wu_hbm, wd_hbm, y_ref,
                wgf_ref, wuf_ref, wdf_ref, wgb_ref, wub_ref, wdb_ref, wsem, *, layer):
    j = pl.program_id(0)
    e = texp_ref[j]
    active = j < nused_ref[0]
    first = (j == 0) | (e != texp_ref[jnp.maximum(j - 1, 0)])
    slot = eord_ref[e] % 2
    pairs = ((wg_hbm, wgf_ref), (wu_hbm, wuf_ref), (wd_hbm, wdf_ref))

    def wcopies(expert, s):
        return [pltpu.make_async_copy(hbm.at[layer, expert], buf.at[s], wsem.at[s, i])
                for i, (hbm, buf) in enumerate(pairs)]

    @pl.when(active & (j == 0))
    def _():
        for cp in wcopies(e, slot):
            cp.start()

    @pl.when(active & first)
    def _():
        nxt = enext_ref[e]

        @pl.when(nxt >= 0)
        def _():
            for cp in wcopies(nxt, 1 - slot):
                cp.start()

        for cp in wcopies(e, slot):
            cp.wait()
        wgb_ref[...] = wgf_ref[slot].astype(BF16)
        wub_ref[...] = wuf_ref[slot].astype(BF16)
        wdb_ref[...] = wdf_ref[slot].astype(BF16)

    @pl.when(active)
    def _():
        x = _load_chunked(x_ref, x_ref.shape[0] // ROW_CHUNKS).astype(BF16)
        gate = _dot(x, wgb_ref[...])
        up = _dot(x, wub_ref[...])
        act = (gate * jax.nn.sigmoid(gate) * up).astype(BF16)
        _store_chunked(y_ref, _dot(act, wdb_ref[...]))


def _expert_ffn(xs, plan, w_gate, w_up, w_down, layer, t_ffn):
    tile_expert, n_used, e_next, e_ord = plan
    _, _, d, de = w_gate.shape
    n_rows = xs.shape[0] // ROW_CHUNKS
    n_tiles = n_rows // t_ffn
    rowmap = lambda j, te, nu, en, eo: (jnp.minimum(j, nu[0] - 1), 0)
    hbm = pl.BlockSpec(memory_space=pl.ANY)
    return pl.pallas_call(
        functools.partial(_ffn_kernel, layer=layer),
        grid_spec=pltpu.PrefetchScalarGridSpec(
            num_scalar_prefetch=4,
            grid=(n_tiles,),
            in_specs=[pl.BlockSpec((t_ffn * ROW_CHUNKS, LANES), rowmap), hbm, hbm, hbm],
            out_specs=pl.BlockSpec((t_ffn * ROW_CHUNKS, LANES), rowmap),
            scratch_shapes=[pltpu.VMEM((2, d, de), F32), pltpu.VMEM((2, d, de), F32), pltpu.VMEM((2, de, d), F32),
                            pltpu.VMEM((d, de), BF16), pltpu.VMEM((d, de), BF16), pltpu.VMEM((de, d), BF16),
                            pltpu.SemaphoreType.DMA((2, 3))],
        ),
        out_shape=jax.ShapeDtypeStruct((n_rows * ROW_CHUNKS, LANES), F32),
        compiler_params=_cparams(("arbitrary",)),
        name="expert_ffn",
    )(tile_expert, n_used, e_next, e_ord, xs, w_gate, w_up, w_down)


def _combine_kernel(dest_ref, dest_next_ref, y_hbm, wts_ref, x1_ref, mod_ref, lng_ref, lnb_ref, x2_ref,
                    buf_ref, sem, *, alpha):
    i = pl.program_id(0)
    n_steps = pl.num_programs(0)
    tc, d = x1_ref.shape

    def row_copy(idx_ref, parity, n, slot, rows):
        return pltpu.make_async_copy(y_hbm.at[idx_ref[0, slot, n]], buf_ref.at[parity * TOP_K + slot, rows],
                                     sem.at[parity])

    @pl.when(i == 0)
    def _():
        def body(n, carry):
            rows = pl.ds(pl.multiple_of(n * ROW_CHUNKS, ROW_CHUNKS), ROW_CHUNKS)
            for slot in range(TOP_K):
                row_copy(dest_ref, 0, n, slot, rows).start(priority=slot)
            return carry
        lax.fori_loop(0, tc, body, 0, unroll=DMA_UNROLL)

    for parity in range(2):
        @pl.when((i + 1 < n_steps) & ((i + 1) % 2 == parity))
        def _():
            for n in range(tc):
                for slot in range(TOP_K):
                    row_copy(dest_next_ref, parity, n, slot, pl.ds(n * ROW_CHUNKS, ROW_CHUNKS)).start(priority=slot)

    for slot in range(TOP_K):
        pltpu.make_async_copy(y_hbm.at[pl.ds(0, tc)], y_hbm.at[pl.ds(0, tc)], sem.at[i % 2]).wait()
    w = wts_ref[...]
    cur = (i % 2) * TOP_K
    f = w[:, 0:1] * _load_chunked(buf_ref.at[cur], tc) + w[:, 1:2] * _load_chunked(buf_ref.at[cur + 1], tc)
    g2 = mod_ref[0][:, 5 * d:6 * d]
    x2_ref[...] = _layer_norm(alpha * x1_ref[...] + g2 * f) * lng_ref[...] + lnb_ref[...]


def _combine(y, dest3, wts_t, x1, mods, lng, lnb, *, tok0, seq_len, mod_row, alpha):
    n, d = x1.shape
    tc = dest3.shape[2]
    tps = max(seq_len // tc, 1)
    steps = n // tc
    blk0 = tok0 // tc
    mod_map = (lambda i: (i // tps, 0, 0)) if mod_row is None else (lambda i: (mod_row, 0, 0))
    cur = lambda i: (blk0 + i, 0, 0)
    nxt = lambda i: (blk0 + jnp.minimum(i + 1, steps - 1), 0, 0)
    smem = pltpu.SMEM
    return pl.pallas_call(
        functools.partial(_combine_kernel, alpha=alpha),
        grid=(steps,),
        in_specs=[
            pl.BlockSpec((1, TOP_K, tc), cur, memory_space=smem),
            pl.BlockSpec((1, TOP_K, tc), nxt, memory_space=smem),
            pl.BlockSpec(memory_space=pl.ANY),
            pl.BlockSpec((tc, TOP_K), lambda i: (blk0 + i, 0)),
            pl.BlockSpec((tc, d), lambda i: (i, 0)),
            pl.BlockSpec((1, 1, mods.shape[2]), mod_map),
            pl.BlockSpec(lng.shape, lambda i: (0, 0)),
            pl.BlockSpec(lnb.shape, lambda i: (0, 0)),
        ],
        out_specs=pl.BlockSpec((tc, d), lambda i: (i, 0)),
        out_shape=jax.ShapeDtypeStruct((n, d), F32),
        scratch_shapes=[pltpu.VMEM((2 * TOP_K, tc * ROW_CHUNKS, LANES), F32), pltpu.SemaphoreType.DMA((2,))],
        compiler_params=_cparams(("arbitrary",)),
        name="combine",
    )(dest3, dest3, y, wts_t, x1, mods, lng, lnb)


def _rope_tables(seq_len):
    t = jnp.arange(seq_len)
    row = (t // GRID_W).astype(F32)
    col = (t % GRID_W).astype(F32)
    half = HEAD_DIM // 2
    inv_freq = ROPE_BASE ** (-jnp.arange(0, half, 2, dtype=F32) / half)
    ang_r, ang_c = row[:, None] * inv_freq, col[:, None] * inv_freq
    cr, sr, cc, sc = jnp.cos(ang_r), jnp.sin(ang_r), jnp.cos(ang_c), jnp.sin(ang_c)
    cos = jnp.concatenate([cr, cr, cc, cc], axis=1)
    sin = jnp.concatenate([-sr, sr, -sc, sc], axis=1)
    return jnp.tile(cos, (1, LANES // HEAD_DIM)), jnp.tile(sin, (1, LANES // HEAD_DIM))


def _pair_heads(a, axis):
    shp = a.shape
    a = a.reshape(*shp[:axis], N_KV_HEADS, Q_REP, HEAD_DIM, *shp[axis + 1:])
    a = jnp.swapaxes(a, axis, axis + 1)
    return a.reshape(shp)


def _moe_plan(eid3, rank3, counts, n_tok, t_ffn):
    counts = counts.reshape(EXPERTS_PER_GROUP, N_EXPERT_GROUPS).T.reshape(N_EXPERTS).astype(jnp.int32)
    padded = (counts + t_ffn - 1) // t_ffn * t_ffn
    pends = jnp.cumsum(padded)
    pstarts = (pends - padded).astype(jnp.int32)
    experts = jnp.arange(N_EXPERTS, dtype=jnp.int32)
    dest3 = rank3 + jnp.sum(jnp.where(eid3[..., None] == experts, pstarts, 0), axis=-1)
    n_rows = -(-(n_tok * TOP_K) // t_ffn) * t_ffn + N_EXPERTS * t_ffn
    n_tiles = n_rows // t_ffn
    tile_row0 = jnp.arange(n_tiles, dtype=jnp.int32) * t_ffn
    tile_expert = jnp.minimum(jnp.sum(pends[None, :] <= tile_row0[:, None], axis=1), N_EXPERTS - 1)
    n_used = (pends[-1] // t_ffn).astype(jnp.int32).reshape(1)
    nonempty = counts > 0
    later = (experts[None, :] > experts[:, None]) & nonempty[None, :]
    e_next = jnp.min(jnp.where(later, experts[None, :], N_EXPERTS), axis=1)
    e_next = jnp.where(e_next < N_EXPERTS, e_next, -1).astype(jnp.int32)
    e_ord = (jnp.cumsum(nonempty.astype(jnp.int32)) - 1).astype(jnp.int32)
    zlo = jnp.where(nonempty, pends - t_ffn, -1).astype(jnp.int32)
    plan = (tile_expert.astype(jnp.int32), n_used, e_next, e_ord)
    return dest3.astype(jnp.int32), plan, zlo, n_rows


def kernel(x, c, ctx, c_ctx, w_ada, b_ada, w_in, w_pool_grp, pool_scale, w_pool_br, w_attn_br, attn_sink,
           w_o, ln1_g, ln1_b, w_router, router_bias, w_exp_gate, w_exp_up, w_exp_down, ln2_g, ln2_b):
    bsz, seq, d = x.shape
    assert d == ROW_CHUNKS * LANES
    c_len = ctx.shape[1]
    depth = w_in.shape[0]
    n_lat, n_ctx = bsz * seq, bsz * c_len
    alpha = (2 * depth) ** 0.25

    cond = jnp.zeros((8, d), F32).at[:bsz].set(c).at[bsz].set(c_ctx)
    ada = _ada_terms(cond, w_ada, b_ada)
    cos, sin = _rope_tables(seq)
    wr_t = w_router.reshape(d, N_EXPERT_GROUPS, EXPERTS_PER_GROUP).transpose(2, 1, 0).reshape(N_EXPERTS, d)
    wr_t = wr_t.astype(BF16)
    bias_col = router_bias.reshape(N_EXPERT_GROUPS, EXPERTS_PER_GROUP).T.reshape(N_EXPERTS, 1).astype(F32)

    xl = x.reshape(n_lat, d)
    xc = ctx.reshape(n_ctx, d)
    for l in range(depth):
        ctx_out = l < depth - 1
        mods = ada[l].reshape(8, 1, 6 * d)
        w_l = w_in[l]
        w_inp = jnp.concatenate([w_l[:, :COL_Q], _pair_heads(w_l[:, COL_Q:COL_K], 1), w_l[:, COL_K:]],
                                axis=1).astype(BF16)
        sink_col = jnp.broadcast_to(attn_sink[l].reshape(N_KV_HEADS, Q_REP, 1, 1),
                                    (N_KV_HEADS, Q_REP, BLOCK, 1)).reshape(N_KV_HEADS, Q_REP * BLOCK, 1)
        sink_col = (sink_col * LOG2E).astype(F32)
        mix_w = (w_pool_grp[l].astype(BF16), pool_scale[l].reshape(1, POOL_W), w_pool_br[l].astype(BF16),
                 _pair_heads(w_attn_br[l], 0).astype(BF16), w_o[l].astype(BF16),
                 ln1_g[l].reshape(1, d), ln1_b[l].reshape(1, d), wr_t)
        lng2, lnb2 = ln2_g[l].reshape(1, d), ln2_b[l].reshape(1, d)

        if ctx_out:
            uc, qc, kc, vc, gc = _inproj(xc, mods, w_inp, cos, sin, seq_len=c_len, mod_row=bsz, rope=False)
        else:
            kc, vc = _inproj(xc, mods, w_inp, cos, sin, seq_len=c_len, mod_row=bsz, rope=False, kv_only=True)
        kc3, vc3 = kc.reshape(bsz, c_len, KV_W), vc.reshape(bsz, c_len, KV_W)
        u, q, k, v, g = _inproj(xl, mods, w_inp, cos, sin, seq_len=seq, mod_row=None, rope=True)
        n_tok = n_lat + n_ctx if ctx_out else n_lat
        tokens = None
        if ctx_out:
            attn_c = _attention(qc, kc, vc, kc3, vc3, sink_col, seq_len=c_len, local=False)
            xc1, tokens, logits_c = _merge(uc, attn_c, gc, xc, mods, mix_w, seq_len=c_len, mod_row=bsz,
                                           alpha=alpha, h2_tokens=n_tok, h2_tok0=n_lat)
        attn = _attention(q, k, v, kc3, vc3, sink_col, seq_len=seq, local=True)
        x1, tokens, logits = _merge(u, attn, g, xl, mods, mix_w, seq_len=seq, mod_row=None, alpha=alpha,
                                    h2_tokens=n_tok, h2_tok0=0, h2_buf=tokens)
        if ctx_out:
            logits = jnp.concatenate([logits, logits_c], axis=1)

        eid3, wts, rank3, counts = _route(logits, bias_col)
        dest3, plan, zlo, n_rows = _moe_plan(eid3, rank3, counts[:, 0], n_tok, T_FFN)
        xs = _dispatch(tokens, dest3, zlo, n_rows, T_FFN)
        y = _expert_ffn(xs.reshape(n_rows * ROW_CHUNKS, LANES), plan, w_exp_gate, w_exp_up, w_exp_down, l, T_FFN)
        y3 = y.reshape(n_rows, ROW_CHUNKS, LANES)
        wts_t = wts.T
        xl = _combine(y3, dest3, wts_t, x1, mods, lng2, lnb2, tok0=0, seq_len=seq, mod_row=None, alpha=alpha)
        if ctx_out:
            xc = _combine(y3, dest3, wts_t, xc1, mods, lng2, lnb2, tok0=n_lat, seq_len=c_len, mod_row=bsz,
                          alpha=alpha)
    return xl.reshape(bsz, seq, d)
```

```python
import functools

import jax
import jax.numpy as jnp
from jax import lax
from jax.experimental import pallas as pl
from jax.experimental.pallas import tpu as pltpu

F32 = jnp.float32
BF16 = jnp.bfloat16

GRID_W = 64
POOL_WINDOWS = (2, 4, 8, 16)
POOL_GROUP_W = 128
POOL_W = 512
HEAD_DIM = 64
N_HEADS = 8
N_KV_HEADS = 2
Q_REP = N_HEADS // N_KV_HEADS
ATTN_W = N_HEADS * HEAD_DIM
KV_W = N_KV_HEADS * HEAD_DIM
BLOCK = 128
ROPE_BASE = 10000.0
COL_POOL = 0
COL_Q = COL_POOL + POOL_W
COL_K = COL_Q + ATTN_W
COL_V = COL_K + KV_W
COL_GATE = COL_V + KV_W
N_EXPERTS = 32
N_EXPERT_GROUPS = 8
EXPERTS_PER_GROUP = N_EXPERTS // N_EXPERT_GROUPS
TOP_K = 2
LN_EPS = 1e-6
LOG2E = 1.4426950408889634

LANES = 128
POOL_HALO = 8
ROW_CHUNKS = 8
VMEM_LIMIT = 56 * 1024 * 1024

TM_IN = 512
TQ_ATT = 2048
TQ_MIX = 512
TN_TOK = 512
T_FFN = 256
DMA_UNROLL = 8


def _cparams(sem, flags=None):
    return pltpu.CompilerParams(dimension_semantics=sem, vmem_limit_bytes=VMEM_LIMIT, flags=flags)


def _layer_norm(x, eps=LN_EPS):
    mu = jnp.mean(x, axis=-1, keepdims=True)
    xc = x - mu
    var = jnp.mean(xc * xc, axis=-1, keepdims=True)
    return xc * lax.rsqrt(var + eps)


def _post_norm(x, gate, y, alpha):
    return _layer_norm(x + (gate * (1.0 / alpha)) * y, LN_EPS / (alpha * alpha))


def _dot(a, b):
    return jnp.dot(a, b, preferred_element_type=F32)


def _store_chunked(ref, val):
    t = val.shape[0]
    for s in range(ROW_CHUNKS):
        ref[pl.ds(s, t, stride=ROW_CHUNKS), :] = val[:, s * LANES:(s + 1) * LANES]


def _load_chunked(ref, t):
    return jnp.concatenate([ref[pl.ds(s, t, stride=ROW_CHUNKS), :] for s in range(ROW_CHUNKS)], axis=1)


def _ada_kernel(cond_ref, w_ref, b_ref, o_ref):
    s = cond_ref[...]
    s = s * jax.nn.sigmoid(s)
    o_ref[0] = _dot(s.astype(BF16), w_ref[0].astype(BF16)) + b_ref[0]


def _ada_terms(cond, w_ada, b_ada):
    depth, d, n6 = w_ada.shape
    tn = n6 // 4
    return pl.pallas_call(
        _ada_kernel,
        grid=(depth, n6 // tn),
        in_specs=[
            pl.BlockSpec((8, d), lambda l, j: (0, 0)),
            pl.BlockSpec((1, d, tn), lambda l, j: (l, 0, j)),
            pl.BlockSpec((1, 1, tn), lambda l, j: (l, 0, j)),
        ],
        out_specs=pl.BlockSpec((1, 8, tn), lambda l, j: (l, 0, j)),
        out_shape=jax.ShapeDtypeStruct((depth, 8, n6), F32),
        compiler_params=_cparams(("arbitrary", "arbitrary")),
        name="ada_terms",
    )(cond, w_ada, b_ada.reshape(depth, 1, n6))


def _rope(t, cos, sin):
    lane = lax.broadcasted_iota(jnp.int32, (1, LANES), 1)
    first = (lane % 32) < 16
    outs = []
    for j in range(t.shape[1] // LANES):
        tj = t[:, j * LANES:(j + 1) * LANES]
        partner = jnp.where(first, pltpu.roll(tj, LANES - 16, 1), pltpu.roll(tj, 16, 1))
        outs.append(tj * cos + partner * sin)
    return outs[0] if len(outs) == 1 else jnp.concatenate(outs, axis=1)


def _inproj_kernel(x_ref, mod_ref, w_ref, cos_ref, sin_ref, *out_refs, rope, kv_only):
    d = x_ref.shape[1]
    mod = mod_ref[0]
    shift, scale = mod[:, 0:d], mod[:, d:2 * d]
    h = (_layer_norm(x_ref[...]) * (1.0 + scale) + shift).astype(BF16)

    def proj(lo, hi):
        return _dot(h, w_ref[:, lo:hi])

    if kv_only:
        k_ref, v_ref = out_refs
    else:
        u_ref, q_ref, k_ref, v_ref, g_ref = out_refs
        u_ref[...] = proj(COL_POOL, COL_Q)
        q = proj(COL_Q, COL_K)
        if rope:
            q = _rope(q, cos_ref[...], sin_ref[...])
        q_ref[...] = (q * (LOG2E * HEAD_DIM ** -0.5)).astype(BF16)
        g_ref[...] = jax.nn.sigmoid(proj(COL_GATE, w_ref.shape[1]))
    kv = proj(COL_K, COL_GATE)
    k = kv[:, 0:KV_W]
    if rope:
        k = _rope(k, cos_ref[...], sin_ref[...])
    k_ref[...] = k.astype(BF16)
    v_ref[...] = kv[:, KV_W:2 * KV_W].astype(BF16)


def _inproj(x2d, mods, w_in, cos, sin, *, seq_len, mod_row, rope, kv_only=False):
    n, d = x2d.shape
    tm = min(TM_IN, seq_len)
    tps = seq_len // tm
    n_cols = w_in.shape[1]
    mod_map = (lambda i: (i // tps, 0, 0)) if mod_row is None else (lambda i: (mod_row, 0, 0))
    tab_map = (lambda i: (i % tps, 0)) if rope else (lambda i: (0, 0))
    row = lambda i: (i, 0)
    kv_shapes = [jax.ShapeDtypeStruct((n, KV_W), BF16)] * 2
    kv_specs = [pl.BlockSpec((tm, KV_W), row)] * 2
    if kv_only:
        out_shape, out_specs = kv_shapes, kv_specs
    else:
        out_shape = [jax.ShapeDtypeStruct((n, POOL_W), F32), jax.ShapeDtypeStruct((n, ATTN_W), BF16),
                     *kv_shapes, jax.ShapeDtypeStruct((n, n_cols - COL_GATE), F32)]
        out_specs = [pl.BlockSpec((tm, POOL_W), row), pl.BlockSpec((tm, ATTN_W), row),
                     *kv_specs, pl.BlockSpec((tm, n_cols - COL_GATE), row)]
    return pl.pallas_call(
        functools.partial(_inproj_kernel, rope=rope, kv_only=kv_only),
        grid=(n // tm,),
        in_specs=[
            pl.BlockSpec((tm, d), row),
            pl.BlockSpec((1, 1, mods.shape[2]), mod_map),
            pl.BlockSpec((d, n_cols), lambda i: (0, 0)),
            pl.BlockSpec((tm, LANES), tab_map),
            pl.BlockSpec((tm, LANES), tab_map),
        ],
        out_specs=out_specs,
        out_shape=out_shape,
        compiler_params=_cparams(("parallel",)),
        name="inproj_kv" if kv_only else "inproj",
    )(x2d, mods, w_in, cos, sin)


def _merge_kernel(u_prev_ref, u_ref, u_next_ref, attn_ref, g_ref, x_ref, mod_ref,
                  wgrp_ref, pscale_ref, wpool_ref, wattn_ref, wo_ref, lng_ref, lnb_ref, wr_ref,
                  *rest, seq_len, alpha):
    x1_ref, h2_ref, logit_ref, uext_ref = rest[-4:]
    tq, d = x_ref.shape
    tps = seq_len // tq
    t_in_seq = pl.program_id(0) % tps
    is_first = t_in_seq == 0
    is_last = t_in_seq == tps - 1

    h8 = POOL_HALO
    uext_ref[0:h8, :] = jnp.where(is_first, 0.0, u_prev_ref[...])
    uext_ref[h8:h8 + tq, :] = u_ref[...]
    uext_ref[h8 + tq:, :] = jnp.where(is_last, 0.0, u_next_ref[...])
    pos = t_in_seq * tq + lax.broadcasted_iota(jnp.int32, (tq, 1), 0)
    pooled = []
    for gi, w in enumerate(POOL_WINDOWS):
        cols = slice(gi * POOL_GROUP_W, (gi + 1) * POOL_GROUP_W)
        acc = uext_ref[h8 - w // 2:h8 - w // 2 + tq, cols]
        for off in range(-w // 2 + 1, w // 2):
            acc = acc + uext_ref[h8 + off:h8 + off + tq, cols]
        lo = jnp.maximum(pos - w // 2, 0)
        hi = jnp.minimum(pos - w // 2 + w - 1, seq_len - 1)
        mean = acc / (hi - lo + 1).astype(F32)
        pg = (mean - u_ref[:, cols]).astype(BF16)
        pooled.append(_dot(pg, wgrp_ref[gi]))
    pool_lat = jnp.concatenate(pooled, axis=1) * pscale_ref[...]
    pool_proj = _dot(pool_lat.astype(BF16), wpool_ref[...])
    attn_proj = _dot(attn_ref[...], wattn_ref[...])

    gates = g_ref[...]
    merged = gates[:, 0:d] * pool_proj + gates[:, d:2 * d] * attn_proj
    y = _dot(merged.astype(BF16), wo_ref[...])
    mod = mod_ref[0]
    g1 = mod[:, 2 * d:3 * d]
    sh2, sc2 = mod[:, 3 * d:4 * d], mod[:, 4 * d:5 * d]
    x1 = _post_norm(x_ref[...], g1, y, alpha) * lng_ref[...] + lnb_ref[...]
    x1_ref[...] = x1
    h2 = _layer_norm(x1) * (1.0 + sc2) + sh2
    _store_chunked(h2_ref, h2)
    logit_ref[...] = lax.dot_general(wr_ref[...], h2.astype(BF16), (((1,), (1,)), ((), ())),
                                     preferred_element_type=F32)


def _attn_kernel(q_ref, k_prev_ref, k_ref, k_next_ref, v_prev_ref, v_ref, v_next_ref, kc_ref, vc_ref,
                 sink_ref, attn_ref, s0_ref, s1_ref, kg_ref, vg_ref, *, seq_len, local):
    tq = q_ref.shape[0]
    nb = tq // BLOCK
    tps = seq_len // tq
    t_in_seq = pl.program_id(0) % tps
    is_first = t_in_seq == 0
    is_last = t_in_seq == tps - 1
    lane = lax.broadcasted_iota(jnp.int32, (1, LANES), 1)
    lo_half = lane < HEAD_DIM
    one = jnp.ones((), BF16)
    kc = kc_ref[0]
    vc = vc_ref[0]
    kc_g = [jnp.where(lo_half, kc, 0), jnp.where(lo_half, 0, kc)]
    vc_g = [jnp.where(lo_half, vc, one), jnp.where(lo_half, one, vc)]
    neg = jnp.float32(-jnp.inf)
    if local:
        k_ext = jnp.concatenate([k_prev_ref[...], k_ref[...], k_next_ref[...]], axis=0)
        v_ext = jnp.concatenate([v_prev_ref[...], v_ref[...], v_next_ref[...]], axis=0)
        kg_ref[0] = jnp.where(lo_half, k_ext, 0)
        kg_ref[1] = jnp.where(lo_half, 0, k_ext)
        vg_ref[0] = jnp.where(lo_half, v_ext, one)
        vg_ref[1] = jnp.where(lo_half, one, v_ext)
        qq = lax.broadcasted_iota(jnp.int32, (Q_REP * BLOCK, BLOCK), 0) % BLOCK
        kk = lax.broadcasted_iota(jnp.int32, (Q_REP * BLOCK, BLOCK), 1)
        mask_prev = jnp.where(kk >= qq, 0.0, neg)
        mask_next = jnp.where(kk <= qq, 0.0, neg)

    def block_rows(b):
        return pl.ds(b * BLOCK, BLOCK) if isinstance(b, int) else pl.ds(pl.multiple_of(b * BLOCK, BLOCK), BLOCK)

    def band_keys(b):
        start = b * BLOCK if isinstance(b, int) else pl.multiple_of(b * BLOCK, BLOCK)
        return pl.ds(start, 3 * BLOCK)

    def stage_a(b, s_buf, first_block, last_block):
        q_st = jnp.concatenate([q_ref[block_rows(b), c * LANES:(c + 1) * LANES] for c in range(Q_REP)], axis=0)
        for g in range(N_KV_HEADS):
            k_all = jnp.concatenate([kg_ref[g, band_keys(b), :], kc_g[g]], axis=0) if local else kc_g[g]
            s = lax.dot_general(q_st, k_all, (((1,), (1,)), ((), ())), preferred_element_type=F32)
            if local:
                m_prev = jnp.where(first_block, neg, mask_prev)
                m_next = jnp.where(last_block, neg, mask_next)
                s = jnp.concatenate([s[:, 0:BLOCK] + m_prev, s[:, BLOCK:2 * BLOCK],
                                     s[:, 2 * BLOCK:3 * BLOCK] + m_next, s[:, 3 * BLOCK:]], axis=1)
            s_buf[g] = s

    def stage_b(b, s_buf):
        pv, sink_w = [], []
        for g in range(N_KV_HEADS):
            v_all = jnp.concatenate([vg_ref[g, band_keys(b), :], vc_g[g]], axis=0) if local else vc_g[g]
            s = s_buf[g]
            sk = sink_ref[g]
            m = jnp.maximum(jnp.max(s, axis=-1, keepdims=True), sk)
            p = jnp.exp2((s - m).astype(BF16))
            pv.append(_dot(p, v_all))
            sink_w.append(jnp.exp2(sk - m))
        num = jnp.where(lo_half, pv[0], pv[1])
        den = pltpu.roll(jnp.where(lo_half, pv[1], pv[0]), HEAD_DIM, 1) + jnp.where(lo_half, sink_w[0], sink_w[1])
        o = num / den
        for c in range(Q_REP):
            attn_ref[block_rows(b), c * LANES:(c + 1) * LANES] = o[c * BLOCK:(c + 1) * BLOCK].astype(BF16)

    stage_a(0, s0_ref, is_first, False)

    def body(j, carry):
        stage_a(2 * j + 1, s1_ref, False, False)
        stage_b(2 * j, s0_ref)
        stage_a(2 * j + 2, s0_ref, False, False)
        stage_b(2 * j + 1, s1_ref)
        return carry

    lax.fori_loop(0, nb // 2 - 1, body, 0)
    stage_a(nb - 1, s1_ref, False, is_last)
    stage_b(nb - 2, s0_ref)
    stage_b(nb - 1, s1_ref)


def _attention(q, k, v, kc, vc, sink_col, *, seq_len, local):
    n = q.shape[0]
    tq = min(TQ_ATT, seq_len)
    tps = seq_len // tq
    kb = tq // BLOCK
    n_kb = n // BLOCK
    c_len = kc.shape[1]
    n_keys = (3 * BLOCK if local else 0) + c_len
    row = lambda i: (i, 0)
    ctx_map = lambda i: (i // tps, 0, 0)
    kv_prev = pl.BlockSpec((BLOCK, KV_W), lambda i: (jnp.maximum(i * kb - 1, 0), 0))
    kv_cur = pl.BlockSpec((tq, KV_W), row)
    kv_next = pl.BlockSpec((BLOCK, KV_W), lambda i: (jnp.minimum((i + 1) * kb, n_kb - 1), 0))
    return pl.pallas_call(
        functools.partial(_attn_kernel, seq_len=seq_len, local=local),
        grid=(n // tq,),
        in_specs=[
            pl.BlockSpec((tq, ATTN_W), row),
            kv_prev, kv_cur, kv_next, kv_prev, kv_cur, kv_next,
            pl.BlockSpec((1, c_len, KV_W), ctx_map), pl.BlockSpec((1, c_len, KV_W), ctx_map),
            pl.BlockSpec(sink_col.shape, lambda i: (0, 0, 0)),
        ],
        out_specs=pl.BlockSpec((tq, ATTN_W), row),
        out_shape=jax.ShapeDtypeStruct((n, ATTN_W), BF16),
        scratch_shapes=[pltpu.VMEM((N_KV_HEADS, Q_REP * BLOCK, n_keys), F32),
                        pltpu.VMEM((N_KV_HEADS, Q_REP * BLOCK, n_keys), F32),
                        pltpu.VMEM((N_KV_HEADS, tq + 2 * BLOCK, KV_W), BF16),
                        pltpu.VMEM((N_KV_HEADS, tq + 2 * BLOCK, KV_W), BF16)],
        compiler_params=_cparams(("parallel",)),
        name="attention" if local else "attention_ctx",
    )(q, k, k, k, v, v, v, kc, vc, sink_col)


def _merge(u, attn, gates, x2d, mods, wts, *, seq_len, mod_row, alpha, h2_tokens, h2_tok0, h2_buf=None):
    n, d = x2d.shape
    tq = min(TQ_MIX, seq_len)
    h2_blk0 = h2_tok0 // tq
    tps = seq_len // tq
    hb = tq // POOL_HALO
    n_hb = n // POOL_HALO
    row = lambda i: (i, 0)
    const2 = lambda i: (0, 0)
    const3 = lambda i: (0, 0, 0)
    mod_map = (lambda i: (i // tps, 0, 0)) if mod_row is None else (lambda i: (mod_row, 0, 0))
    u_prev = pl.BlockSpec((POOL_HALO, POOL_W), lambda i: (jnp.maximum(i * hb - 1, 0), 0))
    u_next = pl.BlockSpec((POOL_HALO, POOL_W), lambda i: (jnp.minimum((i + 1) * hb, n_hb - 1), 0))
    wgrp, pscale, wpool, wattn, wo, lng, lnb, wr_t = wts
    operands = [u, u, u, attn, gates, x2d, mods, wgrp, pscale, wpool, wattn, wo, lng, lnb, wr_t]
    alias_specs, aliases = [], {}
    if h2_buf is not None:
        alias_specs = [pl.BlockSpec(memory_space=pl.ANY)]
        aliases = {len(operands): 1}
        operands.append(h2_buf)
    return pl.pallas_call(
        functools.partial(_merge_kernel, seq_len=seq_len, alpha=alpha),
        grid=(n // tq,),
        input_output_aliases=aliases,
        in_specs=[
            u_prev, pl.BlockSpec((tq, POOL_W), row), u_next,
            pl.BlockSpec((tq, ATTN_W), row),
            pl.BlockSpec((tq, 2 * d), row),
            pl.BlockSpec((tq, d), row),
            pl.BlockSpec((1, 1, mods.shape[2]), mod_map),
            pl.BlockSpec(wgrp.shape, const3), pl.BlockSpec(pscale.shape, const2),
            pl.BlockSpec(wpool.shape, const2), pl.BlockSpec(wattn.shape, const2),
            pl.BlockSpec(wo.shape, const2),
            pl.BlockSpec(lng.shape, const2), pl.BlockSpec(lnb.shape, const2),
            pl.BlockSpec(wr_t.shape, const2),
            *alias_specs,
        ],
        out_specs=[pl.BlockSpec((tq, d), row),
                   pl.BlockSpec((tq * ROW_CHUNKS, LANES), lambda i: (h2_blk0 + i, 0)),
                   pl.BlockSpec((N_EXPERTS, tq), lambda i: (0, i))],
        out_shape=[jax.ShapeDtypeStruct((n, d), F32),
                   jax.ShapeDtypeStruct((h2_tokens * ROW_CHUNKS, LANES), F32),
                   jax.ShapeDtypeStruct((N_EXPERTS, n), F32)],
        scratch_shapes=[pltpu.VMEM((tq + 2 * POOL_HALO, POOL_W), F32)],
        compiler_params=_cparams(("parallel",)),
        name="merge",
    )(*operands)


def _route_kernel(logit_ref, bias_ref, eid_ref, wts_ref, rank_ref, cnt_ref, base_ref):
    tn = logit_ref.shape[1]
    ng, epg = N_EXPERT_GROUPS, EXPERTS_PER_GROUP

    @pl.when(pl.program_id(0) == 0)
    def _():
        base_ref[...] = jnp.zeros_like(base_ref)

    scores = jax.nn.sigmoid(logit_ref[...])
    biased = scores + bias_ref[...]
    bj = [biased[j * ng:(j + 1) * ng] for j in range(epg)]
    sj = [scores[j * ng:(j + 1) * ng] for j in range(epg)]
    hi01, lo01 = jnp.maximum(bj[0], bj[1]), jnp.minimum(bj[0], bj[1])
    hi23, lo23 = jnp.maximum(bj[2], bj[3]), jnp.minimum(bj[2], bj[3])
    gscore = jnp.maximum(hi01, hi23) + jnp.maximum(jnp.minimum(hi01, hi23), jnp.maximum(lo01, lo23))
    giota = lax.broadcasted_iota(jnp.int32, (ng, tn), 0)
    gmax = jnp.max(gscore, axis=0, keepdims=True)
    g_first = jnp.min(jnp.where(gscore == gmax, giota.astype(F32), float(ng)), axis=0, keepdims=True)
    g_sel = g_first.astype(jnp.int32)
    in_g = giota == g_sel
    vb = [jnp.sum(jnp.where(in_g, b, 0.0), axis=0, keepdims=True) for b in bj]
    vs = [jnp.sum(jnp.where(in_g, s, 0.0), axis=0, keepdims=True) for s in sj]

    def first_best(vals):
        best = functools.reduce(jnp.maximum, vals)
        idx = jnp.full(best.shape, epg - 1, jnp.int32)
        for j in range(epg - 2, -1, -1):
            idx = jnp.where(vals[j] == best, j, idx)
        return idx

    def pick(vals, idx):
        out = vals[epg - 1]
        for j in range(epg - 2, -1, -1):
            out = jnp.where(idx == j, vals[j], out)
        return out

    l1 = first_best(vb)
    l2 = first_best([jnp.where(l1 == j, -jnp.inf, vb[j]) for j in range(epg)])
    w1, w2 = pick(vs, l1), pick(vs, l2)
    wsum = w1 + w2
    eid_ref[0, 0:1, :] = g_sel * epg + l1
    eid_ref[0, 1:2, :] = g_sel * epg + l2
    wts_ref[0:1, :] = w1 / wsum
    wts_ref[1:2, :] = w2 / wsum

    r1, r2 = l1 * ng + g_sel, l2 * ng + g_sel
    riota = lax.broadcasted_iota(jnp.int32, (N_EXPERTS, tn), 0)
    hit1, hit2 = riota == r1, riota == r2
    onehot = jnp.where(hit1 | hit2, 1.0, 0.0)
    before = lax.broadcasted_iota(jnp.int32, (tn, tn), 0) < lax.broadcasted_iota(jnp.int32, (tn, tn), 1)
    prefix = _dot(onehot.astype(BF16), jnp.where(before, 1.0, 0.0).astype(BF16)) + base_ref[:, 0:1]
    rank_ref[0, 0:1, :] = jnp.sum(jnp.where(hit1, prefix, 0.0), axis=0, keepdims=True).astype(jnp.int32)
    rank_ref[0, 1:2, :] = jnp.sum(jnp.where(hit2, prefix, 0.0), axis=0, keepdims=True).astype(jnp.int32)
    base_ref[...] = base_ref[...] + jnp.sum(onehot, axis=1, keepdims=True)
    cnt_ref[...] = base_ref[...]


def _route(logits, bias_col):
    n = logits.shape[1]
    tn = TN_TOK
    steps = n // tn
    col = lambda i: (0, i)
    blk = lambda i: (i, 0, 0)
    return pl.pallas_call(
        _route_kernel,
        grid=(steps,),
        in_specs=[
            pl.BlockSpec((N_EXPERTS, tn), col),
            pl.BlockSpec((N_EXPERTS, 1), lambda i: (0, 0)),
        ],
        out_specs=[pl.BlockSpec((1, TOP_K, tn), blk), pl.BlockSpec((TOP_K, tn), col),
                   pl.BlockSpec((1, TOP_K, tn), blk), pl.BlockSpec((N_EXPERTS, LANES), lambda i: (0, 0))],
        out_shape=[jax.ShapeDtypeStruct((steps, TOP_K, tn), jnp.int32), jax.ShapeDtypeStruct((TOP_K, n), F32),
                   jax.ShapeDtypeStruct((steps, TOP_K, tn), jnp.int32),
                   jax.ShapeDtypeStruct((N_EXPERTS, LANES), F32)],
        scratch_shapes=[pltpu.VMEM((N_EXPERTS, LANES), F32)],
        compiler_params=_cparams(("arbitrary",)),
        name="route",
    )(logits, bias_col)


def _dispatch_kernel(dest_ref, zlo_ref, tok_ref, xs_hbm, zero_ref, zsem, sem, *, t_ffn):
    i = pl.program_id(0)
    tn = dest_ref.shape[2]

    @pl.when(i == 0)
    def _():
        zero_ref[...] = jnp.zeros_like(zero_ref)

        def zcopy(e):
            return pltpu.make_async_copy(zero_ref, xs_hbm.at[pl.ds(jnp.maximum(zlo_ref[e], 0), t_ffn)], zsem)

        def start(e, carry):
            @pl.when(zlo_ref[e] >= 0)
            def _():
                zcopy(e).start()
            return carry

        def wait(e, carry):
            @pl.when(zlo_ref[e] >= 0)
            def _():
                zcopy(e).wait()
            return carry

        lax.fori_loop(0, N_EXPERTS, start, 0)
        lax.fori_loop(0, N_EXPERTS, wait, 0)

    for n in range(tn):
        src = tok_ref.at[pl.ds(n * ROW_CHUNKS, ROW_CHUNKS)]
        for slot in range(TOP_K):
            pltpu.make_async_copy(src, xs_hbm.at[dest_ref[0, slot, n]], sem).start(priority=slot)
    for slot in range(TOP_K):
        pltpu.make_async_copy(xs_hbm.at[pl.ds(0, tn)], xs_hbm.at[pl.ds(0, tn)], sem).wait()


def _dispatch(tokens, dest3, zlo, n_rows, t_ffn):
    steps, _, tn = dest3.shape
    smem = pltpu.SMEM
    return pl.pallas_call(
        functools.partial(_dispatch_kernel, t_ffn=t_ffn),
        grid=(steps,),
        in_specs=[
            pl.BlockSpec((1, TOP_K, tn), lambda i: (i, 0, 0), memory_space=smem),
            pl.BlockSpec(memory_space=smem),
            pl.BlockSpec((tn * ROW_CHUNKS, LANES), lambda i: (i, 0)),
        ],
        out_specs=pl.BlockSpec(memory_space=pl.ANY),
        out_shape=jax.ShapeDtypeStruct((n_rows, ROW_CHUNKS, LANES), tokens.dtype),
        scratch_shapes=[pltpu.VMEM((t_ffn, ROW_CHUNKS, LANES), tokens.dtype), pltpu.SemaphoreType.DMA,
                        pltpu.SemaphoreType.DMA],
        compiler_params=_cparams(("arbitrary",)),
        name="dispatch",
    )(dest3, zlo, tokens)


def _ffn_kernel(texp_ref, nused_ref, enext_ref, eord_ref, x_ref, wg_hbm, wu_hbm, wd_hbm, y_ref,
                wgf_ref, wuf_ref, wdf_ref, wgb_ref, wub_ref, wdb_ref, wsem, *, layer):
    j = pl.program_id(0)
    e = texp_ref[j]
    active = j < nused_ref[0]
    first = (j == 0) | (e != texp_ref[jnp.maximum(j - 1, 0)])
    slot = eord_ref[e] % 2
    pairs = ((wg_hbm, wgf_ref), (wu_hbm, wuf_ref), (wd_hbm, wdf_ref))

    def wcopies(expert, s):
        return [pltpu.make_async_copy(hbm.at[layer, expert], buf.at[s], wsem.at[s, i])
                for i, (hbm, buf) in enumerate(pairs)]

    @pl.when(active & (j == 0))
    def _():
        for cp in wcopies(e, slot):
            cp.start()

    @pl.when(active & first)
    def _():
        nxt = enext_ref[e]

        @pl.when(nxt >= 0)
        def _():
            for cp in wcopies(nxt, 1 - slot):
                cp.start(priority=1)

        for cp in wcopies(e, slot):
            cp.wait()
        wgb_ref[...] = wgf_ref[slot].astype(BF16)
        wub_ref[...] = wuf_ref[slot].astype(BF16)
        wdb_ref[...] = wdf_ref[slot].astype(BF16)

    @pl.when(active)
    def _():
        x = _load_chunked(x_ref, x_ref.shape[0] // ROW_CHUNKS).astype(BF16)
        gate = _dot(x, wgb_ref[...])
        up = _dot(x, wub_ref[...])
        act = (gate * jax.nn.sigmoid(gate) * up).astype(BF16)
        _store_chunked(y_ref, _dot(act, wdb_ref[...]))


def _expert_ffn(xs, plan, w_gate, w_up, w_down, layer, t_ffn):
    tile_expert, n_used, e_next, e_ord = plan
    _, _, d, de = w_gate.shape
    n_rows = xs.shape[0] // ROW_CHUNKS
    n_tiles = n_rows // t_ffn
    rowmap = lambda j, te, nu, en, eo: (jnp.minimum(j, nu[0] - 1), 0)
    hbm = pl.BlockSpec(memory_space=pl.ANY)
    return pl.pallas_call(
        functools.partial(_ffn_kernel, layer=layer),
        grid_spec=pltpu.PrefetchScalarGridSpec(
            num_scalar_prefetch=4,
            grid=(n_tiles,),
            in_specs=[pl.BlockSpec((t_ffn * ROW_CHUNKS, LANES), rowmap), hbm, hbm, hbm],
            out_specs=pl.BlockSpec((t_ffn * ROW_CHUNKS, LANES), rowmap),
            scratch_shapes=[pltpu.VMEM((2, d, de), F32), pltpu.VMEM((2, d, de), F32), pltpu.VMEM((2, de, d), F32),
                            pltpu.VMEM((d, de), BF16), pltpu.VMEM((d, de), BF16), pltpu.VMEM((de, d), BF16),
                            pltpu.SemaphoreType.DMA((2, 3))],
        ),
        out_shape=jax.ShapeDtypeStruct((n_rows * ROW_CHUNKS, LANES), F32),
        compiler_params=_cparams(("arbitrary",)),
        name="expert_ffn",
    )(tile_expert, n_used, e_next, e_ord, xs, w_gate, w_up, w_down)


def _combine_kernel(dest_ref, dest_next_ref, y_hbm, wts_ref, x1_ref, mod_ref, lng_ref, lnb_ref, x2_ref,
                    buf_ref, sem, *, alpha):
    i = pl.program_id(0)
    n_steps = pl.num_programs(0)
    tc, d = x1_ref.shape

    def row_copy(idx_ref, parity, n, slot, rows):
        return pltpu.make_async_copy(y_hbm.at[idx_ref[0, slot, n]], buf_ref.at[parity * TOP_K + slot, rows],
                                     sem.at[parity])

    @pl.when(i == 0)
    def _():
        def body(n, carry):
            rows = pl.ds(pl.multiple_of(n * ROW_CHUNKS, ROW_CHUNKS), ROW_CHUNKS)
            for slot in range(TOP_K):
                row_copy(dest_ref, 0, n, slot, rows).start(priority=slot)
            return carry
        lax.fori_loop(0, tc, body, 0, unroll=DMA_UNROLL)

    for parity in range(2):
        @pl.when((i + 1 < n_steps) & ((i + 1) % 2 == parity))
        def _():
            for n in range(tc):
                for slot in range(TOP_K):
                    row_copy(dest_next_ref, parity, n, slot, pl.ds(n * ROW_CHUNKS, ROW_CHUNKS)).start(priority=slot)

    for slot in range(TOP_K):
        pltpu.make_async_copy(y_hbm.at[pl.ds(0, tc)], y_hbm.at[pl.ds(0, tc)], sem.at[i % 2]).wait()
    w = wts_ref[...]
    cur = (i % 2) * TOP_K
    f = w[:, 0:1] * _load_chunked(buf_ref.at[cur], tc) + w[:, 1:2] * _load_chunked(buf_ref.at[cur + 1], tc)
    g2 = mod_ref[0][:, 5 * d:6 * d]
    x2_ref[...] = _post_norm(x1_ref[...], g2, f, alpha) * lng_ref[...] + lnb_ref[...]


def _combine(y, dest3, wts_t, x1, mods, lng, lnb, *, tok0, seq_len, mod_row, alpha):
    n, d = x1.shape
    tc = dest3.shape[2]
    tps = max(seq_len // tc, 1)
    steps = n // tc
    blk0 = tok0 // tc
    mod_map = (lambda i: (i // tps, 0, 0)) if mod_row is None else (lambda i: (mod_row, 0, 0))
    cur = lambda i: (blk0 + i, 0, 0)
    nxt = lambda i: (blk0 + jnp.minimum(i + 1, steps - 1), 0, 0)
    smem = pltpu.SMEM
    return pl.pallas_call(
        functools.partial(_combine_kernel, alpha=alpha),
        grid=(steps,),
        in_specs=[
            pl.BlockSpec((1, TOP_K, tc), cur, memory_space=smem),
            pl.BlockSpec((1, TOP_K, tc), nxt, memory_space=smem),
            pl.BlockSpec(memory_space=pl.ANY),
            pl.BlockSpec((tc, TOP_K), lambda i: (blk0 + i, 0)),
            pl.BlockSpec((tc, d), lambda i: (i, 0)),
            pl.BlockSpec((1, 1, mods.shape[2]), mod_map),
            pl.BlockSpec(lng.shape, lambda i: (0, 0)),
            pl.BlockSpec(lnb.shape, lambda i: (0, 0)),
        ],
        out_specs=pl.BlockSpec((tc, d), lambda i: (i, 0)),
        out_shape=jax.ShapeDtypeStruct((n, d), F32),
        scratch_shapes=[pltpu.VMEM((2 * TOP_K, tc * ROW_CHUNKS, LANES), F32), pltpu.SemaphoreType.DMA((2,))],
        compiler_params=_cparams(("arbitrary",)),
        name="combine",
    )(dest3, dest3, y, wts_t, x1, mods, lng, lnb)


def _rope_tables(seq_len):
    t = jnp.arange(seq_len)
    row = (t // GRID_W).astype(F32)
    col = (t % GRID_W).astype(F32)
    half = HEAD_DIM // 2
    inv_freq = ROPE_BASE ** (-jnp.arange(0, half, 2, dtype=F32) / half)
    ang_r, ang_c = row[:, None] * inv_freq, col[:, None] * inv_freq
    cr, sr, cc, sc = jnp.cos(ang_r), jnp.sin(ang_r), jnp.cos(ang_c), jnp.sin(ang_c)
    cos = jnp.concatenate([cr, cr, cc, cc], axis=1)
    sin = jnp.concatenate([-sr, sr, -sc, sc], axis=1)
    return jnp.tile(cos, (1, LANES // HEAD_DIM)), jnp.tile(sin, (1, LANES // HEAD_DIM))


def _pair_heads(a, axis):
    shp = a.shape
    a = a.reshape(*shp[:axis], N_KV_HEADS, Q_REP, HEAD_DIM, *shp[axis + 1:])
    a = jnp.swapaxes(a, axis, axis + 1)
    return a.reshape(shp)


def _moe_plan(eid3, rank3, counts, n_tok, t_ffn):
    counts = counts.reshape(EXPERTS_PER_GROUP, N_EXPERT_GROUPS).T.reshape(N_EXPERTS).astype(jnp.int32)
    padded = (counts + t_ffn - 1) // t_ffn * t_ffn
    pends = jnp.cumsum(padded)
    pstarts = (pends - padded).astype(jnp.int32)
    experts = jnp.arange(N_EXPERTS, dtype=jnp.int32)
    dest3 = rank3 + jnp.sum(jnp.where(eid3[..., None] == experts, pstarts, 0), axis=-1)
    n_rows = -(-(n_tok * TOP_K) // t_ffn) * t_ffn + N_EXPERTS * t_ffn
    n_tiles = n_rows // t_ffn
    tile_row0 = jnp.arange(n_tiles, dtype=jnp.int32) * t_ffn
    tile_expert = jnp.minimum(jnp.sum(pends[None, :] <= tile_row0[:, None], axis=1), N_EXPERTS - 1)
    n_used = (pends[-1] // t_ffn).astype(jnp.int32).reshape(1)
    nonempty = counts > 0
    later = (experts[None, :] > experts[:, None]) & nonempty[None, :]
    e_next = jnp.min(jnp.where(later, experts[None, :], N_EXPERTS), axis=1)
    e_next = jnp.where(e_next < N_EXPERTS, e_next, -1).astype(jnp.int32)
    e_ord = (jnp.cumsum(nonempty.astype(jnp.int32)) - 1).astype(jnp.int32)
    zlo = jnp.where(nonempty, pends - t_ffn, -1).astype(jnp.int32)
    plan = (tile_expert.astype(jnp.int32), n_used, e_next, e_ord)
    return dest3.astype(jnp.int32), plan, zlo, n_rows


def kernel(x, c, ctx, c_ctx, w_ada, b_ada, w_in, w_pool_grp, pool_scale, w_pool_br, w_attn_br, attn_sink,
           w_o, ln1_g, ln1_b, w_router, router_bias, w_exp_gate, w_exp_up, w_exp_down, ln2_g, ln2_b):
    bsz, seq, d = x.shape
    assert d == ROW_CHUNKS * LANES
    c_len = ctx.shape[1]
    depth = w_in.shape[0]
    n_lat, n_ctx = bsz * seq, bsz * c_len
    alpha = (2 * depth) ** 0.25

    cond = jnp.zeros((8, d), F32).at[:bsz].set(c).at[bsz].set(c_ctx)
    ada = _ada_terms(cond, w_ada, b_ada)
    cos, sin = _rope_tables(seq)
    wr_t = w_router.reshape(d, N_EXPERT_GROUPS, EXPERTS_PER_GROUP).transpose(2, 1, 0).reshape(N_EXPERTS, d)
    wr_t = wr_t.astype(BF16)
    bias_col = router_bias.reshape(N_EXPERT_GROUPS, EXPERTS_PER_GROUP).T.reshape(N_EXPERTS, 1).astype(F32)

    xl = x.reshape(n_lat, d)
    xc = ctx.reshape(n_ctx, d)
    for l in range(depth):
        ctx_out = l < depth - 1
        mods = ada[l].reshape(8, 1, 6 * d)
        w_l = w_in[l]
        w_inp = jnp.concatenate([w_l[:, :COL_Q], _pair_heads(w_l[:, COL_Q:COL_K], 1), w_l[:, COL_K:]],
                                axis=1).astype(BF16)
        sink_col = jnp.broadcast_to(attn_sink[l].reshape(N_KV_HEADS, Q_REP, 1, 1),
                                    (N_KV_HEADS, Q_REP, BLOCK, 1)).reshape(N_KV_HEADS, Q_REP * BLOCK, 1)
        sink_col = (sink_col * LOG2E).astype(F32)
        mix_w = (w_pool_grp[l].astype(BF16), pool_scale[l].reshape(1, POOL_W), w_pool_br[l].astype(BF16),
                 _pair_heads(w_attn_br[l], 0).astype(BF16), w_o[l].astype(BF16),
                 ln1_g[l].reshape(1, d), ln1_b[l].reshape(1, d), wr_t)
        lng2, lnb2 = ln2_g[l].reshape(1, d), ln2_b[l].reshape(1, d)

        if ctx_out:
            uc, qc, kc, vc, gc = _inproj(xc, mods, w_inp, cos, sin, seq_len=c_len, mod_row=bsz, rope=False)
        else:
            kc, vc = _inproj(xc, mods, w_inp, cos, sin, seq_len=c_len, mod_row=bsz, rope=False, kv_only=True)
        kc3, vc3 = kc.reshape(bsz, c_len, KV_W), vc.reshape(bsz, c_len, KV_W)
        u, q, k, v, g = _inproj(xl, mods, w_inp, cos, sin, seq_len=seq, mod_row=None, rope=True)
        n_tok = n_lat + n_ctx if ctx_out else n_lat
        tokens = None
        if ctx_out:
            attn_c = _attention(qc, kc, vc, kc3, vc3, sink_col, seq_len=c_len, local=False)
            xc1, tokens, logits_c = _merge(uc, attn_c, gc, xc, mods, mix_w, seq_len=c_len, mod_row=bsz,
                                           alpha=alpha, h2_tokens=n_tok, h2_tok0=n_lat)
        attn = _attention(q, k, v, kc3, vc3, sink_col, seq_len=seq, local=True)
        x1, tokens, logits = _merge(u, attn, g, xl, mods, mix_w, seq_len=seq, mod_row=None, alpha=alpha,
                                    h2_tokens=n_tok, h2_tok0=0, h2_buf=tokens)
        if ctx_out:
            logits = jnp.concatenate([logits, logits_c], axis=1)

        eid3, wts, rank3, counts = _route(logits, bias_col)
        dest3, plan, zlo, n_rows = _moe_plan(eid3, rank3, counts[:, 0], n_tok, T_FFN)
        xs = _dispatch(tokens, dest3, zlo, n_rows, T_FFN)
        y = _expert_ffn(xs.reshape(n_rows * ROW_CHUNKS, LANES), plan, w_exp_gate, w_exp_up, w_exp_down, l, T_FFN)
        y3 = y.reshape(n_rows, ROW_CHUNKS, LANES)
        wts_t = wts.T
        xl = _combine(y3, dest3, wts_t, x1, mods, lng2, lnb2, tok0=0, seq_len=seq, mod_row=None, alpha=alpha)
        if ctx_out:
            xc = _combine(y3, dest3, wts_t, xc1, mods, lng2, lnb2, tok0=n_lat, seq_len=c_len, mod_row=bsz,
                          alpha=alpha)
    return xl.reshape(bsz, seq, d)
```

```python
import functools

import jax
import jax.numpy as jnp
from jax import lax
from jax.experimental import pallas as pl
from jax.experimental.pallas import tpu as pltpu

F32 = jnp.float32
BF16 = jnp.bfloat16

GRID_W = 64
POOL_WINDOWS = (2, 4, 8, 16)
POOL_GROUP_W = 128
POOL_W = 512
HEAD_DIM = 64
N_HEADS = 8
N_KV_HEADS = 2
Q_REP = N_HEADS // N_KV_HEADS
ATTN_W = N_HEADS * HEAD_DIM
KV_W = N_KV_HEADS * HEAD_DIM
BLOCK = 128
ROPE_BASE = 10000.0
COL_POOL = 0
COL_Q = COL_POOL + POOL_W
COL_K = COL_Q + ATTN_W
COL_V = COL_K + KV_W
COL_GATE = COL_V + KV_W
N_EXPERTS = 32
N_EXPERT_GROUPS = 8
EXPERTS_PER_GROUP = N_EXPERTS // N_EXPERT_GROUPS
TOP_K = 2
LN_EPS = 1e-6
LOG2E = 1.4426950408889634

LANES = 128
POOL_HALO = 8
ROW_CHUNKS = 8
VMEM_LIMIT = 56 * 1024 * 1024

TM_IN = 1024
TQ_ATT = 2048
TQ_MIX = 512
TN_TOK = 512
T_FFN = 256
DMA_UNROLL = 8


def _cparams(sem, flags=None):
    return pltpu.CompilerParams(dimension_semantics=sem, vmem_limit_bytes=VMEM_LIMIT, flags=flags)


def _layer_norm(x, eps=LN_EPS):
    mu = jnp.mean(x, axis=-1, keepdims=True)
    xc = x - mu
    var = jnp.mean(xc * xc, axis=-1, keepdims=True)
    return xc * lax.rsqrt(var + eps)


def _post_norm(x, gate, y, alpha):
    return _layer_norm(x + (gate * (1.0 / alpha)) * y, LN_EPS / (alpha * alpha))


def _dot(a, b):
    return jnp.dot(a, b, preferred_element_type=F32)


def _store_chunked(ref, val):
    t = val.shape[0]
    for s in range(ROW_CHUNKS):
        ref[pl.ds(s, t, stride=ROW_CHUNKS), :] = val[:, s * LANES:(s + 1) * LANES]


def _load_chunked(ref, t):
    return jnp.concatenate([ref[pl.ds(s, t, stride=ROW_CHUNKS), :] for s in range(ROW_CHUNKS)], axis=1)


def _ada_kernel(cond_ref, w_ref, b_ref, o_ref):
    s = cond_ref[...]
    s = s * jax.nn.sigmoid(s)
    o_ref[0] = _dot(s.astype(BF16), w_ref[0].astype(BF16)) + b_ref[0]


def _ada_terms(cond, w_ada, b_ada):
    depth, d, n6 = w_ada.shape
    tn = n6 // 4
    return pl.pallas_call(
        _ada_kernel,
        grid=(depth, n6 // tn),
        in_specs=[
            pl.BlockSpec((8, d), lambda l, j: (0, 0)),
            pl.BlockSpec((1, d, tn), lambda l, j: (l, 0, j)),
            pl.BlockSpec((1, 1, tn), lambda l, j: (l, 0, j)),
        ],
        out_specs=pl.BlockSpec((1, 8, tn), lambda l, j: (l, 0, j)),
        out_shape=jax.ShapeDtypeStruct((depth, 8, n6), F32),
        compiler_params=_cparams(("arbitrary", "arbitrary")),
        name="ada_terms",
    )(cond, w_ada, b_ada.reshape(depth, 1, n6))


def _rope(t, cos, sin):
    lane = lax.broadcasted_iota(jnp.int32, (1, LANES), 1)
    first = (lane % 32) < 16
    outs = []
    for j in range(t.shape[1] // LANES):
        tj = t[:, j * LANES:(j + 1) * LANES]
        partner = jnp.where(first, pltpu.roll(tj, LANES - 16, 1), pltpu.roll(tj, 16, 1))
        outs.append(tj * cos + partner * sin)
    return outs[0] if len(outs) == 1 else jnp.concatenate(outs, axis=1)


def _inproj_kernel(x_ref, mod_ref, w_ref, cos_ref, sin_ref, *out_refs, rope, kv_only):
    d = x_ref.shape[1]
    mod = mod_ref[0]
    shift, scale = mod[:, 0:d], mod[:, d:2 * d]
    h = (_layer_norm(x_ref[...]) * (1.0 + scale) + shift).astype(BF16)

    def proj(lo, hi):
        return _dot(h, w_ref[:, lo:hi])

    if kv_only:
        k_ref, v_ref = out_refs
    else:
        u_ref, q_ref, k_ref, v_ref, g_ref = out_refs
        u_ref[...] = proj(COL_POOL, COL_Q)
        q = proj(COL_Q, COL_K)
        if rope:
            q = _rope(q, cos_ref[...], sin_ref[...])
        q_ref[...] = (q * (LOG2E * HEAD_DIM ** -0.5)).astype(BF16)
        g_ref[...] = jax.nn.sigmoid(proj(COL_GATE, w_ref.shape[1]))
    kv = proj(COL_K, COL_GATE)
    k = kv[:, 0:KV_W]
    if rope:
        k = _rope(k, cos_ref[...], sin_ref[...])
    k_ref[...] = k.astype(BF16)
    v_ref[...] = kv[:, KV_W:2 * KV_W].astype(BF16)


def _inproj(x2d, mods, w_in, cos, sin, *, seq_len, mod_row, rope, kv_only=False):
    n, d = x2d.shape
    tm = min(TM_IN, seq_len)
    tps = seq_len // tm
    n_cols = w_in.shape[1]
    mod_map = (lambda i: (i // tps, 0, 0)) if mod_row is None else (lambda i: (mod_row, 0, 0))
    tab_map = (lambda i: (i % tps, 0)) if rope else (lambda i: (0, 0))
    row = lambda i: (i, 0)
    kv_shapes = [jax.ShapeDtypeStruct((n, KV_W), BF16)] * 2
    kv_specs = [pl.BlockSpec((tm, KV_W), row)] * 2
    if kv_only:
        out_shape, out_specs = kv_shapes, kv_specs
    else:
        out_shape = [jax.ShapeDtypeStruct((n, POOL_W), F32), jax.ShapeDtypeStruct((n, ATTN_W), BF16),
                     *kv_shapes, jax.ShapeDtypeStruct((n, n_cols - COL_GATE), F32)]
        out_specs = [pl.BlockSpec((tm, POOL_W), row), pl.BlockSpec((tm, ATTN_W), row),
                     *kv_specs, pl.BlockSpec((tm, n_cols - COL_GATE), row)]
    return pl.pallas_call(
        functools.partial(_inproj_kernel, rope=rope, kv_only=kv_only),
        grid=(n // tm,),
        in_specs=[
            pl.BlockSpec((tm, d), row),
            pl.BlockSpec((1, 1, mods.shape[2]), mod_map),
            pl.BlockSpec((d, n_cols), lambda i: (0, 0)),
            pl.BlockSpec((tm, LANES), tab_map),
            pl.BlockSpec((tm, LANES), tab_map),
        ],
        out_specs=out_specs,
        out_shape=out_shape,
        compiler_params=_cparams(("parallel",)),
        name="inproj_kv" if kv_only else "inproj",
    )(x2d, mods, w_in, cos, sin)


def _merge_kernel(u_prev_ref, u_ref, u_next_ref, attn_ref, g_ref, x_ref, mod_ref,
                  wgrp_ref, pscale_ref, wpool_ref, wattn_ref, wo_ref, lng_ref, lnb_ref, wr_ref,
                  *rest, seq_len, alpha):
    x1_ref, h2_ref, logit_ref, uext_ref = rest[-4:]
    tq, d = x_ref.shape
    tps = seq_len // tq
    t_in_seq = pl.program_id(0) % tps
    is_first = t_in_seq == 0
    is_last = t_in_seq == tps - 1

    top = 2 * POOL_HALO
    uext_ref[0:POOL_HALO, :] = jnp.zeros((POOL_HALO, POOL_W), F32)
    uext_ref[POOL_HALO:top, :] = jnp.where(is_first, 0.0, u_prev_ref[...])
    uext_ref[top:top + tq, :] = u_ref[...]
    uext_ref[top + tq:top + tq + POOL_HALO, :] = jnp.where(is_last, 0.0, u_next_ref[...])
    uext_ref[top + tq + POOL_HALO:, :] = jnp.zeros((BLOCK - top - POOL_HALO, POOL_W), F32)
    ue = uext_ref[...]
    ue_hi = ue.astype(BF16)
    ue_lo = (ue - ue_hi.astype(F32)).astype(BF16)
    row_t = lax.broadcasted_iota(jnp.int32, (BLOCK, 2 * BLOCK), 0)
    col_j = lax.broadcasted_iota(jnp.int32, (BLOCK, 2 * BLOCK), 1)
    pos = t_in_seq * tq + lax.broadcasted_iota(jnp.int32, (tq, 1), 0)
    pooled = []
    for gi, w in enumerate(POOL_WINDOWS):
        cols = slice(gi * POOL_GROUP_W, (gi + 1) * POOL_GROUP_W)
        first_j = row_t + top - w // 2
        band = jnp.where((col_j >= first_j) & (col_j < first_j + w), 1.0, 0.0).astype(BF16)
        sums = []
        for b in range(tq // BLOCK):
            win = slice(b * BLOCK, (b + 2) * BLOCK)
            sums.append(_dot(band, ue_hi[win, cols]) + _dot(band, ue_lo[win, cols]))
        acc = sums[0] if len(sums) == 1 else jnp.concatenate(sums, axis=0)
        lo = jnp.maximum(pos - w // 2, 0)
        hi = jnp.minimum(pos - w // 2 + w - 1, seq_len - 1)
        mean = acc / (hi - lo + 1).astype(F32)
        pg = (mean - u_ref[:, cols]).astype(BF16)
        pooled.append(_dot(pg, wgrp_ref[gi]))
    pool_lat = jnp.concatenate(pooled, axis=1) * pscale_ref[...]
    pool_proj = _dot(pool_lat.astype(BF16), wpool_ref[...])
    attn_proj = _dot(attn_ref[...], wattn_ref[...])

    gates = g_ref[...]
    merged = gates[:, 0:d] * pool_proj + gates[:, d:2 * d] * attn_proj
    y = _dot(merged.astype(BF16), wo_ref[...])
    mod = mod_ref[0]
    g1 = mod[:, 2 * d:3 * d]
    sh2, sc2 = mod[:, 3 * d:4 * d], mod[:, 4 * d:5 * d]
    x1 = _post_norm(x_ref[...], g1, y, alpha) * lng_ref[...] + lnb_ref[...]
    x1_ref[...] = x1
    h2 = _layer_norm(x1) * (1.0 + sc2) + sh2
    _store_chunked(h2_ref, h2)
    logit_ref[...] = lax.dot_general(wr_ref[...], h2.astype(BF16), (((1,), (1,)), ((), ())),
                                     preferred_element_type=F32)


def _attn_kernel(q_ref, k_prev_ref, k_ref, k_next_ref, v_prev_ref, v_ref, v_next_ref, kc_ref, vc_ref,
                 sink_ref, attn_ref, s0_ref, s1_ref, kg_ref, vg_ref, *, seq_len, local):
    tq = q_ref.shape[0]
    nb = tq // BLOCK
    tps = seq_len // tq
    t_in_seq = pl.program_id(0) % tps
    is_first = t_in_seq == 0
    is_last = t_in_seq == tps - 1
    lane = lax.broadcasted_iota(jnp.int32, (1, LANES), 1)
    lo_half = lane < HEAD_DIM
    one = jnp.ones((), BF16)
    kc = kc_ref[0]
    vc = vc_ref[0]
    kc_g = [jnp.where(lo_half, kc, 0), jnp.where(lo_half, 0, kc)]
    vc_g = [jnp.where(lo_half, vc, one), jnp.where(lo_half, one, vc)]
    neg = jnp.float32(-jnp.inf)
    if local:
        k_ext = jnp.concatenate([k_prev_ref[...], k_ref[...], k_next_ref[...]], axis=0)
        v_ext = jnp.concatenate([v_prev_ref[...], v_ref[...], v_next_ref[...]], axis=0)
        kg_ref[0] = jnp.where(lo_half, k_ext, 0)
        kg_ref[1] = jnp.where(lo_half, 0, k_ext)
        vg_ref[0] = jnp.where(lo_half, v_ext, one)
        vg_ref[1] = jnp.where(lo_half, one, v_ext)
        qq = lax.broadcasted_iota(jnp.int32, (Q_REP * BLOCK, BLOCK), 0) % BLOCK
        kk = lax.broadcasted_iota(jnp.int32, (Q_REP * BLOCK, BLOCK), 1)
        mask_prev = jnp.where(kk >= qq, 0.0, neg)
        mask_next = jnp.where(kk <= qq, 0.0, neg)

    def block_rows(b):
        return pl.ds(b * BLOCK, BLOCK) if isinstance(b, int) else pl.ds(pl.multiple_of(b * BLOCK, BLOCK), BLOCK)

    def band_keys(b):
        start = b * BLOCK if isinstance(b, int) else pl.multiple_of(b * BLOCK, BLOCK)
        return pl.ds(start, 3 * BLOCK)

    def stage_a(b, s_buf, first_block, last_block):
        q_st = jnp.concatenate([q_ref[block_rows(b), c * LANES:(c + 1) * LANES] for c in range(Q_REP)], axis=0)
        for g in range(N_KV_HEADS):
            k_all = jnp.concatenate([kg_ref[g, band_keys(b), :], kc_g[g]], axis=0) if local else kc_g[g]
            s = lax.dot_general(q_st, k_all, (((1,), (1,)), ((), ())), preferred_element_type=F32)
            if local:
                m_prev = jnp.where(first_block, neg, mask_prev)
                m_next = jnp.where(last_block, neg, mask_next)
                s = jnp.concatenate([s[:, 0:BLOCK] + m_prev, s[:, BLOCK:2 * BLOCK],
                                     s[:, 2 * BLOCK:3 * BLOCK] + m_next, s[:, 3 * BLOCK:]], axis=1)
            s_buf[g] = s

    def stage_b(b, s_buf):
        pv, sink_w = [], []
        for g in range(N_KV_HEADS):
            v_all = jnp.concatenate([vg_ref[g, band_keys(b), :], vc_g[g]], axis=0) if local else vc_g[g]
            s = s_buf[g]
            sk = sink_ref[g]
            m = jnp.maximum(jnp.max(s, axis=-1, keepdims=True), sk)
            p = jnp.exp2((s - m).astype(BF16))
            pv.append(_dot(p, v_all))
            sink_w.append(jnp.exp2(sk - m))
        num = jnp.where(lo_half, pv[0], pv[1])
        den = pltpu.roll(jnp.where(lo_half, pv[1], pv[0]), HEAD_DIM, 1) + jnp.where(lo_half, sink_w[0], sink_w[1])
        o = num / den
        for c in range(Q_REP):
            attn_ref[block_rows(b), c * LANES:(c + 1) * LANES] = o[c * BLOCK:(c + 1) * BLOCK].astype(BF16)

    stage_a(0, s0_ref, is_first, False)

    def body(j, carry):
        stage_a(2 * j + 1, s1_ref, False, False)
        stage_b(2 * j, s0_ref)
        stage_a(2 * j + 2, s0_ref, False, False)
        stage_b(2 * j + 1, s1_ref)
        return carry

    lax.fori_loop(0, nb // 2 - 1, body, 0)
    stage_a(nb - 1, s1_ref, False, is_last)
    stage_b(nb - 2, s0_ref)
    stage_b(nb - 1, s1_ref)


def _attention(q, k, v, kc, vc, sink_col, *, seq_len, local):
    n = q.shape[0]
    tq = min(TQ_ATT, seq_len)
    tps = seq_len // tq
    kb = tq // BLOCK
    n_kb = n // BLOCK
    c_len = kc.shape[1]
    n_keys = (3 * BLOCK if local else 0) + c_len
    row = lambda i: (i, 0)
    ctx_map = lambda i: (i // tps, 0, 0)
    kv_prev = pl.BlockSpec((BLOCK, KV_W), lambda i: (jnp.maximum(i * kb - 1, 0), 0))
    kv_cur = pl.BlockSpec((tq, KV_W), row)
    kv_next = pl.BlockSpec((BLOCK, KV_W), lambda i: (jnp.minimum((i + 1) * kb, n_kb - 1), 0))
    return pl.pallas_call(
        functools.partial(_attn_kernel, seq_len=seq_len, local=local),
        grid=(n // tq,),
        in_specs=[
            pl.BlockSpec((tq, ATTN_W), row),
            kv_prev, kv_cur, kv_next, kv_prev, kv_cur, kv_next,
            pl.BlockSpec((1, c_len, KV_W), ctx_map), pl.BlockSpec((1, c_len, KV_W), ctx_map),
            pl.BlockSpec(sink_col.shape, lambda i: (0, 0, 0)),
        ],
        out_specs=pl.BlockSpec((tq, ATTN_W), row),
        out_shape=jax.ShapeDtypeStruct((n, ATTN_W), BF16),
        scratch_shapes=[pltpu.VMEM((N_KV_HEADS, Q_REP * BLOCK, n_keys), F32),
                        pltpu.VMEM((N_KV_HEADS, Q_REP * BLOCK, n_keys), F32),
                        pltpu.VMEM((N_KV_HEADS, tq + 2 * BLOCK, KV_W), BF16),
                        pltpu.VMEM((N_KV_HEADS, tq + 2 * BLOCK, KV_W), BF16)],
        compiler_params=_cparams(("parallel",)),
        name="attention" if local else "attention_ctx",
    )(q, k, k, k, v, v, v, kc, vc, sink_col)


def _merge(u, attn, gates, x2d, mods, wts, *, seq_len, mod_row, alpha, h2_tokens, h2_tok0, h2_buf=None):
    n, d = x2d.shape
    tq = min(TQ_MIX, seq_len)
    h2_blk0 = h2_tok0 // tq
    tps = seq_len // tq
    hb = tq // POOL_HALO
    n_hb = n // POOL_HALO
    row = lambda i: (i, 0)
    const2 = lambda i: (0, 0)
    const3 = lambda i: (0, 0, 0)
    mod_map = (lambda i: (i // tps, 0, 0)) if mod_row is None else (lambda i: (mod_row, 0, 0))
    u_prev = pl.BlockSpec((POOL_HALO, POOL_W), lambda i: (jnp.maximum(i * hb - 1, 0), 0))
    u_next = pl.BlockSpec((POOL_HALO, POOL_W), lambda i: (jnp.minimum((i + 1) * hb, n_hb - 1), 0))
    wgrp, pscale, wpool, wattn, wo, lng, lnb, wr_t = wts
    operands = [u, u, u, attn, gates, x2d, mods, wgrp, pscale, wpool, wattn, wo, lng, lnb, wr_t]
    alias_specs, aliases = [], {}
    if h2_buf is not None:
        alias_specs = [pl.BlockSpec(memory_space=pl.ANY)]
        aliases = {len(operands): 1}
        operands.append(h2_buf)
    return pl.pallas_call(
        functools.partial(_merge_kernel, seq_len=seq_len, alpha=alpha),
        grid=(n // tq,),
        input_output_aliases=aliases,
        in_specs=[
            u_prev, pl.BlockSpec((tq, POOL_W), row), u_next,
            pl.BlockSpec((tq, ATTN_W), row),
            pl.BlockSpec((tq, 2 * d), row),
            pl.BlockSpec((tq, d), row),
            pl.BlockSpec((1, 1, mods.shape[2]), mod_map),
            pl.BlockSpec(wgrp.shape, const3), pl.BlockSpec(pscale.shape, const2),
            pl.BlockSpec(wpool.shape, const2), pl.BlockSpec(wattn.shape, const2),
            pl.BlockSpec(wo.shape, const2),
            pl.BlockSpec(lng.shape, const2), pl.BlockSpec(lnb.shape, const2),
            pl.BlockSpec(wr_t.shape, const2),
            *alias_specs,
        ],
        out_specs=[pl.BlockSpec((tq, d), row),
                   pl.BlockSpec((tq * ROW_CHUNKS, LANES), lambda i: (h2_blk0 + i, 0)),
                   pl.BlockSpec((N_EXPERTS, tq), lambda i: (0, i))],
        out_shape=[jax.ShapeDtypeStruct((n, d), F32),
                   jax.ShapeDtypeStruct((h2_tokens * ROW_CHUNKS, LANES), F32),
                   jax.ShapeDtypeStruct((N_EXPERTS, n), F32)],
        scratch_shapes=[pltpu.VMEM((tq + BLOCK, POOL_W), F32)],
        compiler_params=_cparams(("parallel",)),
        name="merge",
    )(*operands)


def _route_kernel(logit_ref, bias_ref, eid_ref, wts_ref, rank_ref, cnt_ref, base_ref):
    tn = logit_ref.shape[1]
    ng, epg = N_EXPERT_GROUPS, EXPERTS_PER_GROUP

    @pl.when(pl.program_id(0) == 0)
    def _():
        base_ref[...] = jnp.zeros_like(base_ref)

    scores = jax.nn.sigmoid(logit_ref[...])
    biased = scores + bias_ref[...]
    bj = [biased[j * ng:(j + 1) * ng] for j in range(epg)]
    sj = [scores[j * ng:(j + 1) * ng] for j in range(epg)]
    hi01, lo01 = jnp.maximum(bj[0], bj[1]), jnp.minimum(bj[0], bj[1])
    hi23, lo23 = jnp.maximum(bj[2], bj[3]), jnp.minimum(bj[2], bj[3])
    gscore = jnp.maximum(hi01, hi23) + jnp.maximum(jnp.minimum(hi01, hi23), jnp.maximum(lo01, lo23))
    giota = lax.broadcasted_iota(jnp.int32, (ng, tn), 0)
    gmax = jnp.max(gscore, axis=0, keepdims=True)
    g_first = jnp.min(jnp.where(gscore == gmax, giota.astype(F32), float(ng)), axis=0, keepdims=True)
    g_sel = g_first.astype(jnp.int32)
    in_g = giota == g_sel
    vb = [jnp.sum(jnp.where(in_g, b, 0.0), axis=0, keepdims=True) for b in bj]
    vs = [jnp.sum(jnp.where(in_g, s, 0.0), axis=0, keepdims=True) for s in sj]

    def first_best(vals):
        best = functools.reduce(jnp.maximum, vals)
        idx = jnp.full(best.shape, epg - 1, jnp.int32)
        for j in range(epg - 2, -1, -1):
            idx = jnp.where(vals[j] == best, j, idx)
        return idx

    def pick(vals, idx):
        out = vals[epg - 1]
        for j in range(epg - 2, -1, -1):
            out = jnp.where(idx == j, vals[j], out)
        return out

    l1 = first_best(vb)
    l2 = first_best([jnp.where(l1 == j, -jnp.inf, vb[j]) for j in range(epg)])
    w1, w2 = pick(vs, l1), pick(vs, l2)
    wsum = w1 + w2
    eid_ref[0, 0:1, :] = g_sel * epg + l1
    eid_ref[0, 1:2, :] = g_sel * epg + l2
    wts_ref[0:1, :] = w1 / wsum
    wts_ref[1:2, :] = w2 / wsum

    r1, r2 = l1 * ng + g_sel, l2 * ng + g_sel
    riota = lax.broadcasted_iota(jnp.int32, (N_EXPERTS, tn), 0)
    hit1, hit2 = riota == r1, riota == r2
    onehot = jnp.where(hit1 | hit2, 1.0, 0.0)
    before = lax.broadcasted_iota(jnp.int32, (tn, tn), 0) < lax.broadcasted_iota(jnp.int32, (tn, tn), 1)
    prefix = _dot(onehot.astype(BF16), jnp.where(before, 1.0, 0.0).astype(BF16)) + base_ref[:, 0:1]
    rank_ref[0, 0:1, :] = jnp.sum(jnp.where(hit1, prefix, 0.0), axis=0, keepdims=True).astype(jnp.int32)
    rank_ref[0, 1:2, :] = jnp.sum(jnp.where(hit2, prefix, 0.0), axis=0, keepdims=True).astype(jnp.int32)
    base_ref[...] = base_ref[...] + jnp.sum(onehot, axis=1, keepdims=True)
    cnt_ref[...] = base_ref[...]


def _route(logits, bias_col):
    n = logits.shape[1]
    tn = TN_TOK
    steps = n // tn
    col = lambda i: (0, i)
    blk = lambda i: (i, 0, 0)
    return pl.pallas_call(
        _route_kernel,
        grid=(steps,),
        in_specs=[
            pl.BlockSpec((N_EXPERTS, tn), col),
            pl.BlockSpec((N_EXPERTS, 1), lambda i: (0, 0)),
        ],
        out_specs=[pl.BlockSpec((1, TOP_K, tn), blk), pl.BlockSpec((TOP_K, tn), col),
                   pl.BlockSpec((1, TOP_K, tn), blk), pl.BlockSpec((N_EXPERTS, LANES), lambda i: (0, 0))],
        out_shape=[jax.ShapeDtypeStruct((steps, TOP_K, tn), jnp.int32), jax.ShapeDtypeStruct((TOP_K, n), F32),
                   jax.ShapeDtypeStruct((steps, TOP_K, tn), jnp.int32),
                   jax.ShapeDtypeStruct((N_EXPERTS, LANES), F32)],
        scratch_shapes=[pltpu.VMEM((N_EXPERTS, LANES), F32)],
        compiler_params=_cparams(("arbitrary",)),
        name="route",
    )(logits, bias_col)


def _dispatch_kernel(dest_ref, zlo_ref, tok_ref, xs_hbm, zero_ref, zsem, sem, *, t_ffn):
    i = pl.program_id(0)
    tn = dest_ref.shape[2]

    @pl.when(i == 0)
    def _():
        zero_ref[...] = jnp.zeros_like(zero_ref)

        def zcopy(e):
            return pltpu.make_async_copy(zero_ref, xs_hbm.at[pl.ds(jnp.maximum(zlo_ref[e], 0), t_ffn)], zsem)

        def start(e, carry):
            @pl.when(zlo_ref[e] >= 0)
            def _():
                zcopy(e).start()
            return carry

        def wait(e, carry):
            @pl.when(zlo_ref[e] >= 0)
            def _():
                zcopy(e).wait()
            return carry

        lax.fori_loop(0, N_EXPERTS, start, 0)
        lax.fori_loop(0, N_EXPERTS, wait, 0)

    for n in range(tn):
        src = tok_ref.at[pl.ds(n * ROW_CHUNKS, ROW_CHUNKS)]
        for slot in range(TOP_K):
            pltpu.make_async_copy(src, xs_hbm.at[dest_ref[0, slot, n]], sem).start(priority=slot)
    for slot in range(TOP_K):
        pltpu.make_async_copy(xs_hbm.at[pl.ds(0, tn)], xs_hbm.at[pl.ds(0, tn)], sem).wait()


def _dispatch(tokens, dest3, zlo, n_rows, t_ffn):
    steps, _, tn = dest3.shape
    smem = pltpu.SMEM
    return pl.pallas_call(
        functools.partial(_dispatch_kernel, t_ffn=t_ffn),
        grid=(steps,),
        in_specs=[
            pl.BlockSpec((1, TOP_K, tn), lambda i: (i, 0, 0), memory_space=smem),
            pl.BlockSpec(memory_space=smem),
            pl.BlockSpec((tn * ROW_CHUNKS, LANES), lambda i: (i, 0)),
        ],
        out_specs=pl.BlockSpec(memory_space=pl.ANY),
        out_shape=jax.ShapeDtypeStruct((n_rows, ROW_CHUNKS, LANES), tokens.dtype),
        scratch_shapes=[pltpu.VMEM((t_ffn, ROW_CHUNKS, LANES), tokens.dtype), pltpu.SemaphoreType.DMA,
                        pltpu.SemaphoreType.DMA],
        compiler_params=_cparams(("arbitrary",)),
        name="dispatch",
    )(dest3, zlo, tokens)


def _ffn_kernel(texp_ref, nused_ref, enext_ref, eord_ref, x_ref, wg_hbm, wu_hbm, wd_hbm, y_ref,
                wgf_ref, wuf_ref, wdf_ref, wgb_ref, wub_ref, wdb_ref, wsem, *, layer):
    j = pl.program_id(0)
    e = texp_ref[j]
    active = j < nused_ref[0]
    first = (j == 0) | (e != texp_ref[jnp.maximum(j - 1, 0)])
    slot = eord_ref[e] % 2
    pairs = ((wg_hbm, wgf_ref), (wu_hbm, wuf_ref), (wd_hbm, wdf_ref))

    def wcopies(expert, s):
        return [pltpu.make_async_copy(hbm.at[layer, expert], buf.at[s], wsem.at[s, i])
                for i, (hbm, buf) in enumerate(pairs)]

    @pl.when(active & (j == 0))
    def _():
        for cp in wcopies(e, slot):
            cp.start()

    @pl.when(active & first)
    def _():
        nxt = enext_ref[e]

        @pl.when(nxt >= 0)
        def _():
            for cp in wcopies(nxt, 1 - slot):
                cp.start(priority=1)

        for cp in wcopies(e, slot):
            cp.wait()
        wgb_ref[...] = wgf_ref[slot].astype(BF16)
        wub_ref[...] = wuf_ref[slot].astype(BF16)
        wdb_ref[...] = wdf_ref[slot].astype(BF16)

    @pl.when(active)
    def _():
        x = _load_chunked(x_ref, x_ref.shape[0] // ROW_CHUNKS).astype(BF16)
        gate = _dot(x, wgb_ref[...])
        up = _dot(x, wub_ref[...])
        act = (gate * jax.nn.sigmoid(gate) * up).astype(BF16)
        _store_chunked(y_ref, _dot(act, wdb_ref[...]))


def _expert_ffn(xs, plan, w_gate, w_up, w_down, layer, t_ffn):
    tile_expert, n_used, e_next, e_ord = plan
    _, _, d, de = w_gate.shape
    n_rows = xs.shape[0] // ROW_CHUNKS
    n_tiles = n_rows // t_ffn
    rowmap = lambda j, te, nu, en, eo: (jnp.minimum(j, nu[0] - 1), 0)
    hbm = pl.BlockSpec(memory_space=pl.ANY)
    return pl.pallas_call(
        functools.partial(_ffn_kernel, layer=layer),
        grid_spec=pltpu.PrefetchScalarGridSpec(
            num_scalar_prefetch=4,
            grid=(n_tiles,),
            in_specs=[pl.BlockSpec((t_ffn * ROW_CHUNKS, LANES), rowmap), hbm, hbm, hbm],
            out_specs=pl.BlockSpec((t_ffn * ROW_CHUNKS, LANES), rowmap),
            scratch_shapes=[pltpu.VMEM((2, d, de), F32), pltpu.VMEM((2, d, de), F32), pltpu.VMEM((2, de, d), F32),
                            pltpu.VMEM((d, de), BF16), pltpu.VMEM((d, de), BF16), pltpu.VMEM((de, d), BF16),
                            pltpu.SemaphoreType.DMA((2, 3))],
        ),
        out_shape=jax.ShapeDtypeStruct((n_rows * ROW_CHUNKS, LANES), F32),
        compiler_params=_cparams(("arbitrary",)),
        name="expert_ffn",
    )(tile_expert, n_used, e_next, e_ord, xs, w_gate, w_up, w_down)


def _combine_kernel(dest_ref, dest_next_ref, y_hbm, wts_ref, x1_ref, mod_ref, lng_ref, lnb_ref, x2_ref,
                    buf_ref, sem, *, alpha):
    i = pl.program_id(0)
    n_steps = pl.num_programs(0)
    tc, d = x1_ref.shape

    def row_copy(idx_ref, parity, n, slot, rows):
        return pltpu.make_async_copy(y_hbm.at[idx_ref[0, slot, n]], buf_ref.at[parity * TOP_K + slot, rows],
                                     sem.at[parity])

    @pl.when(i == 0)
    def _():
        def body(n, carry):
            rows = pl.ds(pl.multiple_of(n * ROW_CHUNKS, ROW_CHUNKS), ROW_CHUNKS)
            for slot in range(TOP_K):
                row_copy(dest_ref, 0, n, slot, rows).start(priority=slot)
            return carry
        lax.fori_loop(0, tc, body, 0, unroll=DMA_UNROLL)

    for parity in range(2):
        @pl.when((i + 1 < n_steps) & ((i + 1) % 2 == parity))
        def _():
            for n in range(tc):
                for slot in range(TOP_K):
                    row_copy(dest_next_ref, parity, n, slot, pl.ds(n * ROW_CHUNKS, ROW_CHUNKS)).start(priority=slot)

    for slot in range(TOP_K):
        pltpu.make_async_copy(y_hbm.at[pl.ds(0, tc)], y_hbm.at[pl.ds(0, tc)], sem.at[i % 2]).wait()
    w = wts_ref[...]
    cur = (i % 2) * TOP_K
    f = w[:, 0:1] * _load_chunked(buf_ref.at[cur], tc) + w[:, 1:2] * _load_chunked(buf_ref.at[cur + 1], tc)
    g2 = mod_ref[0][:, 5 * d:6 * d]
    x2_ref[...] = _post_norm(x1_ref[...], g2, f, alpha) * lng_ref[...] + lnb_ref[...]


def _combine(y, dest3, wts_t, x1, mods, lng, lnb, *, tok0, seq_len, mod_row, alpha):
    n, d = x1.shape
    tc = dest3.shape[2]
    tps = max(seq_len // tc, 1)
    steps = n // tc
    blk0 = tok0 // tc
    mod_map = (lambda i: (i // tps, 0, 0)) if mod_row is None else (lambda i: (mod_row, 0, 0))
    cur = lambda i: (blk0 + i, 0, 0)
    nxt = lambda i: (blk0 + jnp.minimum(i + 1, steps - 1), 0, 0)
    smem = pltpu.SMEM
    return pl.pallas_call(
        functools.partial(_combine_kernel, alpha=alpha),
        grid=(steps,),
        in_specs=[
            pl.BlockSpec((1, TOP_K, tc), cur, memory_space=smem),
            pl.BlockSpec((1, TOP_K, tc), nxt, memory_space=smem),
            pl.BlockSpec(memory_space=pl.ANY),
            pl.BlockSpec((tc, TOP_K), lambda i: (blk0 + i, 0)),
            pl.BlockSpec((tc, d), lambda i: (i, 0)),
            pl.BlockSpec((1, 1, mods.shape[2]), mod_map),
            pl.BlockSpec(lng.shape, lambda i: (0, 0)),
            pl.BlockSpec(lnb.shape, lambda i: (0, 0)),
        ],
        out_specs=pl.BlockSpec((tc, d), lambda i: (i, 0)),
        out_shape=jax.ShapeDtypeStruct((n, d), F32),
        scratch_shapes=[pltpu.VMEM((2 * TOP_K, tc * ROW_CHUNKS, LANES), F32), pltpu.SemaphoreType.DMA((2,))],
        compiler_params=_cparams(("arbitrary",)),
        name="combine",
    )(dest3, dest3, y, wts_t, x1, mods, lng, lnb)


def _rope_tables(seq_len):
    t = jnp.arange(seq_len)
    row = (t // GRID_W).astype(F32)
    col = (t % GRID_W).astype(F32)
    half = HEAD_DIM // 2
    inv_freq = ROPE_BASE ** (-jnp.arange(0, half, 2, dtype=F32) / half)
    ang_r, ang_c = row[:, None] * inv_freq, col[:, None] * inv_freq
    cr, sr, cc, sc = jnp.cos(ang_r), jnp.sin(ang_r), jnp.cos(ang_c), jnp.sin(ang_c)
    cos = jnp.concatenate([cr, cr, cc, cc], axis=1)
    sin = jnp.concatenate([-sr, sr, -sc, sc], axis=1)
    return jnp.tile(cos, (1, LANES // HEAD_DIM)), jnp.tile(sin, (1, LANES // HEAD_DIM))


def _pair_heads(a, axis):
    shp = a.shape
    a = a.reshape(*shp[:axis], N_KV_HEADS, Q_REP, HEAD_DIM, *shp[axis + 1:])
    a = jnp.swapaxes(a, axis, axis + 1)
    return a.reshape(shp)


def _moe_plan(eid3, rank3, counts, n_tok, t_ffn):
    counts = counts.reshape(EXPERTS_PER_GROUP, N_EXPERT_GROUPS).T.reshape(N_EXPERTS).astype(jnp.int32)
    padded = (counts + t_ffn - 1) // t_ffn * t_ffn
    pends = jnp.cumsum(padded)
    pstarts = (pends - padded).astype(jnp.int32)
    experts = jnp.arange(N_EXPERTS, dtype=jnp.int32)
    dest3 = rank3 + jnp.sum(jnp.where(eid3[..., None] == experts, pstarts, 0), axis=-1)
    n_rows = -(-(n_tok * TOP_K) // t_ffn) * t_ffn + N_EXPERTS * t_ffn
    n_tiles = n_rows // t_ffn
    tile_row0 = jnp.arange(n_tiles, dtype=jnp.int32) * t_ffn
    tile_expert = jnp.minimum(jnp.sum(pends[None, :] <= tile_row0[:, None], axis=1), N_EXPERTS - 1)
    n_used = (pends[-1] // t_ffn).astype(jnp.int32).reshape(1)
    nonempty = counts > 0
    later = (experts[None, :] > experts[:, None]) & nonempty[None, :]
    e_next = jnp.min(jnp.where(later, experts[None, :], N_EXPERTS), axis=1)
    e_next = jnp.where(e_next < N_EXPERTS, e_next, -1).astype(jnp.int32)
    e_ord = (jnp.cumsum(nonempty.astype(jnp.int32)) - 1).astype(jnp.int32)
    zlo = jnp.where(nonempty, pends - t_ffn, -1).astype(jnp.int32)
    plan = (tile_expert.astype(jnp.int32), n_used, e_next, e_ord)
    return dest3.astype(jnp.int32), plan, zlo, n_rows


def kernel(x, c, ctx, c_ctx, w_ada, b_ada, w_in, w_pool_grp, pool_scale, w_pool_br, w_attn_br, attn_sink,
           w_o, ln1_g, ln1_b, w_router, router_bias, w_exp_gate, w_exp_up, w_exp_down, ln2_g, ln2_b):
    bsz, seq, d = x.shape
    assert d == ROW_CHUNKS * LANES
    c_len = ctx.shape[1]
    depth = w_in.shape[0]
    n_lat, n_ctx = bsz * seq, bsz * c_len
    alpha = (2 * depth) ** 0.25

    cond = jnp.zeros((8, d), F32).at[:bsz].set(c).at[bsz].set(c_ctx)
    ada = _ada_terms(cond, w_ada, b_ada)
    cos, sin = _rope_tables(seq)
    wr_t = w_router.reshape(d, N_EXPERT_GROUPS, EXPERTS_PER_GROUP).transpose(2, 1, 0).reshape(N_EXPERTS, d)
    wr_t = wr_t.astype(BF16)
    bias_col = router_bias.reshape(N_EXPERT_GROUPS, EXPERTS_PER_GROUP).T.reshape(N_EXPERTS, 1).astype(F32)

    xl = x.reshape(n_lat, d)
    xc = ctx.reshape(n_ctx, d)
    for l in range(depth):
        ctx_out = l < depth - 1
        mods = ada[l].reshape(8, 1, 6 * d)
        w_l = w_in[l]
        w_inp = jnp.concatenate([w_l[:, :COL_Q], _pair_heads(w_l[:, COL_Q:COL_K], 1), w_l[:, COL_K:]],
                                axis=1).astype(BF16)
        sink_col = jnp.broadcast_to(attn_sink[l].reshape(N_KV_HEADS, Q_REP, 1, 1),
                                    (N_KV_HEADS, Q_REP, BLOCK, 1)).reshape(N_KV_HEADS, Q_REP * BLOCK, 1)
        sink_col = (sink_col * LOG2E).astype(F32)
        mix_w = (w_pool_grp[l].astype(BF16), pool_scale[l].reshape(1, POOL_W), w_pool_br[l].astype(BF16),
                 _pair_heads(w_attn_br[l], 0).astype(BF16), w_o[l].astype(BF16),
                 ln1_g[l].reshape(1, d), ln1_b[l].reshape(1, d), wr_t)
        lng2, lnb2 = ln2_g[l].reshape(1, d), ln2_b[l].reshape(1, d)

        if ctx_out:
            uc, qc, kc, vc, gc = _inproj(xc, mods, w_inp, cos, sin, seq_len=c_len, mod_row=bsz, rope=False)
        else:
            kc, vc = _inproj(xc, mods, w_inp, cos, sin, seq_len=c_len, mod_row=bsz, rope=False, kv_only=True)
        kc3, vc3 = kc.reshape(bsz, c_len, KV_W), vc.reshape(bsz, c_len, KV_W)
        u, q, k, v, g = _inproj(xl, mods, w_inp, cos, sin, seq_len=seq, mod_row=None, rope=True)
        n_tok = n_lat + n_ctx if ctx_out else n_lat
        tokens = None
        if ctx_out:
            attn_c = _attention(qc, kc, vc, kc3, vc3, sink_col, seq_len=c_len, local=False)
            xc1, tokens, logits_c = _merge(uc, attn_c, gc, xc, mods, mix_w, seq_len=c_len, mod_row=bsz,
                                           alpha=alpha, h2_tokens=n_tok, h2_tok0=n_lat)
        attn = _attention(q, k, v, kc3, vc3, sink_col, seq_len=seq, local=True)
        x1, tokens, logits = _merge(u, attn, g, xl, mods, mix_w, seq_len=seq, mod_row=None, alpha=alpha,
                                    h2_tokens=n_tok, h2_tok0=0, h2_buf=tokens)
        if ctx_out:
            logits = jnp.concatenate([logits, logits_c], axis=1)

        eid3, wts, rank3, counts = _route(logits, bias_col)
        dest3, plan, zlo, n_rows = _moe_plan(eid3, rank3, counts[:, 0], n_tok, T_FFN)
        xs = _dispatch(tokens, dest3, zlo, n_rows, T_FFN)
        y = _expert_ffn(xs.reshape(n_rows * ROW_CHUNKS, LANES), plan, w_exp_gate, w_exp_up, w_exp_down, l, T_FFN)
        y3 = y.reshape(n_rows, ROW_CHUNKS, LANES)
        wts_t = wts.T
        xl = _combine(y3, dest3, wts_t, x1, mods, lng2, lnb2, tok0=0, seq_len=seq, mod_row=None, alpha=alpha)
        if ctx_out:
            xc = _combine(y3, dest3, wts_t, xc1, mods, lng2, lnb2, tok0=n_lat, seq_len=c_len, mod_row=bsz,
                          alpha=alpha)
    return xl.reshape(bsz, seq, d)
```

```python
import functools

import jax
import jax.numpy as jnp
from jax import lax
from jax.experimental import pallas as pl
from jax.experimental.pallas import tpu as pltpu

F32 = jnp.float32
BF16 = jnp.bfloat16

GRID_W = 64
POOL_WINDOWS = (2, 4, 8, 16)
POOL_GROUP_W = 128
POOL_W = 512
HEAD_DIM = 64
N_HEADS = 8
N_KV_HEADS = 2
Q_REP = N_HEADS // N_KV_HEADS
ATTN_W = N_HEADS * HEAD_DIM
KV_W = N_KV_HEADS * HEAD_DIM
BLOCK = 128
ROPE_BASE = 10000.0
COL_POOL = 0
COL_Q = COL_POOL + POOL_W
COL_K = COL_Q + ATTN_W
COL_V = COL_K + KV_W
COL_GATE = COL_V + KV_W
N_EXPERTS = 32
N_EXPERT_GROUPS = 8
EXPERTS_PER_GROUP = N_EXPERTS // N_EXPERT_GROUPS
TOP_K = 2
LN_EPS = 1e-6
LOG2E = 1.4426950408889634

LANES = 128
SUBLANES = 8
POOL_HALO = SUBLANES
ROW_CHUNKS = SUBLANES
VMEM_LIMIT = 56 * 1024 * 1024

ADA_COL_TILES = 4
TM_IN = 1024
TQ_ATT = 2048
TQ_MIX = 512
TN_TOK = 512
T_FFN = 256
DMA_UNROLL = 8


def _cparams(sem):
    return pltpu.CompilerParams(dimension_semantics=sem, vmem_limit_bytes=VMEM_LIMIT)


def _layer_norm(x, eps=LN_EPS):
    mu = jnp.mean(x, axis=-1, keepdims=True)
    xc = x - mu
    var = jnp.mean(xc * xc, axis=-1, keepdims=True)
    return xc * lax.rsqrt(var + eps)


def _post_norm(x, gate, y, alpha):
    return _layer_norm(x + (gate * (1.0 / alpha)) * y, LN_EPS / (alpha * alpha))


def _dot(a, b):
    return jnp.dot(a, b, preferred_element_type=F32)


def _store_chunked(ref, val):
    t = val.shape[0]
    for s in range(ROW_CHUNKS):
        ref[pl.ds(s, t, stride=ROW_CHUNKS), :] = val[:, s * LANES:(s + 1) * LANES]


def _load_chunked(ref, t):
    return jnp.concatenate([ref[pl.ds(s, t, stride=ROW_CHUNKS), :] for s in range(ROW_CHUNKS)], axis=1)


def _ada_kernel(cond_ref, w_ref, b_ref, o_ref):
    s = cond_ref[...]
    s = s * jax.nn.sigmoid(s)
    o_ref[0] = _dot(s.astype(BF16), w_ref[0].astype(BF16)) + b_ref[0]


def _ada_terms(cond, w_ada, b_ada):
    depth, d, n6 = w_ada.shape
    rows = cond.shape[0]
    tn = n6 // ADA_COL_TILES
    return pl.pallas_call(
        _ada_kernel,
        grid=(depth, ADA_COL_TILES),
        in_specs=[
            pl.BlockSpec((rows, d), lambda l, j: (0, 0)),
            pl.BlockSpec((1, d, tn), lambda l, j: (l, 0, j)),
            pl.BlockSpec((1, 1, tn), lambda l, j: (l, 0, j)),
        ],
        out_specs=pl.BlockSpec((1, rows, tn), lambda l, j: (l, 0, j)),
        out_shape=jax.ShapeDtypeStruct((depth, rows, n6), F32),
        compiler_params=_cparams(("arbitrary", "arbitrary")),
        name="ada_terms",
    )(cond, w_ada, b_ada.reshape(depth, 1, n6))


def _rope(t, cos, sin):
    lane = lax.broadcasted_iota(jnp.int32, (1, LANES), 1)
    first = (lane % 32) < 16
    outs = []
    for j in range(t.shape[1] // LANES):
        tj = t[:, j * LANES:(j + 1) * LANES]
        partner = jnp.where(first, pltpu.roll(tj, LANES - 16, 1), pltpu.roll(tj, 16, 1))
        outs.append(tj * cos + partner * sin)
    return outs[0] if len(outs) == 1 else jnp.concatenate(outs, axis=1)


def _inproj_kernel(x_ref, mod_ref, w_ref, cos_ref, sin_ref, *out_refs, rope, kv_only):
    d = x_ref.shape[1]
    mod = mod_ref[0]
    shift, scale = mod[:, 0:d], mod[:, d:2 * d]
    h = (_layer_norm(x_ref[...]) * (1.0 + scale) + shift).astype(BF16)

    def proj(lo, hi):
        return _dot(h, w_ref[:, lo:hi])

    if kv_only:
        k_ref, v_ref = out_refs
    else:
        u_ref, q_ref, k_ref, v_ref, g_ref = out_refs
        u_ref[...] = proj(COL_POOL, COL_Q)
        q = proj(COL_Q, COL_K)
        if rope:
            q = _rope(q, cos_ref[...], sin_ref[...])
        q_ref[...] = (q * (LOG2E * HEAD_DIM ** -0.5)).astype(BF16)
        g_ref[...] = jax.nn.sigmoid(proj(COL_GATE, w_ref.shape[1]))
    kv = proj(COL_K, COL_GATE)
    k = kv[:, 0:KV_W]
    if rope:
        k = _rope(k, cos_ref[...], sin_ref[...])
    k_ref[...] = k.astype(BF16)
    v_ref[...] = kv[:, KV_W:2 * KV_W].astype(BF16)


def _inproj(x2d, mods, w_in, cos, sin, *, seq_len, mod_row, rope, kv_only=False):
    n, d = x2d.shape
    tm = min(TM_IN, seq_len)
    tps = seq_len // tm
    n_cols = w_in.shape[1]
    mod_map = (lambda i: (i // tps, 0, 0)) if mod_row is None else (lambda i: (mod_row, 0, 0))
    tab_map = (lambda i: (i % tps, 0)) if rope else (lambda i: (0, 0))
    row = lambda i: (i, 0)
    kv_shapes = [jax.ShapeDtypeStruct((n, KV_W), BF16)] * 2
    kv_specs = [pl.BlockSpec((tm, KV_W), row)] * 2
    if kv_only:
        out_shape, out_specs = kv_shapes, kv_specs
    else:
        out_shape = [jax.ShapeDtypeStruct((n, POOL_W), F32), jax.ShapeDtypeStruct((n, ATTN_W), BF16),
                     *kv_shapes, jax.ShapeDtypeStruct((n, n_cols - COL_GATE), F32)]
        out_specs = [pl.BlockSpec((tm, POOL_W), row), pl.BlockSpec((tm, ATTN_W), row),
                     *kv_specs, pl.BlockSpec((tm, n_cols - COL_GATE), row)]
    return pl.pallas_call(
        functools.partial(_inproj_kernel, rope=rope, kv_only=kv_only),
        grid=(n // tm,),
        in_specs=[
            pl.BlockSpec((tm, d), row),
            pl.BlockSpec((1, 1, mods.shape[2]), mod_map),
            pl.BlockSpec((d, n_cols), lambda i: (0, 0)),
            pl.BlockSpec((tm, LANES), tab_map),
            pl.BlockSpec((tm, LANES), tab_map),
        ],
        out_specs=out_specs,
        out_shape=out_shape,
        compiler_params=_cparams(("parallel",)),
        name="inproj_kv" if kv_only else "inproj",
    )(x2d, mods, w_in, cos, sin)


def _merge_kernel(u_prev_ref, u_ref, u_next_ref, attn_ref, g_ref, x_ref, mod_ref,
                  wgrp_ref, pscale_ref, wpool_ref, wattn_ref, wo_ref, lng_ref, lnb_ref, wr_ref,
                  *rest, seq_len, alpha):
    x1_ref, h2_ref, logit_ref, uext_ref = rest[-4:]
    tq, d = x_ref.shape
    tps = seq_len // tq
    t_in_seq = pl.program_id(0) % tps
    is_first = t_in_seq == 0
    is_last = t_in_seq == tps - 1

    top = 2 * POOL_HALO
    uext_ref[0:POOL_HALO, :] = jnp.zeros((POOL_HALO, POOL_W), F32)
    uext_ref[POOL_HALO:top, :] = jnp.where(is_first, 0.0, u_prev_ref[...])
    uext_ref[top:top + tq, :] = u_ref[...]
    uext_ref[top + tq:top + tq + POOL_HALO, :] = jnp.where(is_last, 0.0, u_next_ref[...])
    uext_ref[top + tq + POOL_HALO:, :] = jnp.zeros((BLOCK - top - POOL_HALO, POOL_W), F32)
    ue = uext_ref[...]
    ue_hi = ue.astype(BF16)
    ue_lo = (ue - ue_hi.astype(F32)).astype(BF16)
    row_t = lax.broadcasted_iota(jnp.int32, (BLOCK, 2 * BLOCK), 0)
    col_j = lax.broadcasted_iota(jnp.int32, (BLOCK, 2 * BLOCK), 1)
    pos = t_in_seq * tq + lax.broadcasted_iota(jnp.int32, (tq, 1), 0)
    pooled = []
    for gi, w in enumerate(POOL_WINDOWS):
        cols = slice(gi * POOL_GROUP_W, (gi + 1) * POOL_GROUP_W)
        first_j = row_t + top - w // 2
        band = jnp.where((col_j >= first_j) & (col_j < first_j + w), 1.0, 0.0).astype(BF16)
        sums = []
        for b in range(tq // BLOCK):
            win = slice(b * BLOCK, (b + 2) * BLOCK)
            sums.append(_dot(band, ue_hi[win, cols]) + _dot(band, ue_lo[win, cols]))
        acc = sums[0] if len(sums) == 1 else jnp.concatenate(sums, axis=0)
        lo = jnp.maximum(pos - w // 2, 0)
        hi = jnp.minimum(pos - w // 2 + w - 1, seq_len - 1)
        mean = acc / (hi - lo + 1).astype(F32)
        pg = (mean - u_ref[:, cols]).astype(BF16)
        pooled.append(_dot(pg, wgrp_ref[gi]))
    pool_lat = jnp.concatenate(pooled, axis=1) * pscale_ref[...]
    pool_proj = _dot(pool_lat.astype(BF16), wpool_ref[...])
    attn_proj = _dot(attn_ref[...], wattn_ref[...])

    gates = g_ref[...]
    merged = gates[:, 0:d] * pool_proj + gates[:, d:2 * d] * attn_proj
    y = _dot(merged.astype(BF16), wo_ref[...])
    mod = mod_ref[0]
    g1 = mod[:, 2 * d:3 * d]
    sh2, sc2 = mod[:, 3 * d:4 * d], mod[:, 4 * d:5 * d]
    x1 = _post_norm(x_ref[...], g1, y, alpha) * lng_ref[...] + lnb_ref[...]
    x1_ref[...] = x1
    h2 = _layer_norm(x1) * (1.0 + sc2) + sh2
    _store_chunked(h2_ref, h2)
    logit_ref[...] = lax.dot_general(wr_ref[...], h2.astype(BF16), (((1,), (1,)), ((), ())),
                                     preferred_element_type=F32)


def _attn_kernel(q_ref, k_prev_ref, k_ref, k_next_ref, v_prev_ref, v_ref, v_next_ref, kc_ref, vc_ref,
                 sink_ref, attn_ref, s0_ref, s1_ref, kg_ref, vg_ref, *, seq_len, local):
    tq = q_ref.shape[0]
    nb = tq // BLOCK
    tps = seq_len // tq
    t_in_seq = pl.program_id(0) % tps
    is_first = t_in_seq == 0
    is_last = t_in_seq == tps - 1
    lane = lax.broadcasted_iota(jnp.int32, (1, LANES), 1)
    lo_half = lane < HEAD_DIM
    one = jnp.ones((), BF16)
    kc = kc_ref[0]
    vc = vc_ref[0]
    kc_g = [jnp.where(lo_half, kc, 0), jnp.where(lo_half, 0, kc)]
    vc_g = [jnp.where(lo_half, vc, one), jnp.where(lo_half, one, vc)]
    neg = jnp.float32(-jnp.inf)
    if local:
        k_ext = jnp.concatenate([k_prev_ref[...], k_ref[...], k_next_ref[...]], axis=0)
        v_ext = jnp.concatenate([v_prev_ref[...], v_ref[...], v_next_ref[...]], axis=0)
        kg_ref[0] = jnp.where(lo_half, k_ext, 0)
        kg_ref[1] = jnp.where(lo_half, 0, k_ext)
        vg_ref[0] = jnp.where(lo_half, v_ext, one)
        vg_ref[1] = jnp.where(lo_half, one, v_ext)
        qq = lax.broadcasted_iota(jnp.int32, (Q_REP * BLOCK, BLOCK), 0) % BLOCK
        kk = lax.broadcasted_iota(jnp.int32, (Q_REP * BLOCK, BLOCK), 1)
        mask_prev = jnp.where(kk >= qq, 0.0, neg)
        mask_next = jnp.where(kk <= qq, 0.0, neg)

    def block_rows(b):
        return pl.ds(b * BLOCK, BLOCK) if isinstance(b, int) else pl.ds(pl.multiple_of(b * BLOCK, BLOCK), BLOCK)

    def band_keys(b):
        start = b * BLOCK if isinstance(b, int) else pl.multiple_of(b * BLOCK, BLOCK)
        return pl.ds(start, 3 * BLOCK)

    def stage_a(b, s_buf, first_block, last_block):
        q_st = jnp.concatenate([q_ref[block_rows(b), c * LANES:(c + 1) * LANES] for c in range(Q_REP)], axis=0)
        for g in range(N_KV_HEADS):
            k_all = jnp.concatenate([kg_ref[g, band_keys(b), :], kc_g[g]], axis=0) if local else kc_g[g]
            s = lax.dot_general(q_st, k_all, (((1,), (1,)), ((), ())), preferred_element_type=F32)
            if local:
                m_prev = jnp.where(first_block, neg, mask_prev)
                m_next = jnp.where(last_block, neg, mask_next)
                s = jnp.concatenate([s[:, 0:BLOCK] + m_prev, s[:, BLOCK:2 * BLOCK],
                                     s[:, 2 * BLOCK:3 * BLOCK] + m_next, s[:, 3 * BLOCK:]], axis=1)
            s_buf[g] = s

    def stage_b(b, s_buf):
        pv, sink_w = [], []
        for g in range(N_KV_HEADS):
            v_all = jnp.concatenate([vg_ref[g, band_keys(b), :], vc_g[g]], axis=0) if local else vc_g[g]
            s = s_buf[g]
            sk = sink_ref[g]
            m = jnp.maximum(jnp.max(s, axis=-1, keepdims=True), sk)
            p = jnp.exp2((s - m).astype(BF16))
            pv.append(_dot(p, v_all))
            sink_w.append(jnp.exp2(sk - m))
        num = jnp.where(lo_half, pv[0], pv[1])
        den = pltpu.roll(jnp.where(lo_half, pv[1], pv[0]), HEAD_DIM, 1) + jnp.where(lo_half, sink_w[0], sink_w[1])
        o = num / den
        for c in range(Q_REP):
            attn_ref[block_rows(b), c * LANES:(c + 1) * LANES] = o[c * BLOCK:(c + 1) * BLOCK].astype(BF16)

    stage_a(0, s0_ref, is_first, False)

    def body(j, carry):
        stage_a(2 * j + 1, s1_ref, False, False)
        stage_b(2 * j, s0_ref)
        stage_a(2 * j + 2, s0_ref, False, False)
        stage_b(2 * j + 1, s1_ref)
        return carry

    lax.fori_loop(0, nb // 2 - 1, body, 0)
    stage_a(nb - 1, s1_ref, False, is_last)
    stage_b(nb - 2, s0_ref)
    stage_b(nb - 1, s1_ref)


def _attention(q, k, v, kc, vc, sink_col, *, seq_len, local):
    n = q.shape[0]
    tq = min(TQ_ATT, seq_len)
    tps = seq_len // tq
    kb = tq // BLOCK
    n_kb = n // BLOCK
    c_len = kc.shape[1]
    n_keys = (3 * BLOCK if local else 0) + c_len
    row = lambda i: (i, 0)
    ctx_map = lambda i: (i // tps, 0, 0)
    kv_prev = pl.BlockSpec((BLOCK, KV_W), lambda i: (jnp.maximum(i * kb - 1, 0), 0))
    kv_cur = pl.BlockSpec((tq, KV_W), row)
    kv_next = pl.BlockSpec((BLOCK, KV_W), lambda i: (jnp.minimum((i + 1) * kb, n_kb - 1), 0))
    return pl.pallas_call(
        functools.partial(_attn_kernel, seq_len=seq_len, local=local),
        grid=(n // tq,),
        in_specs=[
            pl.BlockSpec((tq, ATTN_W), row),
            kv_prev, kv_cur, kv_next, kv_prev, kv_cur, kv_next,
            pl.BlockSpec((1, c_len, KV_W), ctx_map), pl.BlockSpec((1, c_len, KV_W), ctx_map),
            pl.BlockSpec(sink_col.shape, lambda i: (0, 0, 0)),
        ],
        out_specs=pl.BlockSpec((tq, ATTN_W), row),
        out_shape=jax.ShapeDtypeStruct((n, ATTN_W), BF16),
        scratch_shapes=[pltpu.VMEM((N_KV_HEADS, Q_REP * BLOCK, n_keys), F32),
                        pltpu.VMEM((N_KV_HEADS, Q_REP * BLOCK, n_keys), F32),
                        pltpu.VMEM((N_KV_HEADS, tq + 2 * BLOCK, KV_W), BF16),
                        pltpu.VMEM((N_KV_HEADS, tq + 2 * BLOCK, KV_W), BF16)],
        compiler_params=_cparams(("parallel",)),
        name="attention" if local else "attention_ctx",
    )(q, k, k, k, v, v, v, kc, vc, sink_col)


def _merge(u, attn, gates, x2d, mods, wts, *, seq_len, mod_row, alpha, h2_tokens, h2_tok0, h2_buf=None):
    n, d = x2d.shape
    tq = min(TQ_MIX, seq_len)
    h2_blk0 = h2_tok0 // tq
    tps = seq_len // tq
    hb = tq // POOL_HALO
    n_hb = n // POOL_HALO
    row = lambda i: (i, 0)
    const2 = lambda i: (0, 0)
    const3 = lambda i: (0, 0, 0)
    mod_map = (lambda i: (i // tps, 0, 0)) if mod_row is None else (lambda i: (mod_row, 0, 0))
    u_prev = pl.BlockSpec((POOL_HALO, POOL_W), lambda i: (jnp.maximum(i * hb - 1, 0), 0))
    u_next = pl.BlockSpec((POOL_HALO, POOL_W), lambda i: (jnp.minimum((i + 1) * hb, n_hb - 1), 0))
    wgrp, pscale, wpool, wattn, wo, lng, lnb, wr_t = wts
    operands = [u, u, u, attn, gates, x2d, mods, wgrp, pscale, wpool, wattn, wo, lng, lnb, wr_t]
    alias_specs, aliases = [], {}
    if h2_buf is not None:
        alias_specs = [pl.BlockSpec(memory_space=pl.ANY)]
        aliases = {len(operands): 1}
        operands.append(h2_buf)
    return pl.pallas_call(
        functools.partial(_merge_kernel, seq_len=seq_len, alpha=alpha),
        grid=(n // tq,),
        input_output_aliases=aliases,
        in_specs=[
            u_prev, pl.BlockSpec((tq, POOL_W), row), u_next,
            pl.BlockSpec((tq, ATTN_W), row),
            pl.BlockSpec((tq, 2 * d), row),
            pl.BlockSpec((tq, d), row),
            pl.BlockSpec((1, 1, mods.shape[2]), mod_map),
            pl.BlockSpec(wgrp.shape, const3), pl.BlockSpec(pscale.shape, const2),
            pl.BlockSpec(wpool.shape, const2), pl.BlockSpec(wattn.shape, const2),
            pl.BlockSpec(wo.shape, const2),
            pl.BlockSpec(lng.shape, const2), pl.BlockSpec(lnb.shape, const2),
            pl.BlockSpec(wr_t.shape, const2),
            *alias_specs,
        ],
        out_specs=[pl.BlockSpec((tq, d), row),
                   pl.BlockSpec((tq * ROW_CHUNKS, LANES), lambda i: (h2_blk0 + i, 0)),
                   pl.BlockSpec((N_EXPERTS, tq), lambda i: (0, i))],
        out_shape=[jax.ShapeDtypeStruct((n, d), F32),
                   jax.ShapeDtypeStruct((h2_tokens * ROW_CHUNKS, LANES), F32),
                   jax.ShapeDtypeStruct((N_EXPERTS, n), F32)],
        scratch_shapes=[pltpu.VMEM((tq + BLOCK, POOL_W), F32)],
        compiler_params=_cparams(("parallel",)),
        name="merge",
    )(*operands)


def _route_kernel(logit_ref, bias_ref, eid_ref, wts_ref, rank_ref, cnt_ref, base_ref):
    tn = logit_ref.shape[1]
    ng, epg = N_EXPERT_GROUPS, EXPERTS_PER_GROUP

    @pl.when(pl.program_id(0) == 0)
    def _():
        base_ref[...] = jnp.zeros_like(base_ref)

    scores = jax.nn.sigmoid(logit_ref[...])
    biased = scores + bias_ref[...]
    bj = [biased[j * ng:(j + 1) * ng] for j in range(epg)]
    sj = [scores[j * ng:(j + 1) * ng] for j in range(epg)]
    hi01, lo01 = jnp.maximum(bj[0], bj[1]), jnp.minimum(bj[0], bj[1])
    hi23, lo23 = jnp.maximum(bj[2], bj[3]), jnp.minimum(bj[2], bj[3])
    gscore = jnp.maximum(hi01, hi23) + jnp.maximum(jnp.minimum(hi01, hi23), jnp.maximum(lo01, lo23))
    giota = lax.broadcasted_iota(jnp.int32, (ng, tn), 0)
    gmax = jnp.max(gscore, axis=0, keepdims=True)
    g_first = jnp.min(jnp.where(gscore == gmax, giota.astype(F32), float(ng)), axis=0, keepdims=True)
    g_sel = g_first.astype(jnp.int32)
    in_g = giota == g_sel
    vb = [jnp.sum(jnp.where(in_g, b, 0.0), axis=0, keepdims=True) for b in bj]
    vs = [jnp.sum(jnp.where(in_g, s, 0.0), axis=0, keepdims=True) for s in sj]

    def first_best(vals):
        best = functools.reduce(jnp.maximum, vals)
        idx = jnp.full(best.shape, epg - 1, jnp.int32)
        for j in range(epg - 2, -1, -1):
            idx = jnp.where(vals[j] == best, j, idx)
        return idx

    def pick(vals, idx):
        out = vals[epg - 1]
        for j in range(epg - 2, -1, -1):
            out = jnp.where(idx == j, vals[j], out)
        return out

    l1 = first_best(vb)
    l2 = first_best([jnp.where(l1 == j, -jnp.inf, vb[j]) for j in range(epg)])
    w1, w2 = pick(vs, l1), pick(vs, l2)
    wsum = w1 + w2
    eid_ref[0, 0:1, :] = g_sel * epg + l1
    eid_ref[0, 1:2, :] = g_sel * epg + l2
    wts_ref[0:1, :] = w1 / wsum
    wts_ref[1:2, :] = w2 / wsum

    r1, r2 = l1 * ng + g_sel, l2 * ng + g_sel
    riota = lax.broadcasted_iota(jnp.int32, (N_EXPERTS, tn), 0)
    hit1, hit2 = riota == r1, riota == r2
    onehot = jnp.where(hit1 | hit2, 1.0, 0.0)
    before = lax.broadcasted_iota(jnp.int32, (tn, tn), 0) < lax.broadcasted_iota(jnp.int32, (tn, tn), 1)
    prefix = _dot(onehot.astype(BF16), jnp.where(before, 1.0, 0.0).astype(BF16)) + base_ref[:, 0:1]
    rank_ref[0, 0:1, :] = jnp.sum(jnp.where(hit1, prefix, 0.0), axis=0, keepdims=True).astype(jnp.int32)
    rank_ref[0, 1:2, :] = jnp.sum(jnp.where(hit2, prefix, 0.0), axis=0, keepdims=True).astype(jnp.int32)
    base_ref[...] = base_ref[...] + jnp.sum(onehot, axis=1, keepdims=True)
    cnt_ref[...] = base_ref[...]


def _route(logits, bias_col):
    n = logits.shape[1]
    tn = TN_TOK
    steps = n // tn
    col = lambda i: (0, i)
    blk = lambda i: (i, 0, 0)
    return pl.pallas_call(
        _route_kernel,
        grid=(steps,),
        in_specs=[
            pl.BlockSpec((N_EXPERTS, tn), col),
            pl.BlockSpec((N_EXPERTS, 1), lambda i: (0, 0)),
        ],
        out_specs=[pl.BlockSpec((1, TOP_K, tn), blk), pl.BlockSpec((TOP_K, tn), col),
                   pl.BlockSpec((1, TOP_K, tn), blk), pl.BlockSpec((N_EXPERTS, LANES), lambda i: (0, 0))],
        out_shape=[jax.ShapeDtypeStruct((steps, TOP_K, tn), jnp.int32), jax.ShapeDtypeStruct((TOP_K, n), F32),
                   jax.ShapeDtypeStruct((steps, TOP_K, tn), jnp.int32),
                   jax.ShapeDtypeStruct((N_EXPERTS, LANES), F32)],
        scratch_shapes=[pltpu.VMEM((N_EXPERTS, LANES), F32)],
        compiler_params=_cparams(("arbitrary",)),
        name="route",
    )(logits, bias_col)


def _dispatch_kernel(dest_ref, zlo_ref, tok_ref, xs_hbm, zero_ref, zsem, sem, *, t_ffn):
    i = pl.program_id(0)
    tn = dest_ref.shape[2]

    @pl.when(i == 0)
    def _():
        zero_ref[...] = jnp.zeros_like(zero_ref)

        def zcopy(e):
            return pltpu.make_async_copy(zero_ref, xs_hbm.at[pl.ds(jnp.maximum(zlo_ref[e], 0), t_ffn)], zsem)

        def start(e, carry):
            @pl.when(zlo_ref[e] >= 0)
            def _():
                zcopy(e).start()
            return carry

        def wait(e, carry):
            @pl.when(zlo_ref[e] >= 0)
            def _():
                zcopy(e).wait()
            return carry

        lax.fori_loop(0, N_EXPERTS, start, 0)
        lax.fori_loop(0, N_EXPERTS, wait, 0)

    for n in range(tn):
        src = tok_ref.at[pl.ds(n * ROW_CHUNKS, ROW_CHUNKS)]
        for slot in range(TOP_K):
            pltpu.make_async_copy(src, xs_hbm.at[dest_ref[0, slot, n]], sem).start(priority=slot)
    for slot in range(TOP_K):
        pltpu.make_async_copy(xs_hbm.at[pl.ds(0, tn)], xs_hbm.at[pl.ds(0, tn)], sem).wait()


def _dispatch(tokens, dest3, zlo, n_rows, t_ffn):
    steps, _, tn = dest3.shape
    smem = pltpu.SMEM
    return pl.pallas_call(
        functools.partial(_dispatch_kernel, t_ffn=t_ffn),
        grid=(steps,),
        in_specs=[
            pl.BlockSpec((1, TOP_K, tn), lambda i: (i, 0, 0), memory_space=smem),
            pl.BlockSpec(memory_space=smem),
            pl.BlockSpec((tn * ROW_CHUNKS, LANES), lambda i: (i, 0)),
        ],
        out_specs=pl.BlockSpec(memory_space=pl.ANY),
        out_shape=jax.ShapeDtypeStruct((n_rows, ROW_CHUNKS, LANES), tokens.dtype),
        scratch_shapes=[pltpu.VMEM((t_ffn, ROW_CHUNKS, LANES), tokens.dtype), pltpu.SemaphoreType.DMA,
                        pltpu.SemaphoreType.DMA],
        compiler_params=_cparams(("arbitrary",)),
        name="dispatch",
    )(dest3, zlo, tokens)


def _ffn_kernel(texp_ref, nused_ref, enext_ref, eord_ref, x_ref, wg_hbm, wu_hbm, wd_hbm, y_ref,
                wgf_ref, wuf_ref, wdf_ref, wgb_ref, wub_ref, wdb_ref, wsem, *, layer):
    j = pl.program_id(0)
    e = texp_ref[j]
    active = j < nused_ref[0]
    first = (j == 0) | (e != texp_ref[jnp.maximum(j - 1, 0)])
    slot = eord_ref[e] % 2
    pairs = ((wg_hbm, wgf_ref), (wu_hbm, wuf_ref), (wd_hbm, wdf_ref))

    def wcopies(expert, s):
        return [pltpu.make_async_copy(hbm.at[layer, expert], buf.at[s], wsem.at[s, i])
                for i, (hbm, buf) in enumerate(pairs)]

    @pl.when(active & (j == 0))
    def _():
        for cp in wcopies(e, slot):
            cp.start()

    @pl.when(active & first)
    def _():
        nxt = enext_ref[e]

        @pl.when(nxt >= 0)
        def _():
            for cp in wcopies(nxt, 1 - slot):
                cp.start(priority=1)

        for cp in wcopies(e, slot):
            cp.wait()
        wgb_ref[...] = wgf_ref[slot].astype(BF16)
        wub_ref[...] = wuf_ref[slot].astype(BF16)
        wdb_ref[...] = wdf_ref[slot].astype(BF16)

    @pl.when(active)
    def _():
        x = _load_chunked(x_ref, x_ref.shape[0] // ROW_CHUNKS).astype(BF16)
        gate = _dot(x, wgb_ref[...])
        up = _dot(x, wub_ref[...])
        act = (gate * jax.nn.sigmoid(gate) * up).astype(BF16)
        _store_chunked(y_ref, _dot(act, wdb_ref[...]))


def _expert_ffn(xs, plan, w_gate, w_up, w_down, layer, t_ffn):
    tile_expert, n_used, e_next, e_ord = plan
    _, _, d, de = w_gate.shape
    n_rows = xs.shape[0] // ROW_CHUNKS
    n_tiles = n_rows // t_ffn
    rowmap = lambda j, te, nu, en, eo: (jnp.minimum(j, nu[0] - 1), 0)
    hbm = pl.BlockSpec(memory_space=pl.ANY)
    return pl.pallas_call(
        functools.partial(_ffn_kernel, layer=layer),
        grid_spec=pltpu.PrefetchScalarGridSpec(
            num_scalar_prefetch=4,
            grid=(n_tiles,),
            in_specs=[pl.BlockSpec((t_ffn * ROW_CHUNKS, LANES), rowmap), hbm, hbm, hbm],
            out_specs=pl.BlockSpec((t_ffn * ROW_CHUNKS, LANES), rowmap),
            scratch_shapes=[pltpu.VMEM((2, d, de), F32), pltpu.VMEM((2, d, de), F32), pltpu.VMEM((2, de, d), F32),
                            pltpu.VMEM((d, de), BF16), pltpu.VMEM((d, de), BF16), pltpu.VMEM((de, d), BF16),
                            pltpu.SemaphoreType.DMA((2, 3))],
        ),
        out_shape=jax.ShapeDtypeStruct((n_rows * ROW_CHUNKS, LANES), F32),
        compiler_params=_cparams(("arbitrary",)),
        name="expert_ffn",
    )(tile_expert, n_used, e_next, e_ord, xs, w_gate, w_up, w_down)


def _combine_kernel(dest_ref, dest_next_ref, y_hbm, wts_ref, x1_ref, mod_ref, lng_ref, lnb_ref, x2_ref,
                    buf_ref, sem, *, alpha):
    i = pl.program_id(0)
    n_steps = pl.num_programs(0)
    tc, d = x1_ref.shape

    def row_copy(idx_ref, parity, n, slot, rows):
        return pltpu.make_async_copy(y_hbm.at[idx_ref[0, slot, n]], buf_ref.at[parity * TOP_K + slot, rows],
                                     sem.at[parity])

    @pl.when(i == 0)
    def _():
        def body(n, carry):
            rows = pl.ds(pl.multiple_of(n * ROW_CHUNKS, ROW_CHUNKS), ROW_CHUNKS)
            for slot in range(TOP_K):
                row_copy(dest_ref, 0, n, slot, rows).start(priority=slot)
            return carry
        lax.fori_loop(0, tc, body, 0, unroll=DMA_UNROLL)

    for parity in range(2):
        @pl.when((i + 1 < n_steps) & ((i + 1) % 2 == parity))
        def _():
            for n in range(tc):
                for slot in range(TOP_K):
                    row_copy(dest_next_ref, parity, n, slot, pl.ds(n * ROW_CHUNKS, ROW_CHUNKS)).start(priority=slot)

    for slot in range(TOP_K):
        pltpu.make_async_copy(y_hbm.at[pl.ds(0, tc)], y_hbm.at[pl.ds(0, tc)], sem.at[i % 2]).wait()
    w = wts_ref[...]
    cur = (i % 2) * TOP_K
    f = w[:, 0:1] * _load_chunked(buf_ref.at[cur], tc) + w[:, 1:2] * _load_chunked(buf_ref.at[cur + 1], tc)
    g2 = mod_ref[0][:, 5 * d:6 * d]
    x2_ref[...] = _post_norm(x1_ref[...], g2, f, alpha) * lng_ref[...] + lnb_ref[...]


def _combine(y, dest3, wts_t, x1, mods, lng, lnb, *, tok0, seq_len, mod_row, alpha):
    n, d = x1.shape
    tc = dest3.shape[2]
    tps = max(seq_len // tc, 1)
    steps = n // tc
    blk0 = tok0 // tc
    mod_map = (lambda i: (i // tps, 0, 0)) if mod_row is None else (lambda i: (mod_row, 0, 0))
    cur = lambda i: (blk0 + i, 0, 0)
    nxt = lambda i: (blk0 + jnp.minimum(i + 1, steps - 1), 0, 0)
    smem = pltpu.SMEM
    return pl.pallas_call(
        functools.partial(_combine_kernel, alpha=alpha),
        grid=(steps,),
        in_specs=[
            pl.BlockSpec((1, TOP_K, tc), cur, memory_space=smem),
            pl.BlockSpec((1, TOP_K, tc), nxt, memory_space=smem),
            pl.BlockSpec(memory_space=pl.ANY),
            pl.BlockSpec((tc, TOP_K), lambda i: (blk0 + i, 0)),
            pl.BlockSpec((tc, d), lambda i: (i, 0)),
            pl.BlockSpec((1, 1, mods.shape[2]), mod_map),
            pl.BlockSpec(lng.shape, lambda i: (0, 0)),
            pl.BlockSpec(lnb.shape, lambda i: (0, 0)),
        ],
        out_specs=pl.BlockSpec((tc, d), lambda i: (i, 0)),
        out_shape=jax.ShapeDtypeStruct((n, d), F32),
        scratch_shapes=[pltpu.VMEM((2 * TOP_K, tc * ROW_CHUNKS, LANES), F32), pltpu.SemaphoreType.DMA((2,))],
        compiler_params=_cparams(("arbitrary",)),
        name="combine",
    )(dest3, dest3, y, wts_t, x1, mods, lng, lnb)


def _rope_tables(seq_len):
    t = jnp.arange(seq_len)
    row = (t // GRID_W).astype(F32)
    col = (t % GRID_W).astype(F32)
    half = HEAD_DIM // 2
    inv_freq = ROPE_BASE ** (-jnp.arange(0, half, 2, dtype=F32) / half)
    ang_r, ang_c = row[:, None] * inv_freq, col[:, None] * inv_freq
    cr, sr, cc, sc = jnp.cos(ang_r), jnp.sin(ang_r), jnp.cos(ang_c), jnp.sin(ang_c)
    cos = jnp.concatenate([cr, cr, cc, cc], axis=1)
    sin = jnp.concatenate([-sr, sr, -sc, sc], axis=1)
    return jnp.tile(cos, (1, LANES // HEAD_DIM)), jnp.tile(sin, (1, LANES // HEAD_DIM))


def _pair_heads(a, axis):
    shp = a.shape
    a = a.reshape(*shp[:axis], N_KV_HEADS, Q_REP, HEAD_DIM, *shp[axis + 1:])
    a = jnp.swapaxes(a, axis, axis + 1)
    return a.reshape(shp)


def _moe_plan(eid3, rank3, counts, n_tok, t_ffn):
    counts = counts.reshape(EXPERTS_PER_GROUP, N_EXPERT_GROUPS).T.reshape(N_EXPERTS).astype(jnp.int32)
    padded = (counts + t_ffn - 1) // t_ffn * t_ffn
    pends = jnp.cumsum(padded)
    pstarts = (pends - padded).astype(jnp.int32)
    experts = jnp.arange(N_EXPERTS, dtype=jnp.int32)
    dest3 = rank3 + jnp.sum(jnp.where(eid3[..., None] == experts, pstarts, 0), axis=-1)
    n_rows = -(-(n_tok * TOP_K) // t_ffn) * t_ffn + N_EXPERTS * t_ffn
    n_tiles = n_rows // t_ffn
    tile_row0 = jnp.arange(n_tiles, dtype=jnp.int32) * t_ffn
    tile_expert = jnp.minimum(jnp.sum(pends[None, :] <= tile_row0[:, None], axis=1), N_EXPERTS - 1)
    n_used = (pends[-1] // t_ffn).astype(jnp.int32).reshape(1)
    nonempty = counts > 0
    later = (experts[None, :] > experts[:, None]) & nonempty[None, :]
    e_next = jnp.min(jnp.where(later, experts[None, :], N_EXPERTS), axis=1)
    e_next = jnp.where(e_next < N_EXPERTS, e_next, -1).astype(jnp.int32)
    e_ord = (jnp.cumsum(nonempty.astype(jnp.int32)) - 1).astype(jnp.int32)
    zlo = jnp.where(nonempty, pends - t_ffn, -1).astype(jnp.int32)
    plan = (tile_expert.astype(jnp.int32), n_used, e_next, e_ord)
    return dest3.astype(jnp.int32), plan, zlo, n_rows


def kernel(x, c, ctx, c_ctx, w_ada, b_ada, w_in, w_pool_grp, pool_scale, w_pool_br, w_attn_br, attn_sink,
           w_o, ln1_g, ln1_b, w_router, router_bias, w_exp_gate, w_exp_up, w_exp_down, ln2_g, ln2_b):
    bsz, seq, d = x.shape
    assert d == ROW_CHUNKS * LANES and bsz < SUBLANES
    c_len = ctx.shape[1]
    depth = w_in.shape[0]
    n_lat, n_ctx = bsz * seq, bsz * c_len
    alpha = (2 * depth) ** 0.25

    cond = jnp.zeros((SUBLANES, d), F32).at[:bsz].set(c).at[bsz].set(c_ctx)
    ada = _ada_terms(cond, w_ada, b_ada)
    cos, sin = _rope_tables(seq)
    wr_t = w_router.reshape(d, N_EXPERT_GROUPS, EXPERTS_PER_GROUP).transpose(2, 1, 0).reshape(N_EXPERTS, d)
    wr_t = wr_t.astype(BF16)
    bias_col = router_bias.reshape(N_EXPERT_GROUPS, EXPERTS_PER_GROUP).T.reshape(N_EXPERTS, 1).astype(F32)

    xl = x.reshape(n_lat, d)
    xc = ctx.reshape(n_ctx, d)
    for l in range(depth):
        ctx_out = l < depth - 1
        mods = ada[l].reshape(SUBLANES, 1, 6 * d)
        w_l = w_in[l]
        w_inp = jnp.concatenate([w_l[:, :COL_Q], _pair_heads(w_l[:, COL_Q:COL_K], 1), w_l[:, COL_K:]],
                                axis=1).astype(BF16)
        sink_col = jnp.broadcast_to(attn_sink[l].reshape(N_KV_HEADS, Q_REP, 1, 1),
                                    (N_KV_HEADS, Q_REP, BLOCK, 1)).reshape(N_KV_HEADS, Q_REP * BLOCK, 1)
        sink_col = (sink_col * LOG2E).astype(F32)
        mix_w = (w_pool_grp[l].astype(BF16), pool_scale[l].reshape(1, POOL_W), w_pool_br[l].astype(BF16),
                 _pair_heads(w_attn_br[l], 0).astype(BF16), w_o[l].astype(BF16),
                 ln1_g[l].reshape(1, d), ln1_b[l].reshape(1, d), wr_t)
        lng2, lnb2 = ln2_g[l].reshape(1, d), ln2_b[l].reshape(1, d)

        if ctx_out:
            uc, qc, kc, vc, gc = _inproj(xc, mods, w_inp, cos, sin, seq_len=c_len, mod_row=bsz, rope=False)
        else:
            kc, vc = _inproj(xc, mods, w_inp, cos, sin, seq_len=c_len, mod_row=bsz, rope=False, kv_only=True)
        kc3, vc3 = kc.reshape(bsz, c_len, KV_W), vc.reshape(bsz, c_len, KV_W)
        u, q, k, v, g = _inproj(xl, mods, w_inp, cos, sin, seq_len=seq, mod_row=None, rope=True)
        n_tok = n_lat + n_ctx if ctx_out else n_lat
        tokens = None
        if ctx_out:
            attn_c = _attention(qc, kc, vc, kc3, vc3, sink_col, seq_len=c_len, local=False)
            xc1, tokens, logits_c = _merge(uc, attn_c, gc, xc, mods, mix_w, seq_len=c_len, mod_row=bsz,
                                           alpha=alpha, h2_tokens=n_tok, h2_tok0=n_lat)
        attn = _attention(q, k, v, kc3, vc3, sink_col, seq_len=seq, local=True)
        x1, tokens, logits = _merge(u, attn, g, xl, mods, mix_w, seq_len=seq, mod_row=None, alpha=alpha,
                                    h2_tokens=n_tok, h2_tok0=0, h2_buf=tokens)
        if ctx_out:
            logits = jnp.concatenate([logits, logits_c], axis=1)

        eid3, wts, rank3, counts = _route(logits, bias_col)
        dest3, plan, zlo, n_rows = _moe_plan(eid3, rank3, counts[:, 0], n_tok, T_FFN)
        xs = _dispatch(tokens, dest3, zlo, n_rows, T_FFN)
        y = _expert_ffn(xs.reshape(n_rows * ROW_CHUNKS, LANES), plan, w_exp_gate, w_exp_up, w_exp_down, l, T_FFN)
        y3 = y.reshape(n_rows, ROW_CHUNKS, LANES)
        wts_t = wts.T
        xl = _combine(y3, dest3, wts_t, x1, mods, lng2, lnb2, tok0=0, seq_len=seq, mod_row=None, alpha=alpha)
        if ctx_out:
            xc = _combine(y3, dest3, wts_t, xc1, mods, lng2, lnb2, tok0=n_lat, seq_len=c_len, mod_row=bsz,
                          alpha=alpha)
    return xl.reshape(bsz, seq, d)
```

```python
import functools

import jax
import jax.numpy as jnp
from jax import lax
from jax.experimental import pallas as pl
from jax.experimental.pallas import tpu as pltpu

F32 = jnp.float32
BF16 = jnp.bfloat16

GRID_W = 64
POOL_WINDOWS = (2, 4, 8, 16)
POOL_GROUP_W = 128
POOL_W = 512
HEAD_DIM = 64
N_HEADS = 8
N_KV_HEADS = 2
Q_REP = N_HEADS // N_KV_HEADS
ATTN_W = N_HEADS * HEAD_DIM
KV_W = N_KV_HEADS * HEAD_DIM
BLOCK = 128
ROPE_BASE = 10000.0
COL_POOL = 0
COL_Q = COL_POOL + POOL_W
COL_K = COL_Q + ATTN_W
COL_V = COL_K + KV_W
COL_GATE = COL_V + KV_W
N_EXPERTS = 32
N_EXPERT_GROUPS = 8
EXPERTS_PER_GROUP = N_EXPERTS // N_EXPERT_GROUPS
TOP_K = 2
LN_EPS = 1e-6
LOG2E = 1.4426950408889634

LANES = 128
SUBLANES = 8
POOL_HALO = SUBLANES
ROW_CHUNKS = SUBLANES
VMEM_LIMIT = 56 * 1024 * 1024

ADA_COL_TILES = 4
TM_IN = 1024
TQ_ATT = 2048
TQ_MIX = 512
TN_TOK = 512
T_FFN = 256
DMA_UNROLL = 8


def _cparams(sem):
    return pltpu.CompilerParams(dimension_semantics=sem, vmem_limit_bytes=VMEM_LIMIT)


def _layer_norm(x, eps=LN_EPS):
    mu = jnp.mean(x, axis=-1, keepdims=True)
    xc = x - mu
    var = jnp.mean(xc * xc, axis=-1, keepdims=True)
    return xc * lax.rsqrt(var + eps)


def _post_norm(x, gate, y, alpha):
    return _layer_norm(x + (gate * (1.0 / alpha)) * y, LN_EPS / (alpha * alpha))


def _dot(a, b):
    return jnp.dot(a, b, preferred_element_type=F32)


def _store_chunked(ref, val):
    t = val.shape[0]
    for s in range(ROW_CHUNKS):
        ref[pl.ds(s, t, stride=ROW_CHUNKS), :] = val[:, s * LANES:(s + 1) * LANES]


def _load_chunked(ref, t):
    return jnp.concatenate([ref[pl.ds(s, t, stride=ROW_CHUNKS), :] for s in range(ROW_CHUNKS)], axis=1)


def _ada_kernel(cond_ref, w_ref, b_ref, o_ref):
    s = cond_ref[...]
    s = s * jax.nn.sigmoid(s)
    o_ref[0] = _dot(s.astype(BF16), w_ref[0].astype(BF16)) + b_ref[0]


def _ada_terms(cond, w_ada, b_ada):
    depth, d, n6 = w_ada.shape
    rows = cond.shape[0]
    tn = n6 // ADA_COL_TILES
    return pl.pallas_call(
        _ada_kernel,
        grid=(depth, ADA_COL_TILES),
        in_specs=[
            pl.BlockSpec((rows, d), lambda l, j: (0, 0)),
            pl.BlockSpec((1, d, tn), lambda l, j: (l, 0, j)),
            pl.BlockSpec((1, 1, tn), lambda l, j: (l, 0, j)),
        ],
        out_specs=pl.BlockSpec((1, rows, tn), lambda l, j: (l, 0, j)),
        out_shape=jax.ShapeDtypeStruct((depth, rows, n6), F32),
        compiler_params=_cparams(("arbitrary", "arbitrary")),
        name="ada_terms",
    )(cond, w_ada, b_ada.reshape(depth, 1, n6))


def _rope(t, cos, sin):
    lane = lax.broadcasted_iota(jnp.int32, (1, LANES), 1)
    first = (lane % 32) < 16
    outs = []
    for j in range(t.shape[1] // LANES):
        tj = t[:, j * LANES:(j + 1) * LANES]
        partner = jnp.where(first, pltpu.roll(tj, LANES - 16, 1), pltpu.roll(tj, 16, 1))
        outs.append(tj * cos + partner * sin)
    return outs[0] if len(outs) == 1 else jnp.concatenate(outs, axis=1)


def _inproj_kernel(x_ref, mod_ref, w_ref, cos_ref, sin_ref, *out_refs, rope, kv_only):
    d = x_ref.shape[1]
    mod = mod_ref[0]
    shift, scale = mod[:, 0:d], mod[:, d:2 * d]
    h = (_layer_norm(x_ref[...]) * (1.0 + scale) + shift).astype(BF16)

    def proj(lo, hi):
        return _dot(h, w_ref[:, lo:hi])

    if kv_only:
        k_ref, v_ref = out_refs
    else:
        u_ref, q_ref, k_ref, v_ref, g_ref = out_refs
        u_ref[...] = proj(COL_POOL, COL_Q)
        q = proj(COL_Q, COL_K)
        if rope:
            q = _rope(q, cos_ref[...], sin_ref[...])
        q_ref[...] = (q * (LOG2E * HEAD_DIM ** -0.5)).astype(BF16)
        g_ref[...] = jax.nn.sigmoid(proj(COL_GATE, w_ref.shape[1]))
    kv = proj(COL_K, COL_GATE)
    k = kv[:, 0:KV_W]
    if rope:
        k = _rope(k, cos_ref[...], sin_ref[...])
    k_ref[...] = k.astype(BF16)
    v_ref[...] = kv[:, KV_W:2 * KV_W].astype(BF16)


def _inproj(x2d, mods, w_in, cos, sin, *, seq_len, mod_row, rope, kv_only=False):
    n, d = x2d.shape
    tm = min(TM_IN, seq_len)
    tps = seq_len // tm
    n_cols = w_in.shape[1]
    mod_map = (lambda i: (i // tps, 0, 0)) if mod_row is None else (lambda i: (mod_row, 0, 0))
    tab_map = (lambda i: (i % tps, 0)) if rope else (lambda i: (0, 0))
    row = lambda i: (i, 0)
    kv_shapes = [jax.ShapeDtypeStruct((n, KV_W), BF16)] * 2
    kv_specs = [pl.BlockSpec((tm, KV_W), row)] * 2
    if kv_only:
        out_shape, out_specs = kv_shapes, kv_specs
    else:
        out_shape = [jax.ShapeDtypeStruct((n, POOL_W), F32), jax.ShapeDtypeStruct((n, ATTN_W), BF16),
                     *kv_shapes, jax.ShapeDtypeStruct((n, n_cols - COL_GATE), F32)]
        out_specs = [pl.BlockSpec((tm, POOL_W), row), pl.BlockSpec((tm, ATTN_W), row),
                     *kv_specs, pl.BlockSpec((tm, n_cols - COL_GATE), row)]
    return pl.pallas_call(
        functools.partial(_inproj_kernel, rope=rope, kv_only=kv_only),
        grid=(n // tm,),
        in_specs=[
            pl.BlockSpec((tm, d), row),
            pl.BlockSpec((1, 1, mods.shape[2]), mod_map),
            pl.BlockSpec((d, n_cols), lambda i: (0, 0)),
            pl.BlockSpec((tm, LANES), tab_map),
            pl.BlockSpec((tm, LANES), tab_map),
        ],
        out_specs=out_specs,
        out_shape=out_shape,
        compiler_params=_cparams(("parallel",)),
        name="inproj_kv" if kv_only else "inproj",
    )(x2d, mods, w_in, cos, sin)


def _merge_kernel(u_prev_ref, u_ref, u_next_ref, attn_ref, g_ref, x_ref, mod_ref,
                  wgrp_ref, pscale_ref, wpool_ref, wattn_ref, wo_ref, lng_ref, lnb_ref, wr_ref,
                  *rest, seq_len, alpha, n_tiles):
    x1_ref, h2_ref, logit_ref, uext_ref, m0_ref, m1_ref = rest[-6:]
    tq, d = x_ref.shape
    tps = seq_len // tq
    j = pl.program_id(0)
    t_in_seq = jnp.minimum(j, n_tiles - 1) % tps
    is_first = t_in_seq == 0
    is_last = t_in_seq == tps - 1
    n_groups = len(POOL_WINDOWS)
    cw = d // n_groups
    row_chunks = tq // BLOCK

    def a_fill_window():
        top = 2 * POOL_HALO
        uext_ref[0:POOL_HALO, :] = jnp.zeros((POOL_HALO, POOL_W), F32)
        uext_ref[POOL_HALO:top, :] = jnp.where(is_first, 0.0, u_prev_ref[...])
        uext_ref[top:top + tq, :] = u_ref[...]
        uext_ref[top + tq:top + tq + POOL_HALO, :] = jnp.where(is_last, 0.0, u_next_ref[...])
        uext_ref[top + tq + POOL_HALO:, :] = jnp.zeros((BLOCK - top - POOL_HALO, POOL_W), F32)

    def a_pool_group(gi):
        w = POOL_WINDOWS[gi]
        top = 2 * POOL_HALO
        cols = slice(gi * POOL_GROUP_W, (gi + 1) * POOL_GROUP_W)
        ue = uext_ref[:, cols]
        ue_hi = ue.astype(BF16)
        ue_lo = (ue - ue_hi.astype(F32)).astype(BF16)
        row_t = lax.broadcasted_iota(jnp.int32, (BLOCK, 2 * BLOCK), 0)
        col_j = lax.broadcasted_iota(jnp.int32, (BLOCK, 2 * BLOCK), 1)
        first_j = row_t + top - w // 2
        band = jnp.where((col_j >= first_j) & (col_j < first_j + w), 1.0, 0.0).astype(BF16)
        sums = []
        for b in range(row_chunks):
            win = slice(b * BLOCK, (b + 2) * BLOCK)
            sums.append(_dot(band, ue_hi[win]) + _dot(band, ue_lo[win]))
        acc = sums[0] if len(sums) == 1 else jnp.concatenate(sums, axis=0)
        pos = t_in_seq * tq + lax.broadcasted_iota(jnp.int32, (tq, 1), 0)
        lo = jnp.maximum(pos - w // 2, 0)
        hi = jnp.minimum(pos - w // 2 + w - 1, seq_len - 1)
        mean = acc / (hi - lo + 1).astype(F32)
        pg = (mean - u_ref[:, cols]).astype(BF16)
        return _dot(pg, wgrp_ref[gi])

    def a_merge_chunk(c, pool_lat, m_buf):
        cols = slice(c * cw, (c + 1) * cw)
        pool_proj = _dot(pool_lat, wpool_ref[:, cols])
        attn_proj = _dot(attn_ref[...], wattn_ref[:, cols])
        m_buf[:, cols] = (g_ref[:, c * cw:(c + 1) * cw] * pool_proj
                          + g_ref[:, d + c * cw:d + (c + 1) * cw] * attn_proj).astype(BF16)

    def b_project_chunk(c, m_buf):
        return _dot(m_buf[...], wo_ref[:, c * cw:(c + 1) * cw])

    def b_norm_rows(r, ys):
        rows = slice(r * BLOCK, (r + 1) * BLOCK)
        y = jnp.concatenate([yc[rows] for yc in ys], axis=1)
        mod = mod_ref[0]
        g1 = mod[:, 2 * d:3 * d]
        sh2, sc2 = mod[:, 3 * d:4 * d], mod[:, 4 * d:5 * d]
        x1 = _post_norm(x_ref[rows, :], g1, y, alpha) * lng_ref[...] + lnb_ref[...]
        x1_ref[rows, :] = x1
        h2 = _layer_norm(x1) * (1.0 + sc2) + sh2
        _store_chunked(h2_ref.at[r * BLOCK * ROW_CHUNKS:(r + 1) * BLOCK * ROW_CHUNKS], h2)
        logit_ref[:, rows] = lax.dot_general(wr_ref[...], h2.astype(BF16), (((1,), (1,)), ((), ())),
                                             preferred_element_type=F32)

    def run(a_buf, b_buf):
        if a_buf is not None:
            a_fill_window()
        ys, pooled = [], []
        for c in range(n_groups):
            if b_buf is not None:
                ys.append(b_project_chunk(c, b_buf))
            if a_buf is not None:
                pooled.append(a_pool_group(c))
        if a_buf is not None:
            pool_lat = (jnp.concatenate(pooled, axis=1) * pscale_ref[...]).astype(BF16)
        for c in range(n_groups):
            if a_buf is not None:
                a_merge_chunk(c, pool_lat, a_buf)
            if b_buf is not None and c < row_chunks:
                b_norm_rows(c, ys)

    bufs = (m0_ref, m1_ref)

    @pl.when(j == 0)
    def _():
        run(bufs[0], None)

    for parity in range(2):
        @pl.when((j >= 1) & (j < n_tiles) & (j % 2 == parity))
        def _():
            run(bufs[parity], bufs[1 - parity])

    @pl.when(j == n_tiles)
    def _():
        run(None, bufs[(n_tiles - 1) % 2])


def _attn_kernel(q_ref, k_prev_ref, k_ref, k_next_ref, v_prev_ref, v_ref, v_next_ref, kc_ref, vc_ref,
                 sink_ref, attn_ref, s0_ref, s1_ref, kg_ref, vg_ref, *, seq_len, local):
    tq = q_ref.shape[0]
    nb = tq // BLOCK
    tps = seq_len // tq
    t_in_seq = pl.program_id(0) % tps
    is_first = t_in_seq == 0
    is_last = t_in_seq == tps - 1
    lane = lax.broadcasted_iota(jnp.int32, (1, LANES), 1)
    lo_half = lane < HEAD_DIM
    one = jnp.ones((), BF16)
    kc = kc_ref[0]
    vc = vc_ref[0]
    kc_g = [jnp.where(lo_half, kc, 0), jnp.where(lo_half, 0, kc)]
    vc_g = [jnp.where(lo_half, vc, one), jnp.where(lo_half, one, vc)]
    neg = jnp.float32(-jnp.inf)
    if local:
        k_ext = jnp.concatenate([k_prev_ref[...], k_ref[...], k_next_ref[...]], axis=0)
        v_ext = jnp.concatenate([v_prev_ref[...], v_ref[...], v_next_ref[...]], axis=0)
        kg_ref[0] = jnp.where(lo_half, k_ext, 0)
        kg_ref[1] = jnp.where(lo_half, 0, k_ext)
        vg_ref[0] = jnp.where(lo_half, v_ext, one)
        vg_ref[1] = jnp.where(lo_half, one, v_ext)
        qq = lax.broadcasted_iota(jnp.int32, (Q_REP * BLOCK, BLOCK), 0) % BLOCK
        kk = lax.broadcasted_iota(jnp.int32, (Q_REP * BLOCK, BLOCK), 1)
        mask_prev = jnp.where(kk >= qq, 0.0, neg)
        mask_next = jnp.where(kk <= qq, 0.0, neg)

    def block_rows(b):
        return pl.ds(b * BLOCK, BLOCK) if isinstance(b, int) else pl.ds(pl.multiple_of(b * BLOCK, BLOCK), BLOCK)

    def band_keys(b):
        start = b * BLOCK if isinstance(b, int) else pl.multiple_of(b * BLOCK, BLOCK)
        return pl.ds(start, 3 * BLOCK)

    def stage_a(b, s_buf, first_block, last_block):
        q_st = jnp.concatenate([q_ref[block_rows(b), c * LANES:(c + 1) * LANES] for c in range(Q_REP)], axis=0)
        for g in range(N_KV_HEADS):
            k_all = jnp.concatenate([kg_ref[g, band_keys(b), :], kc_g[g]], axis=0) if local else kc_g[g]
            s = lax.dot_general(q_st, k_all, (((1,), (1,)), ((), ())), preferred_element_type=F32)
            if local:
                m_prev = jnp.where(first_block, neg, mask_prev)
                m_next = jnp.where(last_block, neg, mask_next)
                s = jnp.concatenate([s[:, 0:BLOCK] + m_prev, s[:, BLOCK:2 * BLOCK],
                                     s[:, 2 * BLOCK:3 * BLOCK] + m_next, s[:, 3 * BLOCK:]], axis=1)
            s_buf[g] = s

    def stage_b(b, s_buf):
        pv, sink_w = [], []
        for g in range(N_KV_HEADS):
            v_all = jnp.concatenate([vg_ref[g, band_keys(b), :], vc_g[g]], axis=0) if local else vc_g[g]
            s = s_buf[g]
            sk = sink_ref[g]
            m = jnp.maximum(jnp.max(s, axis=-1, keepdims=True), sk)
            p = jnp.exp2((s - m).astype(BF16))
            pv.append(_dot(p, v_all))
            sink_w.append(jnp.exp2(sk - m))
        num = jnp.where(lo_half, pv[0], pv[1])
        den = pltpu.roll(jnp.where(lo_half, pv[1], pv[0]), HEAD_DIM, 1) + jnp.where(lo_half, sink_w[0], sink_w[1])
        o = num / den
        for c in range(Q_REP):
            attn_ref[block_rows(b), c * LANES:(c + 1) * LANES] = o[c * BLOCK:(c + 1) * BLOCK].astype(BF16)

    stage_a(0, s0_ref, is_first, False)

    def body(j, carry):
        stage_a(2 * j + 1, s1_ref, False, False)
        stage_b(2 * j, s0_ref)
        stage_a(2 * j + 2, s0_ref, False, False)
        stage_b(2 * j + 1, s1_ref)
        return carry

    lax.fori_loop(0, nb // 2 - 1, body, 0)
    stage_a(nb - 1, s1_ref, False, is_last)
    stage_b(nb - 2, s0_ref)
    stage_b(nb - 1, s1_ref)


def _attention(q, k, v, kc, vc, sink_col, *, seq_len, local):
    n = q.shape[0]
    tq = min(TQ_ATT, seq_len)
    tps = seq_len // tq
    kb = tq // BLOCK
    n_kb = n // BLOCK
    c_len = kc.shape[1]
    n_keys = (3 * BLOCK if local else 0) + c_len
    row = lambda i: (i, 0)
    ctx_map = lambda i: (i // tps, 0, 0)
    kv_prev = pl.BlockSpec((BLOCK, KV_W), lambda i: (jnp.maximum(i * kb - 1, 0), 0))
    kv_cur = pl.BlockSpec((tq, KV_W), row)
    kv_next = pl.BlockSpec((BLOCK, KV_W), lambda i: (jnp.minimum((i + 1) * kb, n_kb - 1), 0))
    return pl.pallas_call(
        functools.partial(_attn_kernel, seq_len=seq_len, local=local),
        grid=(n // tq,),
        in_specs=[
            pl.BlockSpec((tq, ATTN_W), row),
            kv_prev, kv_cur, kv_next, kv_prev, kv_cur, kv_next,
            pl.BlockSpec((1, c_len, KV_W), ctx_map), pl.BlockSpec((1, c_len, KV_W), ctx_map),
            pl.BlockSpec(sink_col.shape, lambda i: (0, 0, 0)),
        ],
        out_specs=pl.BlockSpec((tq, ATTN_W), row),
        out_shape=jax.ShapeDtypeStruct((n, ATTN_W), BF16),
        scratch_shapes=[pltpu.VMEM((N_KV_HEADS, Q_REP * BLOCK, n_keys), F32),
                        pltpu.VMEM((N_KV_HEADS, Q_REP * BLOCK, n_keys), F32),
                        pltpu.VMEM((N_KV_HEADS, tq + 2 * BLOCK, KV_W), BF16),
                        pltpu.VMEM((N_KV_HEADS, tq + 2 * BLOCK, KV_W), BF16)],
        compiler_params=_cparams(("parallel",)),
        name="attention" if local else "attention_ctx",
    )(q, k, k, k, v, v, v, kc, vc, sink_col)


def _merge(u, attn, gates, x2d, mods, wts, *, seq_len, mod_row, alpha, h2_tokens, h2_tok0, h2_buf=None):
    n, d = x2d.shape
    tq = min(TQ_MIX, seq_len)
    h2_blk0 = h2_tok0 // tq
    tps = seq_len // tq
    hb = tq // POOL_HALO
    n_hb = n // POOL_HALO
    n_tiles = n // tq
    ta = lambda i: jnp.minimum(i, n_tiles - 1)
    tb = lambda i: jnp.maximum(i - 1, 0)
    row_a = lambda i: (ta(i), 0)
    row_b = lambda i: (tb(i), 0)
    const2 = lambda i: (0, 0)
    const3 = lambda i: (0, 0, 0)
    mod_map = (lambda i: (tb(i) // tps, 0, 0)) if mod_row is None else (lambda i: (mod_row, 0, 0))
    u_prev = pl.BlockSpec((POOL_HALO, POOL_W), lambda i: (jnp.maximum(ta(i) * hb - 1, 0), 0))
    u_next = pl.BlockSpec((POOL_HALO, POOL_W), lambda i: (jnp.minimum((ta(i) + 1) * hb, n_hb - 1), 0))
    wgrp, pscale, wpool, wattn, wo, lng, lnb, wr_t = wts
    operands = [u, u, u, attn, gates, x2d, mods, wgrp, pscale, wpool, wattn, wo, lng, lnb, wr_t]
    alias_specs, aliases = [], {}
    if h2_buf is not None:
        alias_specs = [pl.BlockSpec(memory_space=pl.ANY)]
        aliases = {len(operands): 1}
        operands.append(h2_buf)
    return pl.pallas_call(
        functools.partial(_merge_kernel, seq_len=seq_len, alpha=alpha, n_tiles=n_tiles),
        grid=(n_tiles + 1,),
        input_output_aliases=aliases,
        in_specs=[
            u_prev, pl.BlockSpec((tq, POOL_W), row_a), u_next,
            pl.BlockSpec((tq, ATTN_W), row_a),
            pl.BlockSpec((tq, 2 * d), row_a),
            pl.BlockSpec((tq, d), row_b),
            pl.BlockSpec((1, 1, mods.shape[2]), mod_map),
            pl.BlockSpec(wgrp.shape, const3), pl.BlockSpec(pscale.shape, const2),
            pl.BlockSpec(wpool.shape, const2), pl.BlockSpec(wattn.shape, const2),
            pl.BlockSpec(wo.shape, const2),
            pl.BlockSpec(lng.shape, const2), pl.BlockSpec(lnb.shape, const2),
            pl.BlockSpec(wr_t.shape, const2),
            *alias_specs,
        ],
        out_specs=[pl.BlockSpec((tq, d), row_b),
                   pl.BlockSpec((tq * ROW_CHUNKS, LANES), lambda i: (h2_blk0 + tb(i), 0)),
                   pl.BlockSpec((N_EXPERTS, tq), lambda i: (0, tb(i)))],
        out_shape=[jax.ShapeDtypeStruct((n, d), F32),
                   jax.ShapeDtypeStruct((h2_tokens * ROW_CHUNKS, LANES), F32),
                   jax.ShapeDtypeStruct((N_EXPERTS, n), F32)],
        scratch_shapes=[pltpu.VMEM((tq + BLOCK, POOL_W), F32), pltpu.VMEM((tq, d), BF16),
                        pltpu.VMEM((tq, d), BF16)],
        compiler_params=_cparams(("arbitrary",)),
        name="merge",
    )(*operands)


def _route_kernel(logit_ref, bias_ref, eid_ref, wts_ref, rank_ref, cnt_ref, base_ref):
    tn = logit_ref.shape[1]
    ng, epg = N_EXPERT_GROUPS, EXPERTS_PER_GROUP

    @pl.when(pl.program_id(0) == 0)
    def _():
        base_ref[...] = jnp.zeros_like(base_ref)

    scores = jax.nn.sigmoid(logit_ref[...])
    biased = scores + bias_ref[...]
    bj = [biased[j * ng:(j + 1) * ng] for j in range(epg)]
    sj = [scores[j * ng:(j + 1) * ng] for j in range(epg)]
    hi01, lo01 = jnp.maximum(bj[0], bj[1]), jnp.minimum(bj[0], bj[1])
    hi23, lo23 = jnp.maximum(bj[2], bj[3]), jnp.minimum(bj[2], bj[3])
    gscore = jnp.maximum(hi01, hi23) + jnp.maximum(jnp.minimum(hi01, hi23), jnp.maximum(lo01, lo23))
    giota = lax.broadcasted_iota(jnp.int32, (ng, tn), 0)
    gmax = jnp.max(gscore, axis=0, keepdims=True)
    g_first = jnp.min(jnp.where(gscore == gmax, giota.astype(F32), float(ng)), axis=0, keepdims=True)
    g_sel = g_first.astype(jnp.int32)
    in_g = giota == g_sel
    vb = [jnp.sum(jnp.where(in_g, b, 0.0), axis=0, keepdims=True) for b in bj]
    vs = [jnp.sum(jnp.where(in_g, s, 0.0), axis=0, keepdims=True) for s in sj]

    def first_best(vals):
        best = functools.reduce(jnp.maximum, vals)
        idx = jnp.full(best.shape, epg - 1, jnp.int32)
        for j in range(epg - 2, -1, -1):
            idx = jnp.where(vals[j] == best, j, idx)
        return idx

    def pick(vals, idx):
        out = vals[epg - 1]
        for j in range(epg - 2, -1, -1):
            out = jnp.where(idx == j, vals[j], out)
        return out

    l1 = first_best(vb)
    l2 = first_best([jnp.where(l1 == j, -jnp.inf, vb[j]) for j in range(epg)])
    w1, w2 = pick(vs, l1), pick(vs, l2)
    wsum = w1 + w2
    eid_ref[0, 0:1, :] = g_sel * epg + l1
    eid_ref[0, 1:2, :] = g_sel * epg + l2
    wts_ref[0:1, :] = w1 / wsum
    wts_ref[1:2, :] = w2 / wsum

    r1, r2 = l1 * ng + g_sel, l2 * ng + g_sel
    riota = lax.broadcasted_iota(jnp.int32, (N_EXPERTS, tn), 0)
    hit1, hit2 = riota == r1, riota == r2
    onehot = jnp.where(hit1 | hit2, 1.0, 0.0)
    before = lax.broadcasted_iota(jnp.int32, (tn, tn), 0) < lax.broadcasted_iota(jnp.int32, (tn, tn), 1)
    prefix = _dot(onehot.astype(BF16), jnp.where(before, 1.0, 0.0).astype(BF16)) + base_ref[:, 0:1]
    rank_ref[0, 0:1, :] = jnp.sum(jnp.where(hit1, prefix, 0.0), axis=0, keepdims=True).astype(jnp.int32)
    rank_ref[0, 1:2, :] = jnp.sum(jnp.where(hit2, prefix, 0.0), axis=0, keepdims=True).astype(jnp.int32)
    base_ref[...] = base_ref[...] + jnp.sum(onehot, axis=1, keepdims=True)
    cnt_ref[...] = base_ref[...]


def _route(logits, bias_col):
    n = logits.shape[1]
    tn = TN_TOK
    steps = n // tn
    col = lambda i: (0, i)
    blk = lambda i: (i, 0, 0)
    return pl.pallas_call(
        _route_kernel,
        grid=(steps,),
        in_specs=[
            pl.BlockSpec((N_EXPERTS, tn), col),
            pl.BlockSpec((N_EXPERTS, 1), lambda i: (0, 0)),
        ],
        out_specs=[pl.BlockSpec((1, TOP_K, tn), blk), pl.BlockSpec((TOP_K, tn), col),
                   pl.BlockSpec((1, TOP_K, tn), blk), pl.BlockSpec((N_EXPERTS, LANES), lambda i: (0, 0))],
        out_shape=[jax.ShapeDtypeStruct((steps, TOP_K, tn), jnp.int32), jax.ShapeDtypeStruct((TOP_K, n), F32),
                   jax.ShapeDtypeStruct((steps, TOP_K, tn), jnp.int32),
                   jax.ShapeDtypeStruct((N_EXPERTS, LANES), F32)],
        scratch_shapes=[pltpu.VMEM((N_EXPERTS, LANES), F32)],
        compiler_params=_cparams(("arbitrary",)),
        name="route",
    )(logits, bias_col)


def _dispatch_kernel(dest_ref, zlo_ref, tok_ref, xs_hbm, zero_ref, zsem, sem, *, t_ffn):
    i = pl.program_id(0)
    tn = dest_ref.shape[2]

    @pl.when(i == 0)
    def _():
        zero_ref[...] = jnp.zeros_like(zero_ref)

        def zcopy(e):
            return pltpu.make_async_copy(zero_ref, xs_hbm.at[pl.ds(jnp.maximum(zlo_ref[e], 0), t_ffn)], zsem)

        def start(e, carry):
            @pl.when(zlo_ref[e] >= 0)
            def _():
                zcopy(e).start()
            return carry

        def wait(e, carry):
            @pl.when(zlo_ref[e] >= 0)
            def _():
                zcopy(e).wait()
            return carry

        lax.fori_loop(0, N_EXPERTS, start, 0)
        lax.fori_loop(0, N_EXPERTS, wait, 0)

    for n in range(tn):
        src = tok_ref.at[pl.ds(n * ROW_CHUNKS, ROW_CHUNKS)]
        for slot in range(TOP_K):
            pltpu.make_async_copy(src, xs_hbm.at[dest_ref[0, slot, n]], sem).start(priority=slot)
    for slot in range(TOP_K):
        pltpu.make_async_copy(xs_hbm.at[pl.ds(0, tn)], xs_hbm.at[pl.ds(0, tn)], sem).wait()


def _dispatch(tokens, dest3, zlo, n_rows, t_ffn):
    steps, _, tn = dest3.shape
    smem = pltpu.SMEM
    return pl.pallas_call(
        functools.partial(_dispatch_kernel, t_ffn=t_ffn),
        grid=(steps,),
        in_specs=[
            pl.BlockSpec((1, TOP_K, tn), lambda i: (i, 0, 0), memory_space=smem),
            pl.BlockSpec(memory_space=smem),
            pl.BlockSpec((tn * ROW_CHUNKS, LANES), lambda i: (i, 0)),
        ],
        out_specs=pl.BlockSpec(memory_space=pl.ANY),
        out_shape=jax.ShapeDtypeStruct((n_rows, ROW_CHUNKS, LANES), tokens.dtype),
        scratch_shapes=[pltpu.VMEM((t_ffn, ROW_CHUNKS, LANES), tokens.dtype), pltpu.SemaphoreType.DMA,
                        pltpu.SemaphoreType.DMA],
        compiler_params=_cparams(("arbitrary",)),
        name="dispatch",
    )(dest3, zlo, tokens)


def _ffn_kernel(texp_ref, nused_ref, enext_ref, eord_ref, x_ref, wg_hbm, wu_hbm, wd_hbm, y_ref,
                wgf_ref, wuf_ref, wdf_ref, wgb_ref, wub_ref, wdb_ref, wsem, *, layer):
    j = pl.program_id(0)
    e = texp_ref[j]
    active = j < nused_ref[0]
    first = (j == 0) | (e != texp_ref[jnp.maximum(j - 1, 0)])
    slot = eord_ref[e] % 2
    pairs = ((wg_hbm, wgf_ref), (wu_hbm, wuf_ref), (wd_hbm, wdf_ref))

    def wcopies(expert, s):
        return [pltpu.make_async_copy(hbm.at[layer, expert], buf.at[s], wsem.at[s, i])
                for i, (hbm, buf) in enumerate(pairs)]

    @pl.when(active & (j == 0))
    def _():
        for cp in wcopies(e, slot):
            cp.start()

    @pl.when(active & first)
    def _():
        nxt = enext_ref[e]

        @pl.when(nxt >= 0)
        def _():
            for cp in wcopies(nxt, 1 - slot):
                cp.start(priority=1)

        for cp in wcopies(e, slot):
            cp.wait()
        wgb_ref[...] = wgf_ref[slot].astype(BF16)
        wub_ref[...] = wuf_ref[slot].astype(BF16)
        wdb_ref[...] = wdf_ref[slot].astype(BF16)

    @pl.when(active)
    def _():
        x = _load_chunked(x_ref, x_ref.shape[0] // ROW_CHUNKS).astype(BF16)
        gate = _dot(x, wgb_ref[...])
        up = _dot(x, wub_ref[...])
        act = (gate * jax.nn.sigmoid(gate) * up).astype(BF16)
        _store_chunked(y_ref, _dot(act, wdb_ref[...]))


def _expert_ffn(xs, plan, w_gate, w_up, w_down, layer, t_ffn):
    tile_expert, n_used, e_next, e_ord = plan
    _, _, d, de = w_gate.shape
    n_rows = xs.shape[0] // ROW_CHUNKS
    n_tiles = n_rows // t_ffn
    rowmap = lambda j, te, nu, en, eo: (jnp.minimum(j, nu[0] - 1), 0)
    hbm = pl.BlockSpec(memory_space=pl.ANY)
    return pl.pallas_call(
        functools.partial(_ffn_kernel, layer=layer),
        grid_spec=pltpu.PrefetchScalarGridSpec(
            num_scalar_prefetch=4,
            grid=(n_tiles,),
            in_specs=[pl.BlockSpec((t_ffn * ROW_CHUNKS, LANES), rowmap), hbm, hbm, hbm],
            out_specs=pl.BlockSpec((t_ffn * ROW_CHUNKS, LANES), rowmap),
            scratch_shapes=[pltpu.VMEM((2, d, de), F32), pltpu.VMEM((2, d, de), F32), pltpu.VMEM((2, de, d), F32),
                            pltpu.VMEM((d, de), BF16), pltpu.VMEM((d, de), BF16), pltpu.VMEM((de, d), BF16),
                            pltpu.SemaphoreType.DMA((2, 3))],
        ),
        out_shape=jax.ShapeDtypeStruct((n_rows * ROW_CHUNKS, LANES), F32),
        compiler_params=_cparams(("arbitrary",)),
        name="expert_ffn",
    )(tile_expert, n_used, e_next, e_ord, xs, w_gate, w_up, w_down)


def _combine_kernel(dest_ref, dest_next_ref, y_hbm, wts_ref, x1_ref, mod_ref, lng_ref, lnb_ref, x2_ref,
                    buf_ref, sem, *, alpha):
    i = pl.program_id(0)
    n_steps = pl.num_programs(0)
    tc, d = x1_ref.shape

    def row_copy(idx_ref, parity, n, slot, rows):
        return pltpu.make_async_copy(y_hbm.at[idx_ref[0, slot, n]], buf_ref.at[parity * TOP_K + slot, rows],
                                     sem.at[parity])

    @pl.when(i == 0)
    def _():
        def body(n, carry):
            rows = pl.ds(pl.multiple_of(n * ROW_CHUNKS, ROW_CHUNKS), ROW_CHUNKS)
            for slot in range(TOP_K):
                row_copy(dest_ref, 0, n, slot, rows).start(priority=slot)
            return carry
        lax.fori_loop(0, tc, body, 0, unroll=DMA_UNROLL)

    for parity in range(2):
        @pl.when((i + 1 < n_steps) & ((i + 1) % 2 == parity))
        def _():
            for n in range(tc):
                for slot in range(TOP_K):
                    row_copy(dest_next_ref, parity, n, slot, pl.ds(n * ROW_CHUNKS, ROW_CHUNKS)).start(priority=slot)

    for slot in range(TOP_K):
        pltpu.make_async_copy(y_hbm.at[pl.ds(0, tc)], y_hbm.at[pl.ds(0, tc)], sem.at[i % 2]).wait()
    w = wts_ref[...]
    cur = (i % 2) * TOP_K
    f = w[:, 0:1] * _load_chunked(buf_ref.at[cur], tc) + w[:, 1:2] * _load_chunked(buf_ref.at[cur + 1], tc)
    g2 = mod_ref[0][:, 5 * d:6 * d]
    x2_ref[...] = _post_norm(x1_ref[...], g2, f, alpha) * lng_ref[...] + lnb_ref[...]


def _combine(y, dest3, wts_t, x1, mods, lng, lnb, *, tok0, seq_len, mod_row, alpha):
    n, d = x1.shape
    tc = dest3.shape[2]
    tps = max(seq_len // tc, 1)
    steps = n // tc
    blk0 = tok0 // tc
    mod_map = (lambda i: (i // tps, 0, 0)) if mod_row is None else (lambda i: (mod_row, 0, 0))
    cur = lambda i: (blk0 + i, 0, 0)
    nxt = lambda i: (blk0 + jnp.minimum(i + 1, steps - 1), 0, 0)
    smem = pltpu.SMEM
    return pl.pallas_call(
        functools.partial(_combine_kernel, alpha=alpha),
        grid=(steps,),
        in_specs=[
            pl.BlockSpec((1, TOP_K, tc), cur, memory_space=smem),
            pl.BlockSpec((1, TOP_K, tc), nxt, memory_space=smem),
            pl.BlockSpec(memory_space=pl.ANY),
            pl.BlockSpec((tc, TOP_K), lambda i: (blk0 + i, 0)),
            pl.BlockSpec((tc, d), lambda i: (i, 0)),
            pl.BlockSpec((1, 1, mods.shape[2]), mod_map),
            pl.BlockSpec(lng.shape, lambda i: (0, 0)),
            pl.BlockSpec(lnb.shape, lambda i: (0, 0)),
        ],
        out_specs=pl.BlockSpec((tc, d), lambda i: (i, 0)),
        out_shape=jax.ShapeDtypeStruct((n, d), F32),
        scratch_shapes=[pltpu.VMEM((2 * TOP_K, tc * ROW_CHUNKS, LANES), F32), pltpu.SemaphoreType.DMA((2,))],
        compiler_params=_cparams(("arbitrary",)),
        name="combine",
    )(dest3, dest3, y, wts_t, x1, mods, lng, lnb)


def _rope_tables(seq_len):
    t = jnp.arange(seq_len)
    row = (t // GRID_W).astype(F32)
    col = (t % GRID_W).astype(F32)
    half = HEAD_DIM // 2
    inv_freq = ROPE_BASE ** (-jnp.arange(0, half, 2, dtype=F32) / half)
    ang_r, ang_c = row[:, None] * inv_freq, col[:, None] * inv_freq
    cr, sr, cc, sc = jnp.cos(ang_r), jnp.sin(ang_r), jnp.cos(ang_c), jnp.sin(ang_c)
    cos = jnp.concatenate([cr, cr, cc, cc], axis=1)
    sin = jnp.concatenate([-sr, sr, -sc, sc], axis=1)
    return jnp.tile(cos, (1, LANES // HEAD_DIM)), jnp.tile(sin, (1, LANES // HEAD_DIM))


def _pair_heads(a, axis):
    shp = a.shape
    a = a.reshape(*shp[:axis], N_KV_HEADS, Q_REP, HEAD_DIM, *shp[axis + 1:])
    a = jnp.swapaxes(a, axis, axis + 1)
    return a.reshape(shp)


def _moe_plan(eid3, rank3, counts, n_tok, t_ffn):
    counts = counts.reshape(EXPERTS_PER_GROUP, N_EXPERT_GROUPS).T.reshape(N_EXPERTS).astype(jnp.int32)
    padded = (counts + t_ffn - 1) // t_ffn * t_ffn
    pends = jnp.cumsum(padded)
    pstarts = (pends - padded).astype(jnp.int32)
    experts = jnp.arange(N_EXPERTS, dtype=jnp.int32)
    dest3 = rank3 + jnp.sum(jnp.where(eid3[..., None] == experts, pstarts, 0), axis=-1)
    n_rows = -(-(n_tok * TOP_K) // t_ffn) * t_ffn + N_EXPERTS * t_ffn
    n_tiles = n_rows // t_ffn
    tile_row0 = jnp.arange(n_tiles, dtype=jnp.int32) * t_ffn
    tile_expert = jnp.minimum(jnp.sum(pends[None, :] <= tile_row0[:, None], axis=1), N_EXPERTS - 1)
    n_used = (pends[-1] // t_ffn).astype(jnp.int32).reshape(1)
    nonempty = counts > 0
    later = (experts[None, :] > experts[:, None]) & nonempty[None, :]
    e_next = jnp.min(jnp.where(later, experts[None, :], N_EXPERTS), axis=1)
    e_next = jnp.where(e_next < N_EXPERTS, e_next, -1).astype(jnp.int32)
    e_ord = (jnp.cumsum(nonempty.astype(jnp.int32)) - 1).astype(jnp.int32)
    zlo = jnp.where(nonempty, pends - t_ffn, -1).astype(jnp.int32)
    plan = (tile_expert.astype(jnp.int32), n_used, e_next, e_ord)
    return dest3.astype(jnp.int32), plan, zlo, n_rows


def kernel(x, c, ctx, c_ctx, w_ada, b_ada, w_in, w_pool_grp, pool_scale, w_pool_br, w_attn_br, attn_sink,
           w_o, ln1_g, ln1_b, w_router, router_bias, w_exp_gate, w_exp_up, w_exp_down, ln2_g, ln2_b):
    bsz, seq, d = x.shape
    assert d == ROW_CHUNKS * LANES and bsz < SUBLANES
    c_len = ctx.shape[1]
    depth = w_in.shape[0]
    n_lat, n_ctx = bsz * seq, bsz * c_len
    alpha = (2 * depth) ** 0.25

    cond = jnp.zeros((SUBLANES, d), F32).at[:bsz].set(c).at[bsz].set(c_ctx)
    ada = _ada_terms(cond, w_ada, b_ada)
    cos, sin = _rope_tables(seq)
    wr_t = w_router.reshape(d, N_EXPERT_GROUPS, EXPERTS_PER_GROUP).transpose(2, 1, 0).reshape(N_EXPERTS, d)
    wr_t = wr_t.astype(BF16)
    bias_col = router_bias.reshape(N_EXPERT_GROUPS, EXPERTS_PER_GROUP).T.reshape(N_EXPERTS, 1).astype(F32)

    xl = x.reshape(n_lat, d)
    xc = ctx.reshape(n_ctx, d)
    for l in range(depth):
        ctx_out = l < depth - 1
        mods = ada[l].reshape(SUBLANES, 1, 6 * d)
        w_l = w_in[l]
        w_inp = jnp.concatenate([w_l[:, :COL_Q], _pair_heads(w_l[:, COL_Q:COL_K], 1), w_l[:, COL_K:]],
                                axis=1).astype(BF16)
        sink_col = jnp.broadcast_to(attn_sink[l].reshape(N_KV_HEADS, Q_REP, 1, 1),
                                    (N_KV_HEADS, Q_REP, BLOCK, 1)).reshape(N_KV_HEADS, Q_REP * BLOCK, 1)
        sink_col = (sink_col * LOG2E).astype(F32)
        mix_w = (w_pool_grp[l].astype(BF16), pool_scale[l].reshape(1, POOL_W), w_pool_br[l].astype(BF16),
                 _pair_heads(w_attn_br[l], 0).astype(BF16), w_o[l].astype(BF16),
                 ln1_g[l].reshape(1, d), ln1_b[l].reshape(1, d), wr_t)
        lng2, lnb2 = ln2_g[l].reshape(1, d), ln2_b[l].reshape(1, d)

        if ctx_out:
            uc, qc, kc, vc, gc = _inproj(xc, mods, w_inp, cos, sin, seq_len=c_len, mod_row=bsz, rope=False)
        else:
            kc, vc = _inproj(xc, mods, w_inp, cos, sin, seq_len=c_len, mod_row=bsz, rope=False, kv_only=True)
        kc3, vc3 = kc.reshape(bsz, c_len, KV_W), vc.reshape(bsz, c_len, KV_W)
        u, q, k, v, g = _inproj(xl, mods, w_inp, cos, sin, seq_len=seq, mod_row=None, rope=True)
        n_tok = n_lat + n_ctx if ctx_out else n_lat
        tokens = None
        if ctx_out:
            attn_c = _attention(qc, kc, vc, kc3, vc3, sink_col, seq_len=c_len, local=False)
            xc1, tokens, logits_c = _merge(uc, attn_c, gc, xc, mods, mix_w, seq_len=c_len, mod_row=bsz,
                                           alpha=alpha, h2_tokens=n_tok, h2_tok0=n_lat)
        attn = _attention(q, k, v, kc3, vc3, sink_col, seq_len=seq, local=True)
        x1, tokens, logits = _merge(u, attn, g, xl, mods, mix_w, seq_len=seq, mod_row=None, alpha=alpha,
                                    h2_tokens=n_tok, h2_tok0=0, h2_buf=tokens)
        if ctx_out:
            logits = jnp.concatenate([logits, logits_c], axis=1)

        eid3, wts, rank3, counts = _route(logits, bias_col)
        dest3, plan, zlo, n_rows = _moe_plan(eid3, rank3, counts[:, 0], n_tok, T_FFN)
        xs = _dispatch(tokens, dest3, zlo, n_rows, T_FFN)
        y = _expert_ffn(xs.reshape(n_rows * ROW_CHUNKS, LANES), plan, w_exp_gate, w_exp_up, w_exp_down, l, T_FFN)
        y3 = y.reshape(n_rows, ROW_CHUNKS, LANES)
        wts_t = wts.T
        xl = _combine(y3, dest3, wts_t, x1, mods, lng2, lnb2, tok0=0, seq_len=seq, mod_row=None, alpha=alpha)
        if ctx_out:
            xc = _combine(y3, dest3, wts_t, xc1, mods, lng2, lnb2, tok0=n_lat, seq_len=c_len, mod_row=bsz,
                          alpha=alpha)
    return xl.reshape(bsz, seq, d)
```

```python
import functools

import jax
import jax.numpy as jnp
from jax import lax
from jax.experimental import pallas as pl
from jax.experimental.pallas import tpu as pltpu

F32 = jnp.float32
BF16 = jnp.bfloat16

GRID_W = 64
POOL_WINDOWS = (2, 4, 8, 16)
POOL_GROUP_W = 128
POOL_W = 512
HEAD_DIM = 64
N_HEADS = 8
N_KV_HEADS = 2
Q_REP = N_HEADS // N_KV_HEADS
ATTN_W = N_HEADS * HEAD_DIM
KV_W = N_KV_HEADS * HEAD_DIM
BLOCK = 128
ROPE_BASE = 10000.0
COL_POOL = 0
COL_Q = COL_POOL + POOL_W
COL_K = COL_Q + ATTN_W
COL_V = COL_K + KV_W
COL_GATE = COL_V + KV_W
N_EXPERTS = 32
N_EXPERT_GROUPS = 8
EXPERTS_PER_GROUP = N_EXPERTS // N_EXPERT_GROUPS
TOP_K = 2
LN_EPS = 1e-6
LOG2E = 1.4426950408889634

LANES = 128
SUBLANES = 8
POOL_HALO = SUBLANES
ROW_CHUNKS = SUBLANES
VMEM_LIMIT = 56 * 1024 * 1024

ADA_COL_TILES = 4
TM_IN = 1024
TQ_ATT = 2048
TQ_MIX = 512
TN_TOK = 512
T_FFN = 256
DMA_UNROLL = 8


def _cparams(sem):
    return pltpu.CompilerParams(dimension_semantics=sem, vmem_limit_bytes=VMEM_LIMIT)


def _layer_norm(x, eps=LN_EPS):
    mu = jnp.mean(x, axis=-1, keepdims=True)
    xc = x - mu
    var = jnp.mean(xc * xc, axis=-1, keepdims=True)
    return xc * lax.rsqrt(var + eps)


def _post_norm(x, gate, y, alpha):
    return _layer_norm(x + (gate * (1.0 / alpha)) * y, LN_EPS / (alpha * alpha))


def _dot(a, b):
    return jnp.dot(a, b, preferred_element_type=F32)


def _store_chunked(ref, val):
    t = val.shape[0]
    for s in range(ROW_CHUNKS):
        ref[pl.ds(s, t, stride=ROW_CHUNKS), :] = val[:, s * LANES:(s + 1) * LANES]


def _load_chunked(ref, t):
    return jnp.concatenate([ref[pl.ds(s, t, stride=ROW_CHUNKS), :] for s in range(ROW_CHUNKS)], axis=1)


def _ada_kernel(cond_ref, w_ref, b_ref, o_ref):
    s = cond_ref[...]
    s = s * jax.nn.sigmoid(s)
    o_ref[0] = _dot(s.astype(BF16), w_ref[0].astype(BF16)) + b_ref[0]


def _ada_terms(cond, w_ada, b_ada):
    depth, d, n6 = w_ada.shape
    rows = cond.shape[0]
    tn = n6 // ADA_COL_TILES
    return pl.pallas_call(
        _ada_kernel,
        grid=(depth, ADA_COL_TILES),
        in_specs=[
            pl.BlockSpec((rows, d), lambda l, j: (0, 0)),
            pl.BlockSpec((1, d, tn), lambda l, j: (l, 0, j)),
            pl.BlockSpec((1, 1, tn), lambda l, j: (l, 0, j)),
        ],
        out_specs=pl.BlockSpec((1, rows, tn), lambda l, j: (l, 0, j)),
        out_shape=jax.ShapeDtypeStruct((depth, rows, n6), F32),
        compiler_params=_cparams(("arbitrary", "arbitrary")),
        name="ada_terms",
    )(cond, w_ada, b_ada.reshape(depth, 1, n6))


def _rope(t, cos, sin):
    lane = lax.broadcasted_iota(jnp.int32, (1, LANES), 1)
    first = (lane % 32) < 16
    outs = []
    for j in range(t.shape[1] // LANES):
        tj = t[:, j * LANES:(j + 1) * LANES]
        partner = jnp.where(first, pltpu.roll(tj, LANES - 16, 1), pltpu.roll(tj, 16, 1))
        outs.append(tj * cos + partner * sin)
    return outs[0] if len(outs) == 1 else jnp.concatenate(outs, axis=1)


def _inproj_kernel(x_ref, mod_ref, w_ref, cos_ref, sin_ref, *out_refs, rope, kv_only):
    d = x_ref.shape[1]
    mod = mod_ref[0]
    shift, scale = mod[:, 0:d], mod[:, d:2 * d]
    h = (_layer_norm(x_ref[...]) * (1.0 + scale) + shift).astype(BF16)

    def proj(lo, hi):
        return _dot(h, w_ref[:, lo:hi])

    if kv_only:
        k_ref, v_ref = out_refs
    else:
        u_ref, q_ref, k_ref, v_ref, g_ref = out_refs
        u_ref[...] = proj(COL_POOL, COL_Q)
        q = proj(COL_Q, COL_K)
        if rope:
            q = _rope(q, cos_ref[...], sin_ref[...])
        q_ref[...] = (q * (LOG2E * HEAD_DIM ** -0.5)).astype(BF16)
        g_ref[...] = jax.nn.sigmoid(proj(COL_GATE, w_ref.shape[1])).astype(g_ref.dtype)
    kv = proj(COL_K, COL_GATE)
    k = kv[:, 0:KV_W]
    if rope:
        k = _rope(k, cos_ref[...], sin_ref[...])
    k_ref[...] = k.astype(BF16)
    v_ref[...] = kv[:, KV_W:2 * KV_W].astype(BF16)


def _inproj(x2d, mods, w_in, cos, sin, *, seq_len, mod_row, rope, kv_only=False):
    n, d = x2d.shape
    tm = min(TM_IN, seq_len)
    tps = seq_len // tm
    n_cols = w_in.shape[1]
    mod_map = (lambda i: (i // tps, 0, 0)) if mod_row is None else (lambda i: (mod_row, 0, 0))
    tab_map = (lambda i: (i % tps, 0)) if rope else (lambda i: (0, 0))
    row = lambda i: (i, 0)
    kv_shapes = [jax.ShapeDtypeStruct((n, KV_W), BF16)] * 2
    kv_specs = [pl.BlockSpec((tm, KV_W), row)] * 2
    if kv_only:
        out_shape, out_specs = kv_shapes, kv_specs
    else:
        out_shape = [jax.ShapeDtypeStruct((n, POOL_W), F32), jax.ShapeDtypeStruct((n, ATTN_W), BF16),
                     *kv_shapes, jax.ShapeDtypeStruct((n, n_cols - COL_GATE), BF16)]
        out_specs = [pl.BlockSpec((tm, POOL_W), row), pl.BlockSpec((tm, ATTN_W), row),
                     *kv_specs, pl.BlockSpec((tm, n_cols - COL_GATE), row)]
    return pl.pallas_call(
        functools.partial(_inproj_kernel, rope=rope, kv_only=kv_only),
        grid=(n // tm,),
        in_specs=[
            pl.BlockSpec((tm, d), row),
            pl.BlockSpec((1, 1, mods.shape[2]), mod_map),
            pl.BlockSpec((d, n_cols), lambda i: (0, 0)),
            pl.BlockSpec((tm, LANES), tab_map),
            pl.BlockSpec((tm, LANES), tab_map),
        ],
        out_specs=out_specs,
        out_shape=out_shape,
        compiler_params=_cparams(("parallel",)),
        name="inproj_kv" if kv_only else "inproj",
    )(x2d, mods, w_in, cos, sin)


def _merge_kernel(u_prev_ref, u_ref, u_next_ref, attn_ref, g_ref, x_ref, mod_ref,
                  wgrp_ref, pscale_ref, wpool_ref, wattn_ref, wo_ref, lng_ref, lnb_ref, wr_ref,
                  *rest, seq_len, alpha, n_tiles):
    x1_ref, h2_ref, logit_ref, uext_ref, m0_ref, m1_ref = rest[-6:]
    tq, d = x_ref.shape
    tps = seq_len // tq
    j = pl.program_id(0)
    t_in_seq = jnp.minimum(j, n_tiles - 1) % tps
    is_first = t_in_seq == 0
    is_last = t_in_seq == tps - 1
    n_groups = len(POOL_WINDOWS)
    cw = d // n_groups
    row_chunks = tq // BLOCK

    def a_fill_window():
        top = 2 * POOL_HALO
        uext_ref[0:POOL_HALO, :] = jnp.zeros((POOL_HALO, POOL_W), F32)
        uext_ref[POOL_HALO:top, :] = jnp.where(is_first, 0.0, u_prev_ref[...])
        uext_ref[top:top + tq, :] = u_ref[...]
        uext_ref[top + tq:top + tq + POOL_HALO, :] = jnp.where(is_last, 0.0, u_next_ref[...])
        uext_ref[top + tq + POOL_HALO:, :] = jnp.zeros((BLOCK - top - POOL_HALO, POOL_W), F32)

    def a_pool_group(gi):
        w = POOL_WINDOWS[gi]
        top = 2 * POOL_HALO
        cols = slice(gi * POOL_GROUP_W, (gi + 1) * POOL_GROUP_W)
        ue = uext_ref[:, cols]
        ue_hi = ue.astype(BF16)
        ue_lo = (ue - ue_hi.astype(F32)).astype(BF16)
        row_t = lax.broadcasted_iota(jnp.int32, (BLOCK, 2 * BLOCK), 0)
        col_j = lax.broadcasted_iota(jnp.int32, (BLOCK, 2 * BLOCK), 1)
        first_j = row_t + top - w // 2
        band = jnp.where((col_j >= first_j) & (col_j < first_j + w), 1.0, 0.0).astype(BF16)
        sums = []
        for b in range(row_chunks):
            win = slice(b * BLOCK, (b + 2) * BLOCK)
            sums.append(_dot(band, ue_hi[win]) + _dot(band, ue_lo[win]))
        acc = sums[0] if len(sums) == 1 else jnp.concatenate(sums, axis=0)
        pos = t_in_seq * tq + lax.broadcasted_iota(jnp.int32, (tq, 1), 0)
        lo = jnp.maximum(pos - w // 2, 0)
        hi = jnp.minimum(pos - w // 2 + w - 1, seq_len - 1)
        mean = acc / (hi - lo + 1).astype(F32)
        pg = (mean - u_ref[:, cols]).astype(BF16)
        return _dot(pg, wgrp_ref[gi])

    def a_merge_chunk(c, pool_lat, m_buf):
        cols = slice(c * cw, (c + 1) * cw)
        pool_proj = _dot(pool_lat, wpool_ref[:, cols])
        attn_proj = _dot(attn_ref[...], wattn_ref[:, cols])
        m_buf[:, cols] = (g_ref[:, c * cw:(c + 1) * cw].astype(F32) * pool_proj
                          + g_ref[:, d + c * cw:d + (c + 1) * cw].astype(F32) * attn_proj).astype(BF16)

    def b_project_chunk(c, m_buf):
        return _dot(m_buf[...], wo_ref[:, c * cw:(c + 1) * cw])

    def b_norm_rows(r, ys):
        rows = slice(r * BLOCK, (r + 1) * BLOCK)
        y = jnp.concatenate([yc[rows] for yc in ys], axis=1)
        mod = mod_ref[0]
        g1 = mod[:, 2 * d:3 * d]
        sh2, sc2 = mod[:, 3 * d:4 * d], mod[:, 4 * d:5 * d]
        x1 = _post_norm(x_ref[rows, :], g1, y, alpha) * lng_ref[...] + lnb_ref[...]
        x1_ref[rows, :] = x1
        h2 = _layer_norm(x1) * (1.0 + sc2) + sh2
        _store_chunked(h2_ref.at[r * BLOCK * ROW_CHUNKS:(r + 1) * BLOCK * ROW_CHUNKS], h2)
        logit_ref[:, rows] = lax.dot_general(wr_ref[...], h2.astype(BF16), (((1,), (1,)), ((), ())),
                                             preferred_element_type=F32)

    def run(a_buf, b_buf):
        if a_buf is not None:
            a_fill_window()
        ys, pooled = [], []
        for c in range(n_groups):
            if b_buf is not None:
                ys.append(b_project_chunk(c, b_buf))
            if a_buf is not None:
                pooled.append(a_pool_group(c))
        if a_buf is not None:
            pool_lat = (jnp.concatenate(pooled, axis=1) * pscale_ref[...]).astype(BF16)
        for c in range(n_groups):
            if a_buf is not None:
                a_merge_chunk(c, pool_lat, a_buf)
            if b_buf is not None and c < row_chunks:
                b_norm_rows(c, ys)

    bufs = (m0_ref, m1_ref)

    @pl.when(j == 0)
    def _():
        run(bufs[0], None)

    for parity in range(2):
        @pl.when((j >= 1) & (j < n_tiles) & (j % 2 == parity))
        def _():
            run(bufs[parity], bufs[1 - parity])

    @pl.when(j == n_tiles)
    def _():
        run(None, bufs[(n_tiles - 1) % 2])


def _attn_kernel(q_ref, k_prev_ref, k_ref, k_next_ref, v_prev_ref, v_ref, v_next_ref, kc_ref, vc_ref,
                 sink_ref, attn_ref, s0_ref, s1_ref, kg_ref, vg_ref, *, seq_len, local):
    tq = q_ref.shape[0]
    nb = tq // BLOCK
    tps = seq_len // tq
    t_in_seq = pl.program_id(0) % tps
    is_first = t_in_seq == 0
    is_last = t_in_seq == tps - 1
    lane = lax.broadcasted_iota(jnp.int32, (1, LANES), 1)
    lo_half = lane < HEAD_DIM
    one = jnp.ones((), BF16)
    kc = kc_ref[0]
    vc = vc_ref[0]
    kc_g = [jnp.where(lo_half, kc, 0), jnp.where(lo_half, 0, kc)]
    vc_g = [jnp.where(lo_half, vc, one), jnp.where(lo_half, one, vc)]
    neg = jnp.float32(-jnp.inf)
    if local:
        k_ext = jnp.concatenate([k_prev_ref[...], k_ref[...], k_next_ref[...]], axis=0)
        v_ext = jnp.concatenate([v_prev_ref[...], v_ref[...], v_next_ref[...]], axis=0)
        kg_ref[0] = jnp.where(lo_half, k_ext, 0)
        kg_ref[1] = jnp.where(lo_half, 0, k_ext)
        vg_ref[0] = jnp.where(lo_half, v_ext, one)
        vg_ref[1] = jnp.where(lo_half, one, v_ext)
        qq = lax.broadcasted_iota(jnp.int32, (Q_REP * BLOCK, BLOCK), 0) % BLOCK
        kk = lax.broadcasted_iota(jnp.int32, (Q_REP * BLOCK, BLOCK), 1)
        mask_prev = jnp.where(kk >= qq, 0.0, neg)
        mask_next = jnp.where(kk <= qq, 0.0, neg)

    def block_rows(b):
        return pl.ds(b * BLOCK, BLOCK) if isinstance(b, int) else pl.ds(pl.multiple_of(b * BLOCK, BLOCK), BLOCK)

    def band_keys(b):
        start = b * BLOCK if isinstance(b, int) else pl.multiple_of(b * BLOCK, BLOCK)
        return pl.ds(start, 3 * BLOCK)

    def stage_a(b, s_buf, first_block, last_block):
        q_st = jnp.concatenate([q_ref[block_rows(b), c * LANES:(c + 1) * LANES] for c in range(Q_REP)], axis=0)
        for g in range(N_KV_HEADS):
            k_all = jnp.concatenate([kg_ref[g, band_keys(b), :], kc_g[g]], axis=0) if local else kc_g[g]
            s = lax.dot_general(q_st, k_all, (((1,), (1,)), ((), ())), preferred_element_type=F32)
            if local:
                m_prev = jnp.where(first_block, neg, mask_prev)
                m_next = jnp.where(last_block, neg, mask_next)
                s = jnp.concatenate([s[:, 0:BLOCK] + m_prev, s[:, BLOCK:2 * BLOCK],
                                     s[:, 2 * BLOCK:3 * BLOCK] + m_next, s[:, 3 * BLOCK:]], axis=1)
            s_buf[g] = s

    def stage_b(b, s_buf):
        pv, sink_w = [], []
        for g in range(N_KV_HEADS):
            v_all = jnp.concatenate([vg_ref[g, band_keys(b), :], vc_g[g]], axis=0) if local else vc_g[g]
            s = s_buf[g]
            sk = sink_ref[g]
            m = jnp.maximum(jnp.max(s, axis=-1, keepdims=True), sk)
            p = jnp.exp2((s - m).astype(BF16))
            pv.append(_dot(p, v_all))
            sink_w.append(jnp.exp2(sk - m))
        num = jnp.where(lo_half, pv[0], pv[1])
        den = pltpu.roll(jnp.where(lo_half, pv[1], pv[0]), HEAD_DIM, 1) + jnp.where(lo_half, sink_w[0], sink_w[1])
        o = num / den
        for c in range(Q_REP):
            attn_ref[block_rows(b), c * LANES:(c + 1) * LANES] = o[c * BLOCK:(c + 1) * BLOCK].astype(BF16)

    stage_a(0, s0_ref, is_first, False)

    def body(j, carry):
        stage_a(2 * j + 1, s1_ref, False, False)
        stage_b(2 * j, s0_ref)
        stage_a(2 * j + 2, s0_ref, False, False)
        stage_b(2 * j + 1, s1_ref)
        return carry

    lax.fori_loop(0, nb // 2 - 1, body, 0)
    stage_a(nb - 1, s1_ref, False, is_last)
    stage_b(nb - 2, s0_ref)
    stage_b(nb - 1, s1_ref)


def _attention(q, k, v, kc, vc, sink_col, *, seq_len, local):
    n = q.shape[0]
    tq = min(TQ_ATT, seq_len)
    tps = seq_len // tq
    kb = tq // BLOCK
    n_kb = n // BLOCK
    c_len = kc.shape[1]
    n_keys = (3 * BLOCK if local else 0) + c_len
    row = lambda i: (i, 0)
    ctx_map = lambda i: (i // tps, 0, 0)
    kv_prev = pl.BlockSpec((BLOCK, KV_W), lambda i: (jnp.maximum(i * kb - 1, 0), 0))
    kv_cur = pl.BlockSpec((tq, KV_W), row)
    kv_next = pl.BlockSpec((BLOCK, KV_W), lambda i: (jnp.minimum((i + 1) * kb, n_kb - 1), 0))
    return pl.pallas_call(
        functools.partial(_attn_kernel, seq_len=seq_len, local=local),
        grid=(n // tq,),
        in_specs=[
            pl.BlockSpec((tq, ATTN_W), row),
            kv_prev, kv_cur, kv_next, kv_prev, kv_cur, kv_next,
            pl.BlockSpec((1, c_len, KV_W), ctx_map), pl.BlockSpec((1, c_len, KV_W), ctx_map),
            pl.BlockSpec(sink_col.shape, lambda i: (0, 0, 0)),
        ],
        out_specs=pl.BlockSpec((tq, ATTN_W), row),
        out_shape=jax.ShapeDtypeStruct((n, ATTN_W), BF16),
        scratch_shapes=[pltpu.VMEM((N_KV_HEADS, Q_REP * BLOCK, n_keys), F32),
                        pltpu.VMEM((N_KV_HEADS, Q_REP * BLOCK, n_keys), F32),
                        pltpu.VMEM((N_KV_HEADS, tq + 2 * BLOCK, KV_W), BF16),
                        pltpu.VMEM((N_KV_HEADS, tq + 2 * BLOCK, KV_W), BF16)],
        compiler_params=_cparams(("parallel",)),
        name="attention" if local else "attention_ctx",
    )(q, k, k, k, v, v, v, kc, vc, sink_col)


def _merge(u, attn, gates, x2d, mods, wts, *, seq_len, mod_row, alpha, h2_tokens, h2_tok0, h2_buf=None):
    n, d = x2d.shape
    tq = min(TQ_MIX, seq_len)
    h2_blk0 = h2_tok0 // tq
    tps = seq_len // tq
    hb = tq // POOL_HALO
    n_hb = n // POOL_HALO
    n_tiles = n // tq
    ta = lambda i: jnp.minimum(i, n_tiles - 1)
    tb = lambda i: jnp.maximum(i - 1, 0)
    row_a = lambda i: (ta(i), 0)
    row_b = lambda i: (tb(i), 0)
    const2 = lambda i: (0, 0)
    const3 = lambda i: (0, 0, 0)
    mod_map = (lambda i: (tb(i) // tps, 0, 0)) if mod_row is None else (lambda i: (mod_row, 0, 0))
    u_prev = pl.BlockSpec((POOL_HALO, POOL_W), lambda i: (jnp.maximum(ta(i) * hb - 1, 0), 0))
    u_next = pl.BlockSpec((POOL_HALO, POOL_W), lambda i: (jnp.minimum((ta(i) + 1) * hb, n_hb - 1), 0))
    wgrp, pscale, wpool, wattn, wo, lng, lnb, wr_t = wts
    operands = [u, u, u, attn, gates, x2d, mods, wgrp, pscale, wpool, wattn, wo, lng, lnb, wr_t]
    alias_specs, aliases = [], {}
    if h2_buf is not None:
        alias_specs = [pl.BlockSpec(memory_space=pl.ANY)]
        aliases = {len(operands): 1}
        operands.append(h2_buf)
    return pl.pallas_call(
        functools.partial(_merge_kernel, seq_len=seq_len, alpha=alpha, n_tiles=n_tiles),
        grid=(n_tiles + 1,),
        input_output_aliases=aliases,
        in_specs=[
            u_prev, pl.BlockSpec((tq, POOL_W), row_a), u_next,
            pl.BlockSpec((tq, ATTN_W), row_a),
            pl.BlockSpec((tq, 2 * d), row_a),
            pl.BlockSpec((tq, d), row_b),
            pl.BlockSpec((1, 1, mods.shape[2]), mod_map),
            pl.BlockSpec(wgrp.shape, const3), pl.BlockSpec(pscale.shape, const2),
            pl.BlockSpec(wpool.shape, const2), pl.BlockSpec(wattn.shape, const2),
            pl.BlockSpec(wo.shape, const2),
            pl.BlockSpec(lng.shape, const2), pl.BlockSpec(lnb.shape, const2),
            pl.BlockSpec(wr_t.shape, const2),
            *alias_specs,
        ],
        out_specs=[pl.BlockSpec((tq, d), row_b),
                   pl.BlockSpec((tq * ROW_CHUNKS, LANES), lambda i: (h2_blk0 + tb(i), 0)),
                   pl.BlockSpec((N_EXPERTS, tq), lambda i: (0, tb(i)))],
        out_shape=[jax.ShapeDtypeStruct((n, d), F32),
                   jax.ShapeDtypeStruct((h2_tokens * ROW_CHUNKS, LANES), F32),
                   jax.ShapeDtypeStruct((N_EXPERTS, n), F32)],
        scratch_shapes=[pltpu.VMEM((tq + BLOCK, POOL_W), F32), pltpu.VMEM((tq, d), BF16),
                        pltpu.VMEM((tq, d), BF16)],
        compiler_params=_cparams(("arbitrary",)),
        name="merge",
    )(*operands)


def _route_kernel(logit_ref, bias_ref, eid_ref, wts_ref, rank_ref, cnt_ref, base_ref):
    tn = logit_ref.shape[1]
    ng, epg = N_EXPERT_GROUPS, EXPERTS_PER_GROUP

    @pl.when(pl.program_id(0) == 0)
    def _():
        base_ref[...] = jnp.zeros_like(base_ref)

    scores = jax.nn.sigmoid(logit_ref[...])
    biased = scores + bias_ref[...]
    bj = [biased[j * ng:(j + 1) * ng] for j in range(epg)]
    sj = [scores[j * ng:(j + 1) * ng] for j in range(epg)]
    hi01, lo01 = jnp.maximum(bj[0], bj[1]), jnp.minimum(bj[0], bj[1])
    hi23, lo23 = jnp.maximum(bj[2], bj[3]), jnp.minimum(bj[2], bj[3])
    gscore = jnp.maximum(hi01, hi23) + jnp.maximum(jnp.minimum(hi01, hi23), jnp.maximum(lo01, lo23))
    giota = lax.broadcasted_iota(jnp.int32, (ng, tn), 0)
    gmax = jnp.max(gscore, axis=0, keepdims=True)
    g_first = jnp.min(jnp.where(gscore == gmax, giota.astype(F32), float(ng)), axis=0, keepdims=True)
    g_sel = g_first.astype(jnp.int32)
    in_g = giota == g_sel
    vb = [jnp.sum(jnp.where(in_g, b, 0.0), axis=0, keepdims=True) for b in bj]
    vs = [jnp.sum(jnp.where(in_g, s, 0.0), axis=0, keepdims=True) for s in sj]

    def first_best(vals):
        best = functools.reduce(jnp.maximum, vals)
        idx = jnp.full(best.shape, epg - 1, jnp.int32)
        for j in range(epg - 2, -1, -1):
            idx = jnp.where(vals[j] == best, j, idx)
        return idx

    def pick(vals, idx):
        out = vals[epg - 1]
        for j in range(epg - 2, -1, -1):
            out = jnp.where(idx == j, vals[j], out)
        return out

    l1 = first_best(vb)
    l2 = first_best([jnp.where(l1 == j, -jnp.inf, vb[j]) for j in range(epg)])
    w1, w2 = pick(vs, l1), pick(vs, l2)
    wsum = w1 + w2
    eid_ref[0, 0:1, :] = g_sel * epg + l1
    eid_ref[0, 1:2, :] = g_sel * epg + l2
    wts_ref[0:1, :] = w1 / wsum
    wts_ref[1:2, :] = w2 / wsum

    r1, r2 = l1 * ng + g_sel, l2 * ng + g_sel
    riota = lax.broadcasted_iota(jnp.int32, (N_EXPERTS, tn), 0)
    hit1, hit2 = riota == r1, riota == r2
    onehot = jnp.where(hit1 | hit2, 1.0, 0.0)
    before = lax.broadcasted_iota(jnp.int32, (tn, tn), 0) < lax.broadcasted_iota(jnp.int32, (tn, tn), 1)
    prefix = _dot(onehot.astype(BF16), jnp.where(before, 1.0, 0.0).astype(BF16)) + base_ref[:, 0:1]
    rank_ref[0, 0:1, :] = jnp.sum(jnp.where(hit1, prefix, 0.0), axis=0, keepdims=True).astype(jnp.int32)
    rank_ref[0, 1:2, :] = jnp.sum(jnp.where(hit2, prefix, 0.0), axis=0, keepdims=True).astype(jnp.int32)
    base_ref[...] = base_ref[...] + jnp.sum(onehot, axis=1, keepdims=True)
    cnt_ref[...] = base_ref[...]


def _route(logits, bias_col):
    n = logits.shape[1]
    tn = TN_TOK
    steps = n // tn
    col = lambda i: (0, i)
    blk = lambda i: (i, 0, 0)
    return pl.pallas_call(
        _route_kernel,
        grid=(steps,),
        in_specs=[
            pl.BlockSpec((N_EXPERTS, tn), col),
            pl.BlockSpec((N_EXPERTS, 1), lambda i: (0, 0)),
        ],
        out_specs=[pl.BlockSpec((1, TOP_K, tn), blk), pl.BlockSpec((TOP_K, tn), col),
                   pl.BlockSpec((1, TOP_K, tn), blk), pl.BlockSpec((N_EXPERTS, LANES), lambda i: (0, 0))],
        out_shape=[jax.ShapeDtypeStruct((steps, TOP_K, tn), jnp.int32), jax.ShapeDtypeStruct((TOP_K, n), F32),
                   jax.ShapeDtypeStruct((steps, TOP_K, tn), jnp.int32),
                   jax.ShapeDtypeStruct((N_EXPERTS, LANES), F32)],
        scratch_shapes=[pltpu.VMEM((N_EXPERTS, LANES), F32)],
        compiler_params=_cparams(("arbitrary",)),
        name="route",
    )(logits, bias_col)


def _dispatch_kernel(dest_ref, zlo_ref, tok_ref, xs_hbm, zero_ref, zsem, sem, *, t_ffn):
    i = pl.program_id(0)
    tn = dest_ref.shape[2]

    @pl.when(i == 0)
    def _():
        zero_ref[...] = jnp.zeros_like(zero_ref)

        def zcopy(e):
            return pltpu.make_async_copy(zero_ref, xs_hbm.at[pl.ds(jnp.maximum(zlo_ref[e], 0), t_ffn)], zsem)

        def start(e, carry):
            @pl.when(zlo_ref[e] >= 0)
            def _():
                zcopy(e).start()
            return carry

        def wait(e, carry):
            @pl.when(zlo_ref[e] >= 0)
            def _():
                zcopy(e).wait()
            return carry

        lax.fori_loop(0, N_EXPERTS, start, 0)
        lax.fori_loop(0, N_EXPERTS, wait, 0)

    for n in range(tn):
        src = tok_ref.at[pl.ds(n * ROW_CHUNKS, ROW_CHUNKS)]
        for slot in range(TOP_K):
            pltpu.make_async_copy(src, xs_hbm.at[dest_ref[0, slot, n]], sem).start(priority=slot)
    for slot in range(TOP_K):
        pltpu.make_async_copy(xs_hbm.at[pl.ds(0, tn)], xs_hbm.at[pl.ds(0, tn)], sem).wait()


def _dispatch(tokens, dest3, zlo, n_rows, t_ffn):
    steps, _, tn = dest3.shape
    smem = pltpu.SMEM
    return pl.pallas_call(
        functools.partial(_dispatch_kernel, t_ffn=t_ffn),
        grid=(steps,),
        in_specs=[
            pl.BlockSpec((1, TOP_K, tn), lambda i: (i, 0, 0), memory_space=smem),
            pl.BlockSpec(memory_space=smem),
            pl.BlockSpec((tn * ROW_CHUNKS, LANES), lambda i: (i, 0)),
        ],
        out_specs=pl.BlockSpec(memory_space=pl.ANY),
        out_shape=jax.ShapeDtypeStruct((n_rows, ROW_CHUNKS, LANES), tokens.dtype),
        scratch_shapes=[pltpu.VMEM((t_ffn, ROW_CHUNKS, LANES), tokens.dtype), pltpu.SemaphoreType.DMA,
                        pltpu.SemaphoreType.DMA],
        compiler_params=_cparams(("arbitrary",)),
        name="dispatch",
    )(dest3, zlo, tokens)


def _ffn_kernel(texp_ref, nused_ref, enext_ref, eord_ref, x_ref, wg_hbm, wu_hbm, wd_hbm, y_ref,
                wgf_ref, wuf_ref, wdf_ref, wgb_ref, wub_ref, wdb_ref, wsem, *, layer):
    j = pl.program_id(0)
    e = texp_ref[j]
    active = j < nused_ref[0]
    first = (j == 0) | (e != texp_ref[jnp.maximum(j - 1, 0)])
    slot = eord_ref[e] % 2
    pairs = ((wg_hbm, wgf_ref), (wu_hbm, wuf_ref), (wd_hbm, wdf_ref))

    def wcopies(expert, s):
        return [pltpu.make_async_copy(hbm.at[layer, expert], buf.at[s], wsem.at[s, i])
                for i, (hbm, buf) in enumerate(pairs)]

    @pl.when(active & (j == 0))
    def _():
        for cp in wcopies(e, slot):
            cp.start()

    @pl.when(active & first)
    def _():
        nxt = enext_ref[e]

        @pl.when(nxt >= 0)
        def _():
            for cp in wcopies(nxt, 1 - slot):
                cp.start(priority=1)

        for cp in wcopies(e, slot):
            cp.wait()
        wgb_ref[...] = wgf_ref[slot].astype(BF16)
        wub_ref[...] = wuf_ref[slot].astype(BF16)
        wdb_ref[...] = wdf_ref[slot].astype(BF16)

    @pl.when(active)
    def _():
        x = _load_chunked(x_ref, x_ref.shape[0] // ROW_CHUNKS).astype(BF16)
        gate = _dot(x, wgb_ref[...])
        up = _dot(x, wub_ref[...])
        act = (gate * jax.nn.sigmoid(gate) * up).astype(BF16)
        _store_chunked(y_ref, _dot(act, wdb_ref[...]))


def _expert_ffn(xs, plan, w_gate, w_up, w_down, layer, t_ffn):
    tile_expert, n_used, e_next, e_ord = plan
    _, _, d, de = w_gate.shape
    n_rows = xs.shape[0] // ROW_CHUNKS
    n_tiles = n_rows // t_ffn
    rowmap = lambda j, te, nu, en, eo: (jnp.minimum(j, nu[0] - 1), 0)
    hbm = pl.BlockSpec(memory_space=pl.ANY)
    return pl.pallas_call(
        functools.partial(_ffn_kernel, layer=layer),
        grid_spec=pltpu.PrefetchScalarGridSpec(
            num_scalar_prefetch=4,
            grid=(n_tiles,),
            in_specs=[pl.BlockSpec((t_ffn * ROW_CHUNKS, LANES), rowmap), hbm, hbm, hbm],
            out_specs=pl.BlockSpec((t_ffn * ROW_CHUNKS, LANES), rowmap),
            scratch_shapes=[pltpu.VMEM((2, d, de), F32), pltpu.VMEM((2, d, de), F32), pltpu.VMEM((2, de, d), F32),
                            pltpu.VMEM((d, de), BF16), pltpu.VMEM((d, de), BF16), pltpu.VMEM((de, d), BF16),
                            pltpu.SemaphoreType.DMA((2, 3))],
        ),
        out_shape=jax.ShapeDtypeStruct((n_rows * ROW_CHUNKS, LANES), F32),
        compiler_params=_cparams(("arbitrary",)),
        name="expert_ffn",
    )(tile_expert, n_used, e_next, e_ord, xs, w_gate, w_up, w_down)


def _combine_kernel(dest_ref, dest_next_ref, y_hbm, wts_ref, x1_ref, mod_ref, lng_ref, lnb_ref, x2_ref,
                    buf_ref, sem, *, alpha):
    i = pl.program_id(0)
    n_steps = pl.num_programs(0)
    tc, d = x1_ref.shape

    def row_copy(idx_ref, parity, n, slot, rows):
        return pltpu.make_async_copy(y_hbm.at[idx_ref[0, slot, n]], buf_ref.at[parity * TOP_K + slot, rows],
                                     sem.at[parity])

    @pl.when(i == 0)
    def _():
        def body(n, carry):
            rows = pl.ds(pl.multiple_of(n * ROW_CHUNKS, ROW_CHUNKS), ROW_CHUNKS)
            for slot in range(TOP_K):
                row_copy(dest_ref, 0, n, slot, rows).start(priority=slot)
            return carry
        lax.fori_loop(0, tc, body, 0, unroll=DMA_UNROLL)

    for parity in range(2):
        @pl.when((i + 1 < n_steps) & ((i + 1) % 2 == parity))
        def _():
            for n in range(tc):
                for slot in range(TOP_K):
                    row_copy(dest_next_ref, parity, n, slot, pl.ds(n * ROW_CHUNKS, ROW_CHUNKS)).start(priority=slot)

    for slot in range(TOP_K):
        pltpu.make_async_copy(y_hbm.at[pl.ds(0, tc)], y_hbm.at[pl.ds(0, tc)], sem.at[i % 2]).wait()
    w = wts_ref[...]
    cur = (i % 2) * TOP_K
    f = w[:, 0:1] * _load_chunked(buf_ref.at[cur], tc) + w[:, 1:2] * _load_chunked(buf_ref.at[cur + 1], tc)
    g2 = mod_ref[0][:, 5 * d:6 * d]
    x2_ref[...] = _post_norm(x1_ref[...], g2, f, alpha) * lng_ref[...] + lnb_ref[...]


def _combine(y, dest3, wts_t, x1, mods, lng, lnb, *, tok0, seq_len, mod_row, alpha):
    n, d = x1.shape
    tc = dest3.shape[2]
    tps = max(seq_len // tc, 1)
    steps = n // tc
    blk0 = tok0 // tc
    mod_map = (lambda i: (i // tps, 0, 0)) if mod_row is None else (lambda i: (mod_row, 0, 0))
    cur = lambda i: (blk0 + i, 0, 0)
    nxt = lambda i: (blk0 + jnp.minimum(i + 1, steps - 1), 0, 0)
    smem = pltpu.SMEM
    return pl.pallas_call(
        functools.partial(_combine_kernel, alpha=alpha),
        grid=(steps,),
        in_specs=[
            pl.BlockSpec((1, TOP_K, tc), cur, memory_space=smem),
            pl.BlockSpec((1, TOP_K, tc), nxt, memory_space=smem),
            pl.BlockSpec(memory_space=pl.ANY),
            pl.BlockSpec((tc, TOP_K), lambda i: (blk0 + i, 0)),
            pl.BlockSpec((tc, d), lambda i: (i, 0)),
            pl.BlockSpec((1, 1, mods.shape[2]), mod_map),
            pl.BlockSpec(lng.shape, lambda i: (0, 0)),
            pl.BlockSpec(lnb.shape, lambda i: (0, 0)),
        ],
        out_specs=pl.BlockSpec((tc, d), lambda i: (i, 0)),
        out_shape=jax.ShapeDtypeStruct((n, d), F32),
        scratch_shapes=[pltpu.VMEM((2 * TOP_K, tc * ROW_CHUNKS, LANES), F32), pltpu.SemaphoreType.DMA((2,))],
        compiler_params=_cparams(("arbitrary",)),
        name="combine",
    )(dest3, dest3, y, wts_t, x1, mods, lng, lnb)


def _rope_tables(seq_len):
    t = jnp.arange(seq_len)
    row = (t // GRID_W).astype(F32)
    col = (t % GRID_W).astype(F32)
    half = HEAD_DIM // 2
    inv_freq = ROPE_BASE ** (-jnp.arange(0, half, 2, dtype=F32) / half)
    ang_r, ang_c = row[:, None] * inv_freq, col[:, None] * inv_freq
    cr, sr, cc, sc = jnp.cos(ang_r), jnp.sin(ang_r), jnp.cos(ang_c), jnp.sin(ang_c)
    cos = jnp.concatenate([cr, cr, cc, cc], axis=1)
    sin = jnp.concatenate([-sr, sr, -sc, sc], axis=1)
    return jnp.tile(cos, (1, LANES // HEAD_DIM)), jnp.tile(sin, (1, LANES // HEAD_DIM))


def _pair_heads(a, axis):
    shp = a.shape
    a = a.reshape(*shp[:axis], N_KV_HEADS, Q_REP, HEAD_DIM, *shp[axis + 1:])
    a = jnp.swapaxes(a, axis, axis + 1)
    return a.reshape(shp)


def _moe_plan(eid3, rank3, counts, n_tok, t_ffn):
    counts = counts.reshape(EXPERTS_PER_GROUP, N_EXPERT_GROUPS).T.reshape(N_EXPERTS).astype(jnp.int32)
    padded = (counts + t_ffn - 1) // t_ffn * t_ffn
    pends = jnp.cumsum(padded)
    pstarts = (pends - padded).astype(jnp.int32)
    experts = jnp.arange(N_EXPERTS, dtype=jnp.int32)
    dest3 = rank3 + jnp.sum(jnp.where(eid3[..., None] == experts, pstarts, 0), axis=-1)
    n_rows = -(-(n_tok * TOP_K) // t_ffn) * t_ffn + N_EXPERTS * t_ffn
    n_tiles = n_rows // t_ffn
    tile_row0 = jnp.arange(n_tiles, dtype=jnp.int32) * t_ffn
    tile_expert = jnp.minimum(jnp.sum(pends[None, :] <= tile_row0[:, None], axis=1), N_EXPERTS - 1)
    n_used = (pends[-1] // t_ffn).astype(jnp.int32).reshape(1)
    nonempty = counts > 0
    later = (experts[None, :] > experts[:, None]) & nonempty[None, :]
    e_next = jnp.min(jnp.where(later, experts[None, :], N_EXPERTS), axis=1)
    e_next = jnp.where(e_next < N_EXPERTS, e_next, -1).astype(jnp.int32)
    e_ord = (jnp.cumsum(nonempty.astype(jnp.int32)) - 1).astype(jnp.int32)
    zlo = jnp.where(nonempty, pends - t_ffn, -1).astype(jnp.int32)
    plan = (tile_expert.astype(jnp.int32), n_used, e_next, e_ord)
    return dest3.astype(jnp.int32), plan, zlo, n_rows


def kernel(x, c, ctx, c_ctx, w_ada, b_ada, w_in, w_pool_grp, pool_scale, w_pool_br, w_attn_br, attn_sink,
           w_o, ln1_g, ln1_b, w_router, router_bias, w_exp_gate, w_exp_up, w_exp_down, ln2_g, ln2_b):
    bsz, seq, d = x.shape
    assert d == ROW_CHUNKS * LANES and bsz < SUBLANES
    c_len = ctx.shape[1]
    depth = w_in.shape[0]
    n_lat, n_ctx = bsz * seq, bsz * c_len
    alpha = (2 * depth) ** 0.25

    cond = jnp.zeros((SUBLANES, d), F32).at[:bsz].set(c).at[bsz].set(c_ctx)
    ada = _ada_terms(cond, w_ada, b_ada)
    cos, sin = _rope_tables(seq)
    wr_t = w_router.reshape(d, N_EXPERT_GROUPS, EXPERTS_PER_GROUP).transpose(2, 1, 0).reshape(N_EXPERTS, d)
    wr_t = wr_t.astype(BF16)
    bias_col = router_bias.reshape(N_EXPERT_GROUPS, EXPERTS_PER_GROUP).T.reshape(N_EXPERTS, 1).astype(F32)

    xl = x.reshape(n_lat, d)
    xc = ctx.reshape(n_ctx, d)
    for l in range(depth):
        ctx_out = l < depth - 1
        mods = ada[l].reshape(SUBLANES, 1, 6 * d)
        w_l = w_in[l]
        w_inp = jnp.concatenate([w_l[:, :COL_Q], _pair_heads(w_l[:, COL_Q:COL_K], 1), w_l[:, COL_K:]],
                                axis=1).astype(BF16)
        sink_col = jnp.broadcast_to(attn_sink[l].reshape(N_KV_HEADS, Q_REP, 1, 1),
                                    (N_KV_HEADS, Q_REP, BLOCK, 1)).reshape(N_KV_HEADS, Q_REP * BLOCK, 1)
        sink_col = (sink_col * LOG2E).astype(F32)
        mix_w = (w_pool_grp[l].astype(BF16), pool_scale[l].reshape(1, POOL_W), w_pool_br[l].astype(BF16),
                 _pair_heads(w_attn_br[l], 0).astype(BF16), w_o[l].astype(BF16),
                 ln1_g[l].reshape(1, d), ln1_b[l].reshape(1, d), wr_t)
        lng2, lnb2 = ln2_g[l].reshape(1, d), ln2_b[l].reshape(1, d)

        if ctx_out:
            uc, qc, kc, vc, gc = _inproj(xc, mods, w_inp, cos, sin, seq_len=c_len, mod_row=bsz, rope=False)
        else:
            kc, vc = _inproj(xc, mods, w_inp, cos, sin, seq_len=c_len, mod_row=bsz, rope=False, kv_only=True)
        kc3, vc3 = kc.reshape(bsz, c_len, KV_W), vc.reshape(bsz, c_len, KV_W)
        u, q, k, v, g = _inproj(xl, mods, w_inp, cos, sin, seq_len=seq, mod_row=None, rope=True)
        n_tok = n_lat + n_ctx if ctx_out else n_lat
        tokens = None
        if ctx_out:
            attn_c = _attention(qc, kc, vc, kc3, vc3, sink_col, seq_len=c_len, local=False)
            xc1, tokens, logits_c = _merge(uc, attn_c, gc, xc, mods, mix_w, seq_len=c_len, mod_row=bsz,
                                           alpha=alpha, h2_tokens=n_tok, h2_tok0=n_lat)
        attn = _attention(q, k, v, kc3, vc3, sink_col, seq_len=seq, local=True)
        x1, tokens, logits = _merge(u, attn, g, xl, mods, mix_w, seq_len=seq, mod_row=None, alpha=alpha,
                                    h2_tokens=n_tok, h2_tok0=0, h2_buf=tokens)
        if ctx_out:
            logits = jnp.concatenate([logits, logits_c], axis=1)

        eid3, wts, rank3, counts = _route(logits, bias_col)
        dest3, plan, zlo, n_rows = _moe_plan(eid3, rank3, counts[:, 0], n_tok, T_FFN)
        xs = _dispatch(tokens, dest3, zlo, n_rows, T_FFN)
        y = _expert_ffn(xs.reshape(n_rows * ROW_CHUNKS, LANES), plan, w_exp_gate, w_exp_up, w_exp_down, l, T_FFN)
        y3 = y.reshape(n_rows, ROW_CHUNKS, LANES)
        wts_t = wts.T
        xl = _combine(y3, dest3, wts_t, x1, mods, lng2, lnb2, tok0=0, seq_len=seq, mod_row=None, alpha=alpha)
        if ctx_out:
            xc = _combine(y3, dest3, wts_t, xc1, mods, lng2, lnb2, tok0=n_lat, seq_len=c_len, mod_row=bsz,
                          alpha=alpha)
    return xl.reshape(bsz, seq, d)
```

```python
import functools

import jax
import jax.numpy as jnp
from jax import lax
from jax.experimental import pallas as pl
from jax.experimental.pallas import tpu as pltpu

F32 = jnp.float32
BF16 = jnp.bfloat16

GRID_W = 64
POOL_WINDOWS = (2, 4, 8, 16)
POOL_GROUP_W = 128
POOL_W = 512
HEAD_DIM = 64
N_HEADS = 8
N_KV_HEADS = 2
Q_REP = N_HEADS // N_KV_HEADS
ATTN_W = N_HEADS * HEAD_DIM
KV_W = N_KV_HEADS * HEAD_DIM
BLOCK = 128
ROPE_BASE = 10000.0
COL_POOL = 0
COL_Q = COL_POOL + POOL_W
COL_K = COL_Q + ATTN_W
COL_V = COL_K + KV_W
COL_GATE = COL_V + KV_W
N_EXPERTS = 32
N_EXPERT_GROUPS = 8
EXPERTS_PER_GROUP = N_EXPERTS // N_EXPERT_GROUPS
TOP_K = 2
LN_EPS = 1e-6
LOG2E = 1.4426950408889634

LANES = 128
SUBLANES = 8
POOL_HALO = SUBLANES
ROW_CHUNKS = SUBLANES
VMEM_LIMIT = 56 * 1024 * 1024

ADA_COL_TILES = 4
TM_IN = 1024
TQ_ATT = 2048
TQ_MIX = 512
TN_TOK = 512
T_FFN = 256
DMA_UNROLL = 8


def _cparams(sem):
    return pltpu.CompilerParams(dimension_semantics=sem, vmem_limit_bytes=VMEM_LIMIT)


def _layer_norm(x, eps=LN_EPS):
    mu = jnp.mean(x, axis=-1, keepdims=True)
    xc = x - mu
    var = jnp.mean(xc * xc, axis=-1, keepdims=True)
    return xc * lax.rsqrt(var + eps)


def _post_norm(x, gate, y, alpha):
    return _layer_norm(x + (gate * (1.0 / alpha)) * y, LN_EPS / (alpha * alpha))


def _dot(a, b):
    return jnp.dot(a, b, preferred_element_type=F32)


def _store_chunked(ref, val):
    t = val.shape[0]
    for s in range(ROW_CHUNKS):
        ref[pl.ds(s, t, stride=ROW_CHUNKS), :] = val[:, s * LANES:(s + 1) * LANES]


def _load_chunked(ref, t):
    return jnp.concatenate([ref[pl.ds(s, t, stride=ROW_CHUNKS), :] for s in range(ROW_CHUNKS)], axis=1)


def _ada_kernel(cond_ref, w_ref, b_ref, o_ref):
    s = cond_ref[...]
    s = s * jax.nn.sigmoid(s)
    o_ref[0] = _dot(s.astype(BF16), w_ref[0].astype(BF16)) + b_ref[0]


def _ada_terms(cond, w_ada, b_ada):
    depth, d, n6 = w_ada.shape
    rows = cond.shape[0]
    tn = n6 // ADA_COL_TILES
    return pl.pallas_call(
        _ada_kernel,
        grid=(depth, ADA_COL_TILES),
        in_specs=[
            pl.BlockSpec((rows, d), lambda l, j: (0, 0)),
            pl.BlockSpec((1, d, tn), lambda l, j: (l, 0, j)),
            pl.BlockSpec((1, 1, tn), lambda l, j: (l, 0, j)),
        ],
        out_specs=pl.BlockSpec((1, rows, tn), lambda l, j: (l, 0, j)),
        out_shape=jax.ShapeDtypeStruct((depth, rows, n6), F32),
        compiler_params=_cparams(("arbitrary", "arbitrary")),
        name="ada_terms",
    )(cond, w_ada, b_ada.reshape(depth, 1, n6))


def _rope(t, cos, sin):
    lane = lax.broadcasted_iota(jnp.int32, (1, LANES), 1)
    first = (lane % 32) < 16
    outs = []
    for j in range(t.shape[1] // LANES):
        tj = t[:, j * LANES:(j + 1) * LANES]
        partner = jnp.where(first, pltpu.roll(tj, LANES - 16, 1), pltpu.roll(tj, 16, 1))
        outs.append(tj * cos + partner * sin)
    return outs[0] if len(outs) == 1 else jnp.concatenate(outs, axis=1)


def _inproj_kernel(x_ref, mod_ref, w_ref, cos_ref, sin_ref, *out_refs, rope, kv_only):
    d = x_ref.shape[1]
    mod = mod_ref[0]
    shift, scale = mod[:, 0:d], mod[:, d:2 * d]
    h = (_layer_norm(x_ref[...]) * (1.0 + scale) + shift).astype(BF16)

    def proj(lo, hi):
        return _dot(h, w_ref[:, lo:hi])

    if kv_only:
        k_ref, v_ref = out_refs
    else:
        u_ref, q_ref, k_ref, v_ref, g_ref = out_refs
        u_ref[...] = proj(COL_POOL, COL_Q)
        q = proj(COL_Q, COL_K)
        if rope:
            q = _rope(q, cos_ref[...], sin_ref[...])
        q_ref[...] = (q * (LOG2E * HEAD_DIM ** -0.5)).astype(BF16)
        g_ref[...] = jax.nn.sigmoid(proj(COL_GATE, w_ref.shape[1])).astype(g_ref.dtype)
    kv = proj(COL_K, COL_GATE)
    k = kv[:, 0:KV_W]
    if rope:
        k = _rope(k, cos_ref[...], sin_ref[...])
    k_ref[...] = k.astype(BF16)
    v_ref[...] = kv[:, KV_W:2 * KV_W].astype(BF16)


def _inproj(x2d, mods, w_in, cos, sin, *, seq_len, mod_row, rope, kv_only=False):
    n, d = x2d.shape
    tm = min(TM_IN, seq_len)
    tps = seq_len // tm
    n_cols = w_in.shape[1]
    mod_map = (lambda i: (i // tps, 0, 0)) if mod_row is None else (lambda i: (mod_row, 0, 0))
    tab_map = (lambda i: (i % tps, 0)) if rope else (lambda i: (0, 0))
    row = lambda i: (i, 0)
    kv_shapes = [jax.ShapeDtypeStruct((n, KV_W), BF16)] * 2
    kv_specs = [pl.BlockSpec((tm, KV_W), row)] * 2
    if kv_only:
        out_shape, out_specs = kv_shapes, kv_specs
    else:
        out_shape = [jax.ShapeDtypeStruct((n, POOL_W), F32), jax.ShapeDtypeStruct((n, ATTN_W), BF16),
                     *kv_shapes, jax.ShapeDtypeStruct((n, n_cols - COL_GATE), BF16)]
        out_specs = [pl.BlockSpec((tm, POOL_W), row), pl.BlockSpec((tm, ATTN_W), row),
                     *kv_specs, pl.BlockSpec((tm, n_cols - COL_GATE), row)]
    return pl.pallas_call(
        functools.partial(_inproj_kernel, rope=rope, kv_only=kv_only),
        grid=(n // tm,),
        in_specs=[
            pl.BlockSpec((tm, d), row),
            pl.BlockSpec((1, 1, mods.shape[2]), mod_map),
            pl.BlockSpec((d, n_cols), lambda i: (0, 0)),
            pl.BlockSpec((tm, LANES), tab_map),
            pl.BlockSpec((tm, LANES), tab_map),
        ],
        out_specs=out_specs,
        out_shape=out_shape,
        compiler_params=_cparams(("parallel",)),
        name="inproj_kv" if kv_only else "inproj",
    )(x2d, mods, w_in, cos, sin)


def _merge_kernel(u_prev_ref, u_ref, u_next_ref, attn_ref, g_ref, x_ref, mod_ref,
                  wgrp_ref, pscale_ref, wpool_ref, wattn_ref, wo_ref, lng_ref, lnb_ref, wr_ref,
                  *rest, seq_len, alpha, n_tiles):
    x1_ref, h2_ref, logit_ref, uext_ref, m0_ref, m1_ref = rest[-6:]
    tq, d = x_ref.shape
    tps = seq_len // tq
    j = pl.program_id(0)
    t_in_seq = jnp.minimum(j, n_tiles - 1) % tps
    is_first = t_in_seq == 0
    is_last = t_in_seq == tps - 1
    n_groups = len(POOL_WINDOWS)
    cw = d // n_groups
    row_chunks = tq // BLOCK

    def a_fill_window():
        top = 2 * POOL_HALO
        uext_ref[0:POOL_HALO, :] = jnp.zeros((POOL_HALO, POOL_W), F32)
        uext_ref[POOL_HALO:top, :] = jnp.where(is_first, 0.0, u_prev_ref[...])
        uext_ref[top:top + tq, :] = u_ref[...]
        uext_ref[top + tq:top + tq + POOL_HALO, :] = jnp.where(is_last, 0.0, u_next_ref[...])
        uext_ref[top + tq + POOL_HALO:, :] = jnp.zeros((BLOCK - top - POOL_HALO, POOL_W), F32)

    def a_pool_group(gi):
        w = POOL_WINDOWS[gi]
        top = 2 * POOL_HALO
        cols = slice(gi * POOL_GROUP_W, (gi + 1) * POOL_GROUP_W)
        ue = uext_ref[:, cols]
        ue_hi = ue.astype(BF16)
        ue_lo = (ue - ue_hi.astype(F32)).astype(BF16)
        row_t = lax.broadcasted_iota(jnp.int32, (BLOCK, 2 * BLOCK), 0)
        col_j = lax.broadcasted_iota(jnp.int32, (BLOCK, 2 * BLOCK), 1)
        first_j = row_t + top - w // 2
        band = jnp.where((col_j >= first_j) & (col_j < first_j + w), 1.0, 0.0).astype(BF16)
        sums = []
        for b in range(row_chunks):
            win = slice(b * BLOCK, (b + 2) * BLOCK)
            sums.append(_dot(band, ue_hi[win]) + _dot(band, ue_lo[win]))
        acc = sums[0] if len(sums) == 1 else jnp.concatenate(sums, axis=0)
        pos = t_in_seq * tq + lax.broadcasted_iota(jnp.int32, (tq, 1), 0)
        lo = jnp.maximum(pos - w // 2, 0)
        hi = jnp.minimum(pos - w // 2 + w - 1, seq_len - 1)
        mean = acc / (hi - lo + 1).astype(F32)
        pg = (mean - u_ref[:, cols]).astype(BF16)
        return _dot(pg, wgrp_ref[gi])

    def a_merge_chunk(c, pool_lat, m_buf):
        cols = slice(c * cw, (c + 1) * cw)
        pool_proj = _dot(pool_lat, wpool_ref[:, cols])
        attn_proj = _dot(attn_ref[...], wattn_ref[:, cols])
        m_buf[:, cols] = (g_ref[:, c * cw:(c + 1) * cw].astype(F32) * pool_proj
                          + g_ref[:, d + c * cw:d + (c + 1) * cw].astype(F32) * attn_proj).astype(BF16)

    def b_project_chunk(c, m_buf):
        return _dot(m_buf[...], wo_ref[:, c * cw:(c + 1) * cw])

    def b_norm_rows(r, ys):
        rows = slice(r * BLOCK, (r + 1) * BLOCK)
        y = jnp.concatenate([yc[rows] for yc in ys], axis=1)
        mod = mod_ref[0]
        g1 = mod[:, 2 * d:3 * d]
        sh2, sc2 = mod[:, 3 * d:4 * d], mod[:, 4 * d:5 * d]
        x1 = _post_norm(x_ref[rows, :], g1, y, alpha) * lng_ref[...] + lnb_ref[...]
        x1_ref[rows, :] = x1
        h2 = _layer_norm(x1) * (1.0 + sc2) + sh2
        _store_chunked(h2_ref.at[r * BLOCK * ROW_CHUNKS:(r + 1) * BLOCK * ROW_CHUNKS], h2)
        logit_ref[:, rows] = lax.dot_general(wr_ref[...], h2.astype(BF16), (((1,), (1,)), ((), ())),
                                             preferred_element_type=F32)

    def run(a_buf, b_buf):
        if a_buf is not None:
            a_fill_window()
        ys, pooled = [], []
        for c in range(n_groups):
            if b_buf is not None:
                ys.append(b_project_chunk(c, b_buf))
            if a_buf is not None:
                pooled.append(a_pool_group(c))
        if a_buf is not None:
            pool_lat = (jnp.concatenate(pooled, axis=1) * pscale_ref[...]).astype(BF16)
        for c in range(n_groups):
            if a_buf is not None:
                a_merge_chunk(c, pool_lat, a_buf)
            if b_buf is not None and c < row_chunks:
                b_norm_rows(c, ys)

    bufs = (m0_ref, m1_ref)

    @pl.when(j == 0)
    def _():
        run(bufs[0], None)

    for parity in range(2):
        @pl.when((j >= 1) & (j < n_tiles) & (j % 2 == parity))
        def _():
            run(bufs[parity], bufs[1 - parity])

    @pl.when(j == n_tiles)
    def _():
        run(None, bufs[(n_tiles - 1) % 2])


def _attn_kernel(q_ref, k_prev_ref, k_ref, k_next_ref, v_prev_ref, v_ref, v_next_ref, kc_ref, vc_ref,
                 sink_ref, attn_ref, s0_ref, s1_ref, kg_ref, vg_ref, *, seq_len, local):
    tq = q_ref.shape[0]
    nb = tq // BLOCK
    tps = seq_len // tq
    t_in_seq = pl.program_id(0) % tps
    is_first = t_in_seq == 0
    is_last = t_in_seq == tps - 1
    lane = lax.broadcasted_iota(jnp.int32, (1, LANES), 1)
    lo_half = lane < HEAD_DIM
    one = jnp.ones((), BF16)
    kc = kc_ref[0]
    vc = vc_ref[0]
    kc_g = [jnp.where(lo_half, kc, 0), jnp.where(lo_half, 0, kc)]
    vc_g = [jnp.where(lo_half, vc, one), jnp.where(lo_half, one, vc)]
    neg = jnp.float32(-jnp.inf)
    if local:
        k_ext = jnp.concatenate([k_prev_ref[...], k_ref[...], k_next_ref[...]], axis=0)
        v_ext = jnp.concatenate([v_prev_ref[...], v_ref[...], v_next_ref[...]], axis=0)
        kg_ref[0] = jnp.where(lo_half, k_ext, 0)
        kg_ref[1] = jnp.where(lo_half, 0, k_ext)
        vg_ref[0] = jnp.where(lo_half, v_ext, one)
        vg_ref[1] = jnp.where(lo_half, one, v_ext)
        qq = lax.broadcasted_iota(jnp.int32, (Q_REP * BLOCK, BLOCK), 0) % BLOCK
        kk = lax.broadcasted_iota(jnp.int32, (Q_REP * BLOCK, BLOCK), 1)
        mask_prev = jnp.where(kk >= qq, 0.0, neg)
        mask_next = jnp.where(kk <= qq, 0.0, neg)

    def block_rows(b):
        return pl.ds(b * BLOCK, BLOCK) if isinstance(b, int) else pl.ds(pl.multiple_of(b * BLOCK, BLOCK), BLOCK)

    def band_keys(b):
        start = b * BLOCK if isinstance(b, int) else pl.multiple_of(b * BLOCK, BLOCK)
        return pl.ds(start, 3 * BLOCK)

    def stage_a(b, s_buf, first_block, last_block):
        q_st = jnp.concatenate([q_ref[block_rows(b), c * LANES:(c + 1) * LANES] for c in range(Q_REP)], axis=0)
        for g in range(N_KV_HEADS):
            k_all = jnp.concatenate([kg_ref[g, band_keys(b), :], kc_g[g]], axis=0) if local else kc_g[g]
            s = lax.dot_general(q_st, k_all, (((1,), (1,)), ((), ())), preferred_element_type=F32)
            if local:
                m_prev = jnp.where(first_block, neg, mask_prev)
                m_next = jnp.where(last_block, neg, mask_next)
                s = jnp.concatenate([s[:, 0:BLOCK] + m_prev, s[:, BLOCK:2 * BLOCK],
                                     s[:, 2 * BLOCK:3 * BLOCK] + m_next, s[:, 3 * BLOCK:]], axis=1)
            s_buf[g] = s

    def stage_b(b, s_buf):
        pv, sink_w = [], []
        for g in range(N_KV_HEADS):
            v_all = jnp.concatenate([vg_ref[g, band_keys(b), :], vc_g[g]], axis=0) if local else vc_g[g]
            s = s_buf[g]
            sk = sink_ref[g]
            m = jnp.maximum(jnp.max(s, axis=-1, keepdims=True), sk)
            p = jnp.exp2((s - m).astype(BF16))
            pv.append(_dot(p, v_all))
            sink_w.append(jnp.exp2(sk - m))
        num = jnp.where(lo_half, pv[0], pv[1])
        den = pltpu.roll(jnp.where(lo_half, pv[1], pv[0]), HEAD_DIM, 1) + jnp.where(lo_half, sink_w[0], sink_w[1])
        o = num / den
        for c in range(Q_REP):
            attn_ref[block_rows(b), c * LANES:(c + 1) * LANES] = o[c * BLOCK:(c + 1) * BLOCK].astype(BF16)

    stage_a(0, s0_ref, is_first, False)

    def body(j, carry):
        stage_a(2 * j + 1, s1_ref, False, False)
        stage_b(2 * j, s0_ref)
        stage_a(2 * j + 2, s0_ref, False, False)
        stage_b(2 * j + 1, s1_ref)
        return carry

    lax.fori_loop(0, nb // 2 - 1, body, 0)
    stage_a(nb - 1, s1_ref, False, is_last)
    stage_b(nb - 2, s0_ref)
    stage_b(nb - 1, s1_ref)


def _attention(q, k, v, kc, vc, sink_col, *, seq_len, local):
    n = q.shape[0]
    tq = min(TQ_ATT, seq_len)
    tps = seq_len // tq
    kb = tq // BLOCK
    n_kb = n // BLOCK
    c_len = kc.shape[1]
    n_keys = (3 * BLOCK if local else 0) + c_len
    row = lambda i: (i, 0)
    ctx_map = lambda i: (i // tps, 0, 0)
    kv_prev = pl.BlockSpec((BLOCK, KV_W), lambda i: (jnp.maximum(i * kb - 1, 0), 0))
    kv_cur = pl.BlockSpec((tq, KV_W), row)
    kv_next = pl.BlockSpec((BLOCK, KV_W), lambda i: (jnp.minimum((i + 1) * kb, n_kb - 1), 0))
    return pl.pallas_call(
        functools.partial(_attn_kernel, seq_len=seq_len, local=local),
        grid=(n // tq,),
        in_specs=[
            pl.BlockSpec((tq, ATTN_W), row),
            kv_prev, kv_cur, kv_next, kv_prev, kv_cur, kv_next,
            pl.BlockSpec((1, c_len, KV_W), ctx_map), pl.BlockSpec((1, c_len, KV_W), ctx_map),
            pl.BlockSpec(sink_col.shape, lambda i: (0, 0, 0)),
        ],
        out_specs=pl.BlockSpec((tq, ATTN_W), row),
        out_shape=jax.ShapeDtypeStruct((n, ATTN_W), BF16),
        scratch_shapes=[pltpu.VMEM((N_KV_HEADS, Q_REP * BLOCK, n_keys), F32),
                        pltpu.VMEM((N_KV_HEADS, Q_REP * BLOCK, n_keys), F32),
                        pltpu.VMEM((N_KV_HEADS, tq + 2 * BLOCK, KV_W), BF16),
                        pltpu.VMEM((N_KV_HEADS, tq + 2 * BLOCK, KV_W), BF16)],
        compiler_params=_cparams(("parallel",)),
        name="attention" if local else "attention_ctx",
    )(q, k, k, k, v, v, v, kc, vc, sink_col)


def _merge(u, attn, gates, x2d, mods, wts, *, seq_len, mod_row, alpha, h2_tokens, h2_tok0, h2_buf=None):
    n, d = x2d.shape
    tq = min(TQ_MIX, seq_len)
    h2_blk0 = h2_tok0 // tq
    tps = seq_len // tq
    hb = tq // POOL_HALO
    n_hb = n // POOL_HALO
    n_tiles = n // tq
    ta = lambda i: jnp.minimum(i, n_tiles - 1)
    tb = lambda i: jnp.maximum(i - 1, 0)
    row_a = lambda i: (ta(i), 0)
    row_b = lambda i: (tb(i), 0)
    const2 = lambda i: (0, 0)
    const3 = lambda i: (0, 0, 0)
    mod_map = (lambda i: (tb(i) // tps, 0, 0)) if mod_row is None else (lambda i: (mod_row, 0, 0))
    u_prev = pl.BlockSpec((POOL_HALO, POOL_W), lambda i: (jnp.maximum(ta(i) * hb - 1, 0), 0))
    u_next = pl.BlockSpec((POOL_HALO, POOL_W), lambda i: (jnp.minimum((ta(i) + 1) * hb, n_hb - 1), 0))
    wgrp, pscale, wpool, wattn, wo, lng, lnb, wr_t = wts
    operands = [u, u, u, attn, gates, x2d, mods, wgrp, pscale, wpool, wattn, wo, lng, lnb, wr_t]
    alias_specs, aliases = [], {}
    if h2_buf is not None:
        alias_specs = [pl.BlockSpec(memory_space=pl.ANY)]
        aliases = {len(operands): 1}
        operands.append(h2_buf)
    return pl.pallas_call(
        functools.partial(_merge_kernel, seq_len=seq_len, alpha=alpha, n_tiles=n_tiles),
        grid=(n_tiles + 1,),
        input_output_aliases=aliases,
        in_specs=[
            u_prev, pl.BlockSpec((tq, POOL_W), row_a), u_next,
            pl.BlockSpec((tq, ATTN_W), row_a),
            pl.BlockSpec((tq, 2 * d), row_a),
            pl.BlockSpec((tq, d), row_b),
            pl.BlockSpec((1, 1, mods.shape[2]), mod_map),
            pl.BlockSpec(wgrp.shape, const3), pl.BlockSpec(pscale.shape, const2),
            pl.BlockSpec(wpool.shape, const2), pl.BlockSpec(wattn.shape, const2),
            pl.BlockSpec(wo.shape, const2),
            pl.BlockSpec(lng.shape, const2), pl.BlockSpec(lnb.shape, const2),
            pl.BlockSpec(wr_t.shape, const2),
            *alias_specs,
        ],
        out_specs=[pl.BlockSpec((tq, d), row_b),
                   pl.BlockSpec((tq * ROW_CHUNKS, LANES), lambda i: (h2_blk0 + tb(i), 0)),
                   pl.BlockSpec((N_EXPERTS, tq), lambda i: (0, tb(i)))],
        out_shape=[jax.ShapeDtypeStruct((n, d), F32),
                   jax.ShapeDtypeStruct((h2_tokens * ROW_CHUNKS, LANES), F32),
                   jax.ShapeDtypeStruct((N_EXPERTS, n), F32)],
        scratch_shapes=[pltpu.VMEM((tq + BLOCK, POOL_W), F32), pltpu.VMEM((tq, d), BF16),
                        pltpu.VMEM((tq, d), BF16)],
        compiler_params=_cparams(("arbitrary",)),
        name="merge",
    )(*operands)


def _route_kernel(logit_ref, bias_ref, before_ref, eid_ref, wts_ref, rank_ref, cnt_ref, base_ref):
    tn = logit_ref.shape[1]
    ng, epg = N_EXPERT_GROUPS, EXPERTS_PER_GROUP

    @pl.when(pl.program_id(0) == 0)
    def _():
        base_ref[...] = jnp.zeros_like(base_ref)

    scores = jax.nn.sigmoid(logit_ref[...])
    biased = scores + bias_ref[...]
    bj = [biased[j * ng:(j + 1) * ng] for j in range(epg)]
    sj = [scores[j * ng:(j + 1) * ng] for j in range(epg)]
    hi01, lo01 = jnp.maximum(bj[0], bj[1]), jnp.minimum(bj[0], bj[1])
    hi23, lo23 = jnp.maximum(bj[2], bj[3]), jnp.minimum(bj[2], bj[3])
    gscore = jnp.maximum(hi01, hi23) + jnp.maximum(jnp.minimum(hi01, hi23), jnp.maximum(lo01, lo23))
    giota = lax.broadcasted_iota(jnp.int32, (ng, tn), 0)
    gmax = jnp.max(gscore, axis=0, keepdims=True)
    g_first = jnp.min(jnp.where(gscore == gmax, giota.astype(F32), float(ng)), axis=0, keepdims=True)
    g_sel = g_first.astype(jnp.int32)
    in_g = giota == g_sel
    vb = [jnp.sum(jnp.where(in_g, b, 0.0), axis=0, keepdims=True) for b in bj]
    vs = [jnp.sum(jnp.where(in_g, s, 0.0), axis=0, keepdims=True) for s in sj]

    def first_best(vals):
        best = functools.reduce(jnp.maximum, vals)
        idx = jnp.full(best.shape, epg - 1, jnp.int32)
        for j in range(epg - 2, -1, -1):
            idx = jnp.where(vals[j] == best, j, idx)
        return idx

    def pick(vals, idx):
        out = vals[epg - 1]
        for j in range(epg - 2, -1, -1):
            out = jnp.where(idx == j, vals[j], out)
        return out

    l1 = first_best(vb)
    l2 = first_best([jnp.where(l1 == j, -jnp.inf, vb[j]) for j in range(epg)])
    w1, w2 = pick(vs, l1), pick(vs, l2)
    wsum = w1 + w2
    eid_ref[0, 0:1, :] = g_sel * epg + l1
    eid_ref[0, 1:2, :] = g_sel * epg + l2
    wts_ref[0:1, :] = w1 / wsum
    wts_ref[1:2, :] = w2 / wsum

    r1, r2 = l1 * ng + g_sel, l2 * ng + g_sel
    riota = lax.broadcasted_iota(jnp.int32, (N_EXPERTS, tn), 0)
    hit1, hit2 = riota == r1, riota == r2
    onehot = jnp.where(hit1 | hit2, 1.0, 0.0)
    prefix = _dot(onehot.astype(BF16), before_ref[...]) + base_ref[:, 0:1]
    rank_ref[0, 0:1, :] = jnp.sum(jnp.where(hit1, prefix, 0.0), axis=0, keepdims=True).astype(jnp.int32)
    rank_ref[0, 1:2, :] = jnp.sum(jnp.where(hit2, prefix, 0.0), axis=0, keepdims=True).astype(jnp.int32)
    base_ref[...] = base_ref[...] + jnp.sum(onehot, axis=1, keepdims=True)
    cnt_ref[...] = base_ref[...]


def _route(logits, bias_col):
    n = logits.shape[1]
    tn = TN_TOK
    steps = n // tn
    col = lambda i: (0, i)
    blk = lambda i: (i, 0, 0)
    tok = jnp.arange(tn)
    before = (tok[:, None] < tok[None, :]).astype(BF16)
    return pl.pallas_call(
        _route_kernel,
        grid=(steps,),
        in_specs=[
            pl.BlockSpec((N_EXPERTS, tn), col),
            pl.BlockSpec((N_EXPERTS, 1), lambda i: (0, 0)),
            pl.BlockSpec((tn, tn), lambda i: (0, 0)),
        ],
        out_specs=[pl.BlockSpec((1, TOP_K, tn), blk), pl.BlockSpec((TOP_K, tn), col),
                   pl.BlockSpec((1, TOP_K, tn), blk), pl.BlockSpec((N_EXPERTS, LANES), lambda i: (0, 0))],
        out_shape=[jax.ShapeDtypeStruct((steps, TOP_K, tn), jnp.int32), jax.ShapeDtypeStruct((TOP_K, n), F32),
                   jax.ShapeDtypeStruct((steps, TOP_K, tn), jnp.int32),
                   jax.ShapeDtypeStruct((N_EXPERTS, LANES), F32)],
        scratch_shapes=[pltpu.VMEM((N_EXPERTS, LANES), F32)],
        compiler_params=_cparams(("arbitrary",)),
        name="route",
    )(logits, bias_col, before)


def _dispatch_kernel(dest_ref, zlo_ref, tok_ref, xs_hbm, zero_ref, zsem, sem, *, t_ffn):
    i = pl.program_id(0)
    tn = dest_ref.shape[2]

    @pl.when(i == 0)
    def _():
        zero_ref[...] = jnp.zeros_like(zero_ref)

        def zcopy(e):
            return pltpu.make_async_copy(zero_ref, xs_hbm.at[pl.ds(jnp.maximum(zlo_ref[e], 0), t_ffn)], zsem)

        def start(e, carry):
            @pl.when(zlo_ref[e] >= 0)
            def _():
                zcopy(e).start()
            return carry

        def wait(e, carry):
            @pl.when(zlo_ref[e] >= 0)
            def _():
                zcopy(e).wait()
            return carry

        lax.fori_loop(0, N_EXPERTS, start, 0)
        lax.fori_loop(0, N_EXPERTS, wait, 0)

    for n in range(tn):
        src = tok_ref.at[pl.ds(n * ROW_CHUNKS, ROW_CHUNKS)]
        for slot in range(TOP_K):
            pltpu.make_async_copy(src, xs_hbm.at[dest_ref[0, slot, n]], sem).start(priority=slot)
    for slot in range(TOP_K):
        pltpu.make_async_copy(xs_hbm.at[pl.ds(0, tn)], xs_hbm.at[pl.ds(0, tn)], sem).wait()


def _dispatch(tokens, dest3, zlo, n_rows, t_ffn):
    steps, _, tn = dest3.shape
    smem = pltpu.SMEM
    return pl.pallas_call(
        functools.partial(_dispatch_kernel, t_ffn=t_ffn),
        grid=(steps,),
        in_specs=[
            pl.BlockSpec((1, TOP_K, tn), lambda i: (i, 0, 0), memory_space=smem),
            pl.BlockSpec(memory_space=smem),
            pl.BlockSpec((tn * ROW_CHUNKS, LANES), lambda i: (i, 0)),
        ],
        out_specs=pl.BlockSpec(memory_space=pl.ANY),
        out_shape=jax.ShapeDtypeStruct((n_rows, ROW_CHUNKS, LANES), tokens.dtype),
        scratch_shapes=[pltpu.VMEM((t_ffn, ROW_CHUNKS, LANES), tokens.dtype), pltpu.SemaphoreType.DMA,
                        pltpu.SemaphoreType.DMA],
        compiler_params=_cparams(("arbitrary",)),
        name="dispatch",
    )(dest3, zlo, tokens)


def _ffn_kernel(texp_ref, nused_ref, enext_ref, eord_ref, x_ref, wg_hbm, wu_hbm, wd_hbm, y_ref,
                wgf_ref, wuf_ref, wdf_ref, wgb_ref, wub_ref, wdb_ref, wsem, *, layer):
    j = pl.program_id(0)
    e = texp_ref[j]
    active = j < nused_ref[0]
    first = (j == 0) | (e != texp_ref[jnp.maximum(j - 1, 0)])
    slot = eord_ref[e] % 2
    pairs = ((wg_hbm, wgf_ref), (wu_hbm, wuf_ref), (wd_hbm, wdf_ref))

    def wcopies(expert, s):
        return [pltpu.make_async_copy(hbm.at[layer, expert], buf.at[s], wsem.at[s, i])
                for i, (hbm, buf) in enumerate(pairs)]

    @pl.when(active & (j == 0))
    def _():
        for cp in wcopies(e, slot):
            cp.start()

    @pl.when(active & first)
    def _():
        nxt = enext_ref[e]

        @pl.when(nxt >= 0)
        def _():
            for cp in wcopies(nxt, 1 - slot):
                cp.start(priority=1)

        for cp in wcopies(e, slot):
            cp.wait()
        wgb_ref[...] = wgf_ref[slot].astype(BF16)
        wub_ref[...] = wuf_ref[slot].astype(BF16)
        wdb_ref[...] = wdf_ref[slot].astype(BF16)

    @pl.when(active)
    def _():
        x = _load_chunked(x_ref, x_ref.shape[0] // ROW_CHUNKS).astype(BF16)
        gate = _dot(x, wgb_ref[...])
        up = _dot(x, wub_ref[...])
        act = (gate * jax.nn.sigmoid(gate) * up).astype(BF16)
        _store_chunked(y_ref, _dot(act, wdb_ref[...]))


def _expert_ffn(xs, plan, w_gate, w_up, w_down, layer, t_ffn):
    tile_expert, n_used, e_next, e_ord = plan
    _, _, d, de = w_gate.shape
    n_rows = xs.shape[0] // ROW_CHUNKS
    n_tiles = n_rows // t_ffn
    rowmap = lambda j, te, nu, en, eo: (jnp.minimum(j, nu[0] - 1), 0)
    hbm = pl.BlockSpec(memory_space=pl.ANY)
    return pl.pallas_call(
        functools.partial(_ffn_kernel, layer=layer),
        grid_spec=pltpu.PrefetchScalarGridSpec(
            num_scalar_prefetch=4,
            grid=(n_tiles,),
            in_specs=[pl.BlockSpec((t_ffn * ROW_CHUNKS, LANES), rowmap), hbm, hbm, hbm],
            out_specs=pl.BlockSpec((t_ffn * ROW_CHUNKS, LANES), rowmap),
            scratch_shapes=[pltpu.VMEM((2, d, de), F32), pltpu.VMEM((2, d, de), F32), pltpu.VMEM((2, de, d), F32),
                            pltpu.VMEM((d, de), BF16), pltpu.VMEM((d, de), BF16), pltpu.VMEM((de, d), BF16),
                            pltpu.SemaphoreType.DMA((2, 3))],
        ),
        out_shape=jax.ShapeDtypeStruct((n_rows * ROW_CHUNKS, LANES), F32),
        compiler_params=_cparams(("arbitrary",)),
        name="expert_ffn",
    )(tile_expert, n_used, e_next, e_ord, xs, w_gate, w_up, w_down)


def _combine_kernel(dest_ref, dest_next_ref, y_hbm, wts_ref, x1_ref, mod_ref, lng_ref, lnb_ref, x2_ref,
                    buf_ref, sem, *, alpha):
    i = pl.program_id(0)
    n_steps = pl.num_programs(0)
    tc, d = x1_ref.shape

    def row_copy(idx_ref, parity, n, slot, rows):
        return pltpu.make_async_copy(y_hbm.at[idx_ref[0, slot, n]], buf_ref.at[parity * TOP_K + slot, rows],
                                     sem.at[parity])

    @pl.when(i == 0)
    def _():
        def body(n, carry):
            rows = pl.ds(pl.multiple_of(n * ROW_CHUNKS, ROW_CHUNKS), ROW_CHUNKS)
            for slot in range(TOP_K):
                row_copy(dest_ref, 0, n, slot, rows).start(priority=slot)
            return carry
        lax.fori_loop(0, tc, body, 0, unroll=DMA_UNROLL)

    for parity in range(2):
        @pl.when((i + 1 < n_steps) & ((i + 1) % 2 == parity))
        def _():
            for n in range(tc):
                for slot in range(TOP_K):
                    row_copy(dest_next_ref, parity, n, slot, pl.ds(n * ROW_CHUNKS, ROW_CHUNKS)).start(priority=slot)

    for slot in range(TOP_K):
        pltpu.make_async_copy(y_hbm.at[pl.ds(0, tc)], y_hbm.at[pl.ds(0, tc)], sem.at[i % 2]).wait()
    w = wts_ref[...]
    cur = (i % 2) * TOP_K
    f = w[:, 0:1] * _load_chunked(buf_ref.at[cur], tc) + w[:, 1:2] * _load_chunked(buf_ref.at[cur + 1], tc)
    g2 = mod_ref[0][:, 5 * d:6 * d]
    x2_ref[...] = _post_norm(x1_ref[...], g2, f, alpha) * lng_ref[...] + lnb_ref[...]


def _combine(y, dest3, wts_t, x1, mods, lng, lnb, *, tok0, seq_len, mod_row, alpha):
    n, d = x1.shape
    tc = dest3.shape[2]
    tps = max(seq_len // tc, 1)
    steps = n // tc
    blk0 = tok0 // tc
    mod_map = (lambda i: (i // tps, 0, 0)) if mod_row is None else (lambda i: (mod_row, 0, 0))
    cur = lambda i: (blk0 + i, 0, 0)
    nxt = lambda i: (blk0 + jnp.minimum(i + 1, steps - 1), 0, 0)
    smem = pltpu.SMEM
    return pl.pallas_call(
        functools.partial(_combine_kernel, alpha=alpha),
        grid=(steps,),
        in_specs=[
            pl.BlockSpec((1, TOP_K, tc), cur, memory_space=smem),
            pl.BlockSpec((1, TOP_K, tc), nxt, memory_space=smem),
            pl.BlockSpec(memory_space=pl.ANY),
            pl.BlockSpec((tc, TOP_K), lambda i: (blk0 + i, 0)),
            pl.BlockSpec((tc, d), lambda i: (i, 0)),
            pl.BlockSpec((1, 1, mods.shape[2]), mod_map),
            pl.BlockSpec(lng.shape, lambda i: (0, 0)),
            pl.BlockSpec(lnb.shape, lambda i: (0, 0)),
        ],
        out_specs=pl.BlockSpec((tc, d), lambda i: (i, 0)),
        out_shape=jax.ShapeDtypeStruct((n, d), F32),
        scratch_shapes=[pltpu.VMEM((2 * TOP_K, tc * ROW_CHUNKS, LANES), F32), pltpu.SemaphoreType.DMA((2,))],
        compiler_params=_cparams(("arbitrary",)),
        name="combine",
    )(dest3, dest3, y, wts_t, x1, mods, lng, lnb)


def _rope_tables(seq_len):
    t = jnp.arange(seq_len)
    row = (t // GRID_W).astype(F32)
    col = (t % GRID_W).astype(F32)
    half = HEAD_DIM // 2
    inv_freq = ROPE_BASE ** (-jnp.arange(0, half, 2, dtype=F32) / half)
    ang_r, ang_c = row[:, None] * inv_freq, col[:, None] * inv_freq
    cr, sr, cc, sc = jnp.cos(ang_r), jnp.sin(ang_r), jnp.cos(ang_c), jnp.sin(ang_c)
    cos = jnp.concatenate([cr, cr, cc, cc], axis=1)
    sin = jnp.concatenate([-sr, sr, -sc, sc], axis=1)
    return jnp.tile(cos, (1, LANES // HEAD_DIM)), jnp.tile(sin, (1, LANES // HEAD_DIM))


def _pair_heads(a, axis):
    shp = a.shape
    a = a.reshape(*shp[:axis], N_KV_HEADS, Q_REP, HEAD_DIM, *shp[axis + 1:])
    a = jnp.swapaxes(a, axis, axis + 1)
    return a.reshape(shp)


def _moe_plan(eid3, rank3, counts, n_tok, t_ffn):
    counts = counts.reshape(EXPERTS_PER_GROUP, N_EXPERT_GROUPS).T.reshape(N_EXPERTS).astype(jnp.int32)
    padded = (counts + t_ffn - 1) // t_ffn * t_ffn
    pends = jnp.cumsum(padded)
    pstarts = (pends - padded).astype(jnp.int32)
    experts = jnp.arange(N_EXPERTS, dtype=jnp.int32)
    dest3 = rank3 + jnp.sum(jnp.where(eid3[..., None] == experts, pstarts, 0), axis=-1)
    n_rows = -(-(n_tok * TOP_K) // t_ffn) * t_ffn + N_EXPERTS * t_ffn
    n_tiles = n_rows // t_ffn
    tile_row0 = jnp.arange(n_tiles, dtype=jnp.int32) * t_ffn
    tile_expert = jnp.minimum(jnp.sum(pends[None, :] <= tile_row0[:, None], axis=1), N_EXPERTS - 1)
    n_used = (pends[-1] // t_ffn).astype(jnp.int32).reshape(1)
    nonempty = counts > 0
    later = (experts[None, :] > experts[:, None]) & nonempty[None, :]
    e_next = jnp.min(jnp.where(later, experts[None, :], N_EXPERTS), axis=1)
    e_next = jnp.where(e_next < N_EXPERTS, e_next, -1).astype(jnp.int32)
    e_ord = (jnp.cumsum(nonempty.astype(jnp.int32)) - 1).astype(jnp.int32)
    zlo = jnp.where(nonempty, pends - t_ffn, -1).astype(jnp.int32)
    plan = (tile_expert.astype(jnp.int32), n_used, e_next, e_ord)
    return dest3.astype(jnp.int32), plan, zlo, n_rows


def kernel(x, c, ctx, c_ctx, w_ada, b_ada, w_in, w_pool_grp, pool_scale, w_pool_br, w_attn_br, attn_sink,
           w_o, ln1_g, ln1_b, w_router, router_bias, w_exp_gate, w_exp_up, w_exp_down, ln2_g, ln2_b):
    bsz, seq, d = x.shape
    assert d == ROW_CHUNKS * LANES and bsz < SUBLANES
    c_len = ctx.shape[1]
    depth = w_in.shape[0]
    n_lat, n_ctx = bsz * seq, bsz * c_len
    alpha = (2 * depth) ** 0.25

    cond = jnp.zeros((SUBLANES, d), F32).at[:bsz].set(c).at[bsz].set(c_ctx)
    ada = _ada_terms(cond, w_ada, b_ada)
    cos, sin = _rope_tables(seq)
    wr_t = w_router.reshape(d, N_EXPERT_GROUPS, EXPERTS_PER_GROUP).transpose(2, 1, 0).reshape(N_EXPERTS, d)
    wr_t = wr_t.astype(BF16)
    bias_col = router_bias.reshape(N_EXPERT_GROUPS, EXPERTS_PER_GROUP).T.reshape(N_EXPERTS, 1).astype(F32)

    xl = x.reshape(n_lat, d)
    xc = ctx.reshape(n_ctx, d)
    for l in range(depth):
        ctx_out = l < depth - 1
        mods = ada[l].reshape(SUBLANES, 1, 6 * d)
        w_l = w_in[l]
        w_inp = jnp.concatenate([w_l[:, :COL_Q], _pair_heads(w_l[:, COL_Q:COL_K], 1), w_l[:, COL_K:]],
                                axis=1).astype(BF16)
        sink_col = jnp.broadcast_to(attn_sink[l].reshape(N_KV_HEADS, Q_REP, 1, 1),
                                    (N_KV_HEADS, Q_REP, BLOCK, 1)).reshape(N_KV_HEADS, Q_REP * BLOCK, 1)
        sink_col = (sink_col * LOG2E).astype(F32)
        mix_w = (w_pool_grp[l].astype(BF16), pool_scale[l].reshape(1, POOL_W), w_pool_br[l].astype(BF16),
                 _pair_heads(w_attn_br[l], 0).astype(BF16), w_o[l].astype(BF16),
                 ln1_g[l].reshape(1, d), ln1_b[l].reshape(1, d), wr_t)
        lng2, lnb2 = ln2_g[l].reshape(1, d), ln2_b[l].reshape(1, d)

        if ctx_out:
            uc, qc, kc, vc, gc = _inproj(xc, mods, w_inp, cos, sin, seq_len=c_len, mod_row=bsz, rope=False)
        else:
            kc, vc = _inproj(xc, mods, w_inp, cos, sin, seq_len=c_len, mod_row=bsz, rope=False, kv_only=True)
        kc3, vc3 = kc.reshape(bsz, c_len, KV_W), vc.reshape(bsz, c_len, KV_W)
        u, q, k, v, g = _inproj(xl, mods, w_inp, cos, sin, seq_len=seq, mod_row=None, rope=True)
        n_tok = n_lat + n_ctx if ctx_out else n_lat
        tokens = None
        if ctx_out:
            attn_c = _attention(qc, kc, vc, kc3, vc3, sink_col, seq_len=c_len, local=False)
            xc1, tokens, logits_c = _merge(uc, attn_c, gc, xc, mods, mix_w, seq_len=c_len, mod_row=bsz,
                                           alpha=alpha, h2_tokens=n_tok, h2_tok0=n_lat)
        attn = _attention(q, k, v, kc3, vc3, sink_col, seq_len=seq, local=True)
        x1, tokens, logits = _merge(u, attn, g, xl, mods, mix_w, seq_len=seq, mod_row=None, alpha=alpha,
                                    h2_tokens=n_tok, h2_tok0=0, h2_buf=tokens)
        if ctx_out:
            logits = jnp.concatenate([logits, logits_c], axis=1)

        eid3, wts, rank3, counts = _route(logits, bias_col)
        dest3, plan, zlo, n_rows = _moe_plan(eid3, rank3, counts[:, 0], n_tok, T_FFN)
        xs = _dispatch(tokens, dest3, zlo, n_rows, T_FFN)
        y = _expert_ffn(xs.reshape(n_rows * ROW_CHUNKS, LANES), plan, w_exp_gate, w_exp_up, w_exp_down, l, T_FFN)
        y3 = y.reshape(n_rows, ROW_CHUNKS, LANES)
        wts_t = wts.T
        xl = _combine(y3, dest3, wts_t, x1, mods, lng2, lnb2, tok0=0, seq_len=seq, mod_row=None, alpha=alpha)
        if ctx_out:
            xc = _combine(y3, dest3, wts_t, xc1, mods, lng2, lnb2, tok0=n_lat, seq_len=c_len, mod_row=bsz,
                          alpha=alpha)
    return xl.reshape(bsz, seq, d)
```
